```python
import math
import jax
import jax.numpy as jnp
from jax import lax
import numpy as np

D_MODEL = 1024
BATCH = 8
SEQ = 2048
DEPTH = 2
DEC_BATCH = 128
DEC_SEQ = 4
PAST_LEN = 16384
PAGE_SIZE = 128

D_MIX = D_MODEL
CONV_CH = D_MIX // 2
CONV_WIDTH = 31
CONV_GROUPS = 4
MLSTM_HEADS = 4
MLSTM_WIDTH = D_MIX - CONV_CH
MLSTM_HEAD_DIM = MLSTM_WIDTH // MLSTM_HEADS
MLSTM_CHUNK = 64
D_FF = ((8 * D_MODEL // 3 + 127) // 128) * 128
FFN_CONV_WIDTH = 3
PLE_DIM = 256
EPS = 1e-6

OFF_Q = 2 * CONV_CH
OFF_K = OFF_Q + MLSTM_WIDTH
OFF_V = OFF_K + MLSTM_WIDTH
OFF_O = OFF_V + MLSTM_WIDTH
OFF_I = OFF_O + MLSTM_WIDTH
OFF_F = OFF_I + MLSTM_HEADS
IN_COLS = OFF_F + MLSTM_HEADS

kernel_name = "hymba_conformer_mlstm_convffn_step"


def rmsnorm(x, g):
    xf = x.astype(jnp.float32)
    y = xf * lax.rsqrt(jnp.mean(xf * xf, axis=-1, keepdims=True) + EPS)
    return (y * g.astype(jnp.float32)).astype(x.dtype)


def group_layernorm(x, g, b, groups):
    shp = x.shape
    xf = x.astype(jnp.float32).reshape(*shp[:-1], groups, shp[-1] // groups)
    mu = jnp.mean(xf, axis=-1, keepdims=True)
    var = jnp.mean(jnp.square(xf - mu), axis=-1, keepdims=True)
    y = ((xf - mu) * lax.rsqrt(var + EPS)).reshape(shp)
    return (y * g.astype(jnp.float32) + b.astype(jnp.float32)).astype(x.dtype)


def causal_dwconv(u, buf, w, b):
    width = w.shape[0]
    u_ext = jnp.concatenate([buf.astype(u.dtype), u], axis=1)
    y = lax.conv_general_dilated(
        u_ext, w[:, None, :].astype(u.dtype), window_strides=(1,), padding="VALID",
        dimension_numbers=("NWC", "WIO", "NWC"), feature_group_count=u.shape[-1])
    new_buf = u_ext[:, u_ext.shape[1] - (width - 1):]
    return y + b.astype(u.dtype), new_buf


def mlstm_chunkwise(q, k, v, i_pre, f_pre, C0, n0, m0):
    B, S, H, DH = q.shape
    L = math.gcd(S, MLSTM_CHUNK)
    NC = S // L

    def to_chunks(a):
        return jnp.swapaxes(a.reshape(B, NC, L, *a.shape[2:]), 0, 1)

    log_f = jax.nn.log_sigmoid(f_pre)
    causal = jnp.tril(jnp.ones((L, L), dtype=bool))

    def step(carry, inp):
        C, n, m = carry
        qc, kc, vc, ic, lfc = inp
        bcum = jnp.swapaxes(jnp.cumsum(lfc, axis=1), 1, 2)
        ih = jnp.swapaxes(ic, 1, 2)
        d = bcum[:, :, :, None] - bcum[:, :, None, :] + ih[:, :, None, :]
        d = jnp.where(causal, d, -jnp.inf)
        inter = bcum + m[:, :, None]
        m_t = jnp.maximum(inter, jnp.max(d, axis=-1))
        w_intra = jnp.exp(d - m_t[..., None])
        w_inter = jnp.exp(inter - m_t)
        s = jnp.einsum("blhd,bshd->bhls", qc, kc) * w_intra
        num = (jnp.einsum("bhls,bshe->bhle", s, vc)
               + jnp.einsum("blhd,bhde->bhle", qc, C) * w_inter[..., None])
        den = jnp.sum(s, axis=-1) + jnp.einsum("blhd,bhd->bhl", qc, n) * w_inter
        h = num / jnp.maximum(jnp.abs(den), jnp.exp(-m_t))[..., None]
        b_last = bcum[:, :, -1]
        g = b_last[..., None] - bcum + ih
        m_new = jnp.maximum(b_last + m, jnp.max(g, axis=-1))
        decay = jnp.exp(b_last + m - m_new)
        ws = jnp.exp(g - m_new[..., None])
        C_new = decay[..., None, None] * C + jnp.einsum("bhs,bshd,bshe->bhde", ws, kc, vc)
        n_new = decay[..., None] * n + jnp.einsum("bhs,bshd->bhd", ws, kc)
        return (C_new, n_new, m_new), jnp.swapaxes(h, 1, 2)

    (C1, n1, m1), hs = lax.scan(
        step, (C0, n0, m0),
        (to_chunks(q), to_chunks(k), to_chunks(v), to_chunks(i_pre), to_chunks(log_f)))
    h = jnp.swapaxes(hs, 0, 1).reshape(B, S, H, DH)
    return h, C1, n1, m1


def trunk_layer(x, p, conv_buf, C0, n0, m0, ffn_buf,
                g_mix_pre, w_in, b_igate, b_fgate, w_conv_mix, b_conv_mix, g_conv_norm,
                b_conv_norm, g_mlstm_norm, w_out, g_mix_post, g_ffn_pre, w_up, w_conv_ffn,
                b_conv_ffn, w_down, g_ffn_post, g_ple, w_ple, w_ple_gate):
    B, S, _ = x.shape
    f32 = jnp.float32
    h = rmsnorm(x, g_mix_pre)
    z = h @ w_in

    a = z[..., :CONV_CH] * jax.nn.sigmoid(z[..., CONV_CH:OFF_Q])
    a, conv_buf_new = causal_dwconv(a, conv_buf, w_conv_mix, b_conv_mix)
    a = jax.nn.silu(group_layernorm(a, g_conv_norm, b_conv_norm, CONV_GROUPS))

    def heads(t):
        return t.reshape(B, S, MLSTM_HEADS, MLSTM_HEAD_DIM).astype(f32)
    q = heads(z[..., OFF_Q:OFF_K]) * (MLSTM_HEAD_DIM ** -0.5)
    k = heads(z[..., OFF_K:OFF_V])
    v = heads(z[..., OFF_V:OFF_O])
    i_pre = z[..., OFF_I:OFF_F].astype(f32) + b_igate.astype(f32)
    f_pre = z[..., OFF_F:IN_COLS].astype(f32) + b_fgate.astype(f32)
    hm, C1, n1, m1 = mlstm_chunkwise(q, k, v, i_pre, f_pre,
                                     C0.astype(f32), n0.astype(f32), m0.astype(f32))
    mu = jnp.mean(hm, axis=-1, keepdims=True)
    var = jnp.mean(jnp.square(hm - mu), axis=-1, keepdims=True)
    hm = ((hm - mu) * lax.rsqrt(var + EPS)).reshape(B, S, MLSTM_WIDTH) * g_mlstm_norm.astype(f32)
    hm = (hm * jax.nn.sigmoid(z[..., OFF_O:OFF_I].astype(f32))).astype(x.dtype)

    mix = jnp.concatenate([a, hm], axis=-1) @ w_out
    x = x + rmsnorm(mix, g_mix_post)

    up = rmsnorm(x, g_ffn_pre) @ w_up
    up, ffn_buf_new = causal_dwconv(up, ffn_buf, w_conv_ffn, b_conv_ffn)
    f = jax.nn.gelu(up[..., :D_FF], approximate=True) * up[..., D_FF:]
    x = x + rmsnorm(f @ w_down, g_ffn_post)

    x = x + (p @ w_ple) * jax.nn.sigmoid(rmsnorm(x, g_ple) @ w_ple_gate)
    return (x, conv_buf_new, C1.astype(C0.dtype), n1.astype(n0.dtype), m1.astype(m0.dtype),
            ffn_buf_new)


def setup_inputs(seed: int = 0) -> dict:
    key = jax.random.key(seed)
    ks = jax.random.split(key, 32)
    nrm = jax.random.normal
    f32 = jnp.float32

    def gain(kk, n):
        return 1.0 + 0.05 * nrm(kk, (DEPTH, n), f32)

    return {
        "x_prompt": nrm(ks[0], (BATCH, SEQ, D_MODEL), f32),
        "x_sample": nrm(ks[1], (DEC_BATCH, DEC_SEQ, D_MODEL), f32),
        "p_prompt": nrm(ks[2], (DEPTH, BATCH, SEQ, PLE_DIM), f32),
        "p_sample": nrm(ks[3], (DEPTH, DEC_BATCH, DEC_SEQ, PLE_DIM), f32),
        "state_conv_mix": 0.5 * nrm(ks[4], (DEPTH, DEC_BATCH, CONV_WIDTH - 1, CONV_CH), f32),
        "state_mlstm_C": 0.3 * nrm(ks[5], (DEPTH, DEC_BATCH, MLSTM_HEADS, MLSTM_HEAD_DIM, MLSTM_HEAD_DIM), f32),
        "state_mlstm_n": 0.5 * jnp.abs(nrm(ks[6], (DEPTH, DEC_BATCH, MLSTM_HEADS, MLSTM_HEAD_DIM), f32)),
        "state_mlstm_m": nrm(ks[7], (DEPTH, DEC_BATCH, MLSTM_HEADS), f32),
        "state_conv_ffn": nrm(ks[8], (DEPTH, DEC_BATCH, FFN_CONV_WIDTH - 1, 2 * D_FF), f32),
        "g_mix_pre": gain(ks[9], D_MODEL),
        "w_in": nrm(ks[10], (DEPTH, D_MODEL, IN_COLS), f32) * D_MODEL ** -0.5,
        "b_igate": 0.1 * nrm(ks[11], (DEPTH, MLSTM_HEADS), f32),
        "b_fgate": 3.0 + 0.5 * nrm(ks[12], (DEPTH, MLSTM_HEADS), f32),
        "w_conv_mix": nrm(ks[13], (DEPTH, CONV_WIDTH, CONV_CH), f32) * CONV_WIDTH ** -0.5,
        "b_conv_mix": 0.02 * nrm(ks[14], (DEPTH, CONV_CH), f32),
        "g_conv_norm": gain(ks[15], CONV_CH),
        "b_conv_norm": 0.02 * nrm(ks[16], (DEPTH, CONV_CH), f32),
        "g_mlstm_norm": gain(ks[17], MLSTM_WIDTH),
        "w_out": nrm(ks[18], (DEPTH, D_MIX, D_MODEL), f32) * D_MIX ** -0.5,
        "g_mix_post": gain(ks[19], D_MODEL),
        "g_ffn_pre": gain(ks[20], D_MODEL),
        "w_up": nrm(ks[21], (DEPTH, D_MODEL, 2 * D_FF), f32) * D_MODEL ** -0.5,
        "w_conv_ffn": nrm(ks[22], (DEPTH, FFN_CONV_WIDTH, 2 * D_FF), f32) * FFN_CONV_WIDTH ** -0.5,
        "b_conv_ffn": 0.02 * nrm(ks[23], (DEPTH, 2 * D_FF), f32),
        "w_down": nrm(ks[24], (DEPTH, D_FF, D_MODEL), f32) * D_FF ** -0.5,
        "g_ffn_post": gain(ks[25], D_MODEL),
        "g_ple": gain(ks[26], D_MODEL),
        "w_ple": nrm(ks[27], (DEPTH, PLE_DIM, D_MODEL), f32) * PLE_DIM ** -0.5,
        "w_ple_gate": nrm(ks[28], (DEPTH, D_MODEL, D_MODEL), f32) * D_MODEL ** -0.5,
    }


def reference(x_prompt, x_sample, p_prompt, p_sample, state_conv_mix, state_mlstm_C,
              state_mlstm_n, state_mlstm_m, state_conv_ffn,
              g_mix_pre, w_in, b_igate, b_fgate, w_conv_mix, b_conv_mix, g_conv_norm,
              b_conv_norm, g_mlstm_norm, w_out, g_mix_post, g_ffn_pre, w_up, w_conv_ffn,
              b_conv_ffn, w_down, g_ffn_post, g_ple, w_ple, w_ple_gate):
    bp = x_prompt.shape[0]
    st_dtype = state_mlstm_C.dtype
    zc = jnp.zeros((bp, CONV_WIDTH - 1, CONV_CH), x_prompt.dtype)
    zC = jnp.zeros((bp, MLSTM_HEADS, MLSTM_HEAD_DIM, MLSTM_HEAD_DIM), st_dtype)
    zn = jnp.zeros((bp, MLSTM_HEADS, MLSTM_HEAD_DIM), st_dtype)
    zm = jnp.zeros((bp, MLSTM_HEADS), st_dtype)
    zf = jnp.zeros((bp, FFN_CONV_WIDTH - 1, 2 * D_FF), x_prompt.dtype)

    xp, xs = x_prompt, x_sample
    pc, pC, pn, pm, pf = [], [], [], [], []
    sc, sC, sn, sm, sf = [], [], [], [], []
    for l in range(DEPTH):
        params = (g_mix_pre[l], w_in[l], b_igate[l], b_fgate[l], w_conv_mix[l], b_conv_mix[l],
                  g_conv_norm[l], b_conv_norm[l], g_mlstm_norm[l], w_out[l], g_mix_post[l],
                  g_ffn_pre[l], w_up[l], w_conv_ffn[l], b_conv_ffn[l], w_down[l], g_ffn_post[l],
                  g_ple[l], w_ple[l], w_ple_gate[l])
        xp, c1, C1, n1, m1, f1 = trunk_layer(xp, p_prompt[l], zc, zC, zn, zm, zf, *params)
        pc.append(c1); pC.append(C1); pn.append(n1); pm.append(m1); pf.append(f1)
        xs, c2, C2, n2, m2, f2 = trunk_layer(
            xs, p_sample[l], state_conv_mix[l], state_mlstm_C[l], state_mlstm_n[l],
            state_mlstm_m[l], state_conv_ffn[l], *params)
        sc.append(c2); sC.append(C2); sn.append(n2); sm.append(m2); sf.append(f2)

    conv_mix_prompt = jnp.stack(pc)
    mlstm_C_prompt = jnp.stack(pC)
    mlstm_n_prompt = jnp.stack(pn)
    mlstm_m_prompt = jnp.stack(pm)
    conv_ffn_prompt = jnp.stack(pf)
    conv_mix_sample = jnp.stack(sc)
    mlstm_C_sample = jnp.stack(sC)
    mlstm_n_sample = jnp.stack(sn)
    mlstm_m_sample = jnp.stack(sm)
    conv_ffn_sample = jnp.stack(sf)
    return (xp, xs, conv_mix_prompt, mlstm_C_prompt, mlstm_n_prompt, mlstm_m_prompt,
            conv_ffn_prompt, conv_mix_sample, mlstm_C_sample, mlstm_n_sample, mlstm_m_sample,
            conv_ffn_sample)
```

```python
import functools

import jax
import jax.numpy as jnp
from jax import lax
from jax.experimental import pallas as pl
from jax.experimental.pallas import tpu as pltpu

F32 = jnp.float32
BF16 = jnp.bfloat16

D_MODEL = 1024
CONV_CH = 512
CONV_WIDTH = 31
CONV_TAIL = CONV_WIDTH - 1
CONV_GROUPS = 4
HEADS = 4
HEAD_DIM = 128
MLSTM_WIDTH = HEADS * HEAD_DIM
D_FF = 2816
FFN_TAIL = 2
PLE_DIM = 256
EPS = 1e-6
MAIN_COLS = 2 * CONV_CH + 4 * MLSTM_WIDTH
LANES = 128
SUBLANES = 8
NEG = -1e30

PROMPT_TILE = 256
SAMPLE_SEQS = 16
SEQ_PAD = SUBLANES
HIST = 32
FFN_CHUNK = 256
SAMPLE_EXT_ROWS = -(-(CONV_TAIL + SEQ_PAD) // SUBLANES) * SUBLANES
VMEM_LIMIT = 56 * 1024 * 1024


def _dot(a, b):
    return jnp.dot(a, b, preferred_element_type=F32)


def _dot_exact(a, b):
    return jnp.dot(a, b, preferred_element_type=F32, precision=lax.Precision.HIGHEST)


def _rms(x, g):
    ms = jnp.mean(x * x, axis=-1, keepdims=True)
    return x * lax.rsqrt(ms + EPS) * g


def _layernorm(x):
    mu = jnp.mean(x, axis=-1, keepdims=True)
    xc = x - mu
    var = jnp.mean(xc * xc, axis=-1, keepdims=True)
    return xc * lax.rsqrt(var + EPS)


def _sigmoid(x):
    return 1.0 / (1.0 + jnp.exp(-x))


def _log_sigmoid(x):
    return jnp.minimum(x, 0.0) - jnp.log(1.0 + jnp.exp(-jnp.abs(x)))


def _gelu_tanh(x):
    return 0.5 * x * (1.0 + jnp.tanh(0.7978845608028654 * (x + 0.044715 * (x * x * x))))


def _conv_branch_post(acc, g_ref, b_ref):
    parts = []
    for g in range(CONV_GROUPS):
        sl = slice(g * LANES, (g + 1) * LANES)
        y = _layernorm(acc[:, sl]) * g_ref[:, sl] + b_ref[:, sl]
        parts.append(y * _sigmoid(y))
    return parts


def _mlstm_intra(q_bf, k_bf, v_bf, d, inter):
    m_t = jnp.maximum(inter, jnp.max(d, axis=1, keepdims=True))
    w_intra = jnp.exp(d - m_t)
    w_inter = jnp.exp(inter - m_t)
    s = lax.dot_general(q_bf, k_bf, (((1,), (1,)), ((), ())), preferred_element_type=F32) * w_intra
    num = _dot(s.astype(BF16), v_bf)
    den = jnp.sum(s, axis=1, keepdims=True)
    return m_t, w_inter, num, den


def _head_out(num, den, m_t, g_mn, zo):
    hh = num / jnp.maximum(jnp.abs(den), jnp.exp(-m_t))
    return _layernorm(hh) * g_mn * _sigmoid(zo)


def _mixer_prompt_kernel(x_ref, g_pre_ref, w_in_ref, w_gate_ref, b_gate_ref, w_conv_ref, b_conv_ref,
                         g_cn_ref, b_cn_ref, g_mn_ref, w_out_ref, g_post_ref,
                         y_ref, conv_out_ref, c_out_ref, n_out_ref, m_out_ref,
                         ext_ref, cn_ref, m_ref):
    T = PROMPT_TILE
    s_idx = pl.program_id(1)
    last = pl.num_programs(1) - 1

    @pl.when(s_idx == 0)
    def _():
        ext_ref[0:HIST, :] = jnp.zeros((HIST, CONV_CH), F32)
        cn_ref[...] = jnp.zeros(cn_ref.shape, F32)
        m_ref[...] = jnp.zeros(m_ref.shape, F32)

    x = x_ref[...]
    h = _rms(x, g_pre_ref[...]).astype(BF16)

    zv = _dot(h, w_in_ref[:, 0:CONV_CH])
    zg = _dot(h, w_in_ref[:, CONV_CH:2 * CONV_CH])
    ext_ref[HIST:HIST + T, :] = zv * _sigmoid(zg)
    acc = jnp.broadcast_to(b_conv_ref[...], (T, CONV_CH))
    for j in range(CONV_WIDTH):
        acc = acc + w_conv_ref[j:j + 1, :] * ext_ref[pl.ds(HIST - CONV_TAIL + j, T), :]

    @pl.when(s_idx == last)
    def _():
        conv_out_ref[...] = ext_ref[pl.ds(HIST + T - CONV_TAIL, CONV_TAIL), :]

    ext_ref[0:HIST, :] = ext_ref[T:T + HIST, :]
    mix_parts = _conv_branch_post(acc, g_cn_ref, b_cn_ref)

    gates = _dot(h, w_gate_ref[...]) + b_gate_ref[...]
    row = lax.broadcasted_iota(jnp.int32, (T, T), 0)
    col = lax.broadcasted_iota(jnp.int32, (T, T), 1)
    causal = col <= row
    bcum = _dot_exact(causal.astype(F32), _log_sigmoid(gates))
    gates_t = gates.T
    bcum_t = bcum.T
    q_off = 2 * CONV_CH
    for hd in range(HEADS):
        c0 = hd * HEAD_DIM
        zq = _dot(h, w_in_ref[:, q_off + c0:q_off + c0 + HEAD_DIM])
        zk = _dot(h, w_in_ref[:, q_off + MLSTM_WIDTH + c0:q_off + MLSTM_WIDTH + c0 + HEAD_DIM])
        zvv = _dot(h, w_in_ref[:, q_off + 2 * MLSTM_WIDTH + c0:q_off + 2 * MLSTM_WIDTH + c0 + HEAD_DIM])
        zo = _dot(h, w_in_ref[:, q_off + 3 * MLSTM_WIDTH + c0:q_off + 3 * MLSTM_WIDTH + c0 + HEAD_DIM])
        q_bf = (zq * (HEAD_DIM ** -0.5)).astype(BF16)
        k_bf = zk.astype(BF16)
        v_bf = zvv.astype(BF16)

        i_row = gates_t[hd:hd + 1, :]
        i_col = gates[:, hd:hd + 1]
        b_row = bcum_t[HEADS + hd:HEADS + hd + 1, :]
        b_col = bcum[:, HEADS + hd:HEADS + hd + 1]
        m_prev = m_ref[hd:hd + 1, 0:1]
        d = jnp.where(causal, b_col - b_row + i_row, NEG)
        inter = b_col + m_prev
        m_t, w_inter, num, den = _mlstm_intra(q_bf, k_bf, v_bf, d, inter)

        cn = cn_ref[hd]
        carried = _dot(q_bf, cn.astype(BF16))
        num = num + carried[:, 0:HEAD_DIM] * w_inter
        den = den + carried[:, HEAD_DIM:HEAD_DIM + 1] * w_inter
        mix_parts.append(_head_out(num, den, m_t, g_mn_ref[:, c0:c0 + HEAD_DIM], zo))

        b_last = b_col[T - 1:T, :]
        m_new = m_t[T - 1:T, :]
        decay = w_inter[T - 1:T, :]
        ws = jnp.exp(b_last - b_col + i_col - m_new)
        vp = jnp.concatenate([ws * zvv, jnp.broadcast_to(ws, (T, HEAD_DIM))], axis=1).astype(BF16)
        cn_new = decay * cn + lax.dot_general(k_bf, vp, (((0,), (0,)), ((), ())),
                                              preferred_element_type=F32)
        cn_ref[hd] = cn_new
        m_ref[hd:hd + 1, :] = jnp.broadcast_to(m_new, (1, LANES))

        @pl.when(s_idx == last)
        def _():
            c_out_ref[hd] = cn_new[:, 0:HEAD_DIM]
            n_out_ref[hd:hd + 1, :] = cn_new[:, HEAD_DIM:].T[0:1, :]

    @pl.when(s_idx == last)
    def _():
        m_out_ref[...] = m_ref[...]

    mix = jnp.concatenate(mix_parts, axis=1).astype(BF16)
    y_ref[...] = x + _rms(_dot(mix, w_out_ref[...]), g_post_ref[...])


def _full(shape):
    n = len(shape)
    return pl.BlockSpec(shape, lambda *_: (0,) * n)


def _mixer_prompt(x, wts):
    B, S, D = x.shape
    T = PROMPT_TILE
    weights = (wts["g_mix_pre"], wts["w_in"], wts["w_gate"], wts["b_gate"], wts["w_conv_mix"],
               wts["b_conv_mix"], wts["g_conv_norm"], wts["b_conv_norm"], wts["g_mlstm_norm"],
               wts["w_out"], wts["g_mix_post"])
    out_shape = (
        jax.ShapeDtypeStruct((B, S, D), F32),
        jax.ShapeDtypeStruct((B, CONV_TAIL, CONV_CH), F32),
        jax.ShapeDtypeStruct((B, HEADS, HEAD_DIM, HEAD_DIM), F32),
        jax.ShapeDtypeStruct((B, HEADS, HEAD_DIM), F32),
        jax.ShapeDtypeStruct((B, SUBLANES, LANES), F32),
    )
    out_specs = (
        pl.BlockSpec((None, T, D), lambda b, s: (b, s, 0)),
        pl.BlockSpec((None, CONV_TAIL, CONV_CH), lambda b, s: (b, 0, 0)),
        pl.BlockSpec((None, HEADS, HEAD_DIM, HEAD_DIM), lambda b, s: (b, 0, 0, 0)),
        pl.BlockSpec((None, HEADS, HEAD_DIM), lambda b, s: (b, 0, 0)),
        pl.BlockSpec((None, SUBLANES, LANES), lambda b, s: (b, 0, 0)),
    )
    return pl.pallas_call(
        _mixer_prompt_kernel,
        grid=(B, S // T),
        in_specs=[pl.BlockSpec((None, T, D), lambda b, s: (b, s, 0))] + [_full(w.shape) for w in weights],
        out_specs=out_specs,
        out_shape=out_shape,
        scratch_shapes=[
            pltpu.VMEM((HIST + T, CONV_CH), F32),
            pltpu.VMEM((HEADS, HEAD_DIM, 2 * HEAD_DIM), F32),
            pltpu.VMEM((SUBLANES, LANES), F32),
        ],
        compiler_params=pltpu.CompilerParams(
            dimension_semantics=("arbitrary", "arbitrary"), vmem_limit_bytes=VMEM_LIMIT),
        name="mixer_prompt",
    )(x, *weights)


def _ffn_tail(x, f_ref, p_ref, w_down_ref, g_post_ref, g_ple_ref, w_ple_ref, w_pg_ref):
    x2 = x + _rms(_dot(f_ref[...], w_down_ref[...]), g_post_ref[...])
    emb = _dot(p_ref[...].astype(BF16), w_ple_ref[...])
    gate = _sigmoid(_dot(_rms(x2, g_ple_ref[...]).astype(BF16), w_pg_ref[...]))
    return x2 + emb * gate


def _ffn_prompt_kernel(x_ref, p_ref, g_pre_ref, w_up_ref, w_conv_ref, b_conv_ref, w_down_ref,
                       g_post_ref, g_ple_ref, w_ple_ref, w_pg_ref,
                       y_ref, tail_out_ref,
                       hist_ref, ubuf_ref, f_ref):
    T = PROMPT_TILE
    s_idx = pl.program_id(1)
    last = pl.num_programs(1) - 1

    @pl.when(s_idx == 0)
    def _():
        hist_ref[...] = jnp.zeros(hist_ref.shape, F32)

    x = x_ref[...]
    h = _rms(x, g_pre_ref[...]).astype(BF16)
    for c in range(D_FF // FFN_CHUNK):
        halves = []
        for half in range(2):
            c0 = half * D_FF + c * FFN_CHUNK
            cs = slice(c0, c0 + FFN_CHUNK)
            u = _dot(h, w_up_ref[:, cs])
            ubuf_ref[half, 0:SUBLANES, :] = hist_ref[:, cs]
            ubuf_ref[half, SUBLANES:SUBLANES + T, :] = u
            hist_ref[:, cs] = u[T - SUBLANES:T, :]
            y = (w_conv_ref[0:1, cs] * ubuf_ref[half, pl.ds(SUBLANES - 2, T), :]
                 + w_conv_ref[1:2, cs] * ubuf_ref[half, pl.ds(SUBLANES - 1, T), :]
                 + w_conv_ref[2:3, cs] * u + b_conv_ref[:, cs])
            halves.append(y)
        f_ref[:, c * FFN_CHUNK:(c + 1) * FFN_CHUNK] = (_gelu_tanh(halves[0]) * halves[1]).astype(BF16)

    @pl.when(s_idx == last)
    def _():
        tail_out_ref[...] = hist_ref[SUBLANES - FFN_TAIL:SUBLANES, :]

    y_ref[...] = _ffn_tail(x, f_ref, p_ref, w_down_ref, g_post_ref, g_ple_ref, w_ple_ref, w_pg_ref)


def _ffn_prompt(x, p, wts):
    B, S, D = x.shape
    T = PROMPT_TILE
    weights = (wts["g_ffn_pre"], wts["w_up"], wts["w_conv_ffn"], wts["b_conv_ffn"], wts["w_down"],
               wts["g_ffn_post"], wts["g_ple"], wts["w_ple"], wts["w_ple_gate"])
    return pl.pallas_call(
        _ffn_prompt_kernel,
        grid=(B, S // T),
        in_specs=[pl.BlockSpec((None, T, D), lambda b, s: (b, s, 0)),
                  pl.BlockSpec((None, T, PLE_DIM), lambda b, s: (b, s, 0))]
                 + [_full(w.shape) for w in weights],
        out_specs=(pl.BlockSpec((None, T, D), lambda b, s: (b, s, 0)),
                   pl.BlockSpec((None, FFN_TAIL, 2 * D_FF), lambda b, s: (b, 0, 0))),
        out_shape=(jax.ShapeDtypeStruct((B, S, D), F32),
                   jax.ShapeDtypeStruct((B, FFN_TAIL, 2 * D_FF), F32)),
        scratch_shapes=[
            pltpu.VMEM((SUBLANES, 2 * D_FF), F32),
            pltpu.VMEM((2, SUBLANES + T, FFN_CHUNK), F32),
            pltpu.VMEM((T, D_FF), BF16),
        ],
        compiler_params=pltpu.CompilerParams(
            dimension_semantics=("arbitrary", "arbitrary"), vmem_limit_bytes=VMEM_LIMIT),
        name="ffn_prompt",
    )(x, p, *weights)


def _mixer_sample_kernel(x_ref, st_ref, c_ref, n_ref, mrow_ref,
                         g_pre_ref, w_in_ref, w_gate_ref, b_gate_ref, w_conv_ref, b_conv_ref,
                         g_cn_ref, b_cn_ref, g_mn_ref, w_out_ref, g_post_ref,
                         y_ref, conv_out_ref, c_out_ref, n_out_ref, m_out_ref,
                         ext_ref, q_ref, kt_ref, wv_ref, wk_ref, carried_ref, qn_ref, dec_ref, *, seq_len):
    NB = SAMPLE_SEQS
    R = NB * SEQ_PAD
    x = x_ref[...]
    h = _rms(x, g_pre_ref[...]).astype(BF16)

    zv = _dot(h, w_in_ref[:, 0:CONV_CH])
    zg = _dot(h, w_in_ref[:, CONV_CH:2 * CONV_CH])
    a = zv * _sigmoid(zg)
    ext_ref[:, 0:CONV_TAIL, :] = st_ref[...]
    ext_ref[:, CONV_TAIL:CONV_TAIL + SEQ_PAD, :] = a.reshape(NB, SEQ_PAD, CONV_CH)
    ext_ref[:, CONV_TAIL + SEQ_PAD:, :] = jnp.zeros((NB, SAMPLE_EXT_ROWS - CONV_TAIL - SEQ_PAD, CONV_CH), F32)
    acc = jnp.broadcast_to(b_conv_ref[...][None], (NB, SEQ_PAD, CONV_CH))
    for j in range(CONV_WIDTH):
        acc = acc + w_conv_ref[j:j + 1, :][None] * ext_ref[:, pl.ds(j, SEQ_PAD), :]
    conv_out_ref[...] = ext_ref[:, seq_len:seq_len + CONV_TAIL, :]
    mix_parts = _conv_branch_post(acc.reshape(R, CONV_CH), g_cn_ref, b_cn_ref)

    gates = _dot(h, w_gate_ref[...]) + b_gate_ref[...]
    row = lax.broadcasted_iota(jnp.int32, (R, R), 0)
    col = lax.broadcasted_iota(jnp.int32, (R, R), 1)
    same_seq = (row // SEQ_PAD) == (col // SEQ_PAD)
    causal = same_seq & (col <= row)
    bcum = _dot_exact(causal.astype(F32), _log_sigmoid(gates))
    mask = causal & ((col % SEQ_PAD) < seq_len)
    pick_last = (same_seq & ((col % SEQ_PAD) == seq_len - 1)).astype(F32)
    inter_all = bcum + mrow_ref[...]
    gates_t = gates.T
    bcum_t = bcum.T
    lane = lax.broadcasted_iota(jnp.int32, (R, LANES), 1)
    row_valid = (lax.broadcasted_iota(jnp.int32, (R, 1), 0) % SEQ_PAD) < seq_len
    stats = jnp.where((lane >= HEADS) & (lane < 2 * HEADS), bcum, 0.0)
    q_off = 2 * CONV_CH
    saved = []
    for hd in range(HEADS):
        c0 = hd * HEAD_DIM
        zq = _dot(h, w_in_ref[:, q_off + c0:q_off + c0 + HEAD_DIM]) * (HEAD_DIM ** -0.5)
        zk = _dot(h, w_in_ref[:, q_off + MLSTM_WIDTH + c0:q_off + MLSTM_WIDTH + c0 + HEAD_DIM])
        zvv = _dot(h, w_in_ref[:, q_off + 2 * MLSTM_WIDTH + c0:q_off + 2 * MLSTM_WIDTH + c0 + HEAD_DIM])
        zo = _dot(h, w_in_ref[:, q_off + 3 * MLSTM_WIDTH + c0:q_off + 3 * MLSTM_WIDTH + c0 + HEAD_DIM])
        i_row = gates_t[hd:hd + 1, :]
        b_row = bcum_t[HEADS + hd:HEADS + hd + 1, :]
        b_col = bcum[:, HEADS + hd:HEADS + hd + 1]
        d = jnp.where(mask, b_col - b_row + i_row, NEG)
        inter = inter_all[:, HEADS + hd:HEADS + hd + 1]
        m_t, w_inter, num, den = _mlstm_intra(zq.astype(BF16), zk.astype(BF16), zvv.astype(BF16), d, inter)
        stats = jnp.where(lane == hd, m_t, stats)
        stats = jnp.where(lane == 2 * HEADS + hd, w_inter, stats)
        q_ref[hd, 0:R, :] = zq
        q_ref[hd, R:R + SEQ_PAD, :] = jnp.zeros((SEQ_PAD, HEAD_DIM), F32)
        kt_ref[hd] = zk.T
        saved.append((m_t, w_inter, num, den, zo, zk, zvv))

    per_seq = _dot_exact(pick_last, stats)
    m_out_ref[...] = per_seq
    for hd in range(HEADS):
        zk, zvv = saved[hd][5], saved[hd][6]
        m_new = per_seq[:, hd:hd + 1]
        b_last = per_seq[:, HEADS + hd:HEADS + hd + 1]
        decay = per_seq[:, 2 * HEADS + hd:2 * HEADS + hd + 1]
        b_col = bcum[:, HEADS + hd:HEADS + hd + 1]
        i_col = gates[:, hd:hd + 1]
        ws = jnp.where(row_valid, jnp.exp(b_last - b_col + i_col - m_new), 0.0)
        wv_ref[hd] = (ws * zvv).astype(BF16)
        wk_ref[hd] = ws * zk
        dec_ref[hd] = jnp.broadcast_to(decay, (R, LANES))

    col_seq = lax.broadcasted_iota(jnp.int32, (HEAD_DIM, R), 1) // SEQ_PAD

    def per_sequence(b, carry):
        r0 = pl.multiple_of(b * SEQ_PAD, SEQ_PAD)
        for hd in range(HEADS):
            c_old = c_ref[b, hd]
            n_old = n_ref[b, hd:hd + 1, :]
            q2 = q_ref[hd, pl.ds(r0, 2 * SEQ_PAD), :]
            carried_ref[hd, pl.ds(r0, SEQ_PAD), :] = _dot(q2.astype(BF16), c_old.astype(BF16))[0:SEQ_PAD, :]
            qn = jnp.sum(q2[0:SEQ_PAD, :] * n_old, axis=1, keepdims=True)
            qn_ref[hd, pl.ds(r0, SEQ_PAD), :] = jnp.broadcast_to(qn, (SEQ_PAD, LANES))
            dec = dec_ref[hd, pl.ds(r0, 1), :]
            kt_b = jnp.where(col_seq == b, kt_ref[hd], 0.0).astype(BF16)
            c_out_ref[b, hd] = dec * c_old + _dot(kt_b, wv_ref[hd])
            n_out_ref[b, hd:hd + 1, :] = dec * n_old + jnp.sum(wk_ref[hd, pl.ds(r0, SEQ_PAD), :], axis=0,
                                                                keepdims=True)
        return carry

    lax.fori_loop(0, NB, per_sequence, 0)

    for hd in range(HEADS):
        c0 = hd * HEAD_DIM
        m_t, w_inter, num, den, zo = saved[hd][:5]
        num = num + carried_ref[hd] * w_inter
        den = den + qn_ref[hd][:, 0:1] * w_inter
        mix_parts.append(_head_out(num, den, m_t, g_mn_ref[:, c0:c0 + HEAD_DIM], zo))

    mix = jnp.concatenate(mix_parts, axis=1).astype(BF16)
    y_ref[...] = x + _rms(_dot(mix, w_out_ref[...]), g_post_ref[...])


def _mixer_sample(x, st, c, n, mrow, wts, seq_len):
    NB = SAMPLE_SEQS
    R = NB * SEQ_PAD
    nseq = c.shape[0]
    weights = (wts["g_mix_pre"], wts["w_in"], wts["w_gate"], wts["b_gate"], wts["w_conv_mix"],
               wts["b_conv_mix"], wts["g_conv_norm"], wts["b_conv_norm"], wts["g_mlstm_norm"],
               wts["w_out"], wts["g_mix_post"])
    rows = lambda width: pl.BlockSpec((R, width), lambda i: (i, 0))
    st_spec = pl.BlockSpec((NB, CONV_TAIL, CONV_CH), lambda i: (i, 0, 0))
    c_spec = pl.BlockSpec((NB, HEADS, HEAD_DIM, HEAD_DIM), lambda i: (i, 0, 0, 0))
    n_spec = pl.BlockSpec((NB, HEADS, HEAD_DIM), lambda i: (i, 0, 0))
    return pl.pallas_call(
        functools.partial(_mixer_sample_kernel, seq_len=seq_len),
        grid=(nseq // NB,),
        in_specs=[rows(D_MODEL), st_spec, c_spec, n_spec, rows(LANES)] + [_full(w.shape) for w in weights],
        out_specs=(rows(D_MODEL), st_spec, c_spec, n_spec, rows(LANES)),
        out_shape=(jax.ShapeDtypeStruct(x.shape, F32), jax.ShapeDtypeStruct(st.shape, F32),
                   jax.ShapeDtypeStruct(c.shape, F32), jax.ShapeDtypeStruct(n.shape, F32),
                   jax.ShapeDtypeStruct(mrow.shape, F32)),
        scratch_shapes=[
            pltpu.VMEM((NB, SAMPLE_EXT_ROWS, CONV_CH), F32),
            pltpu.VMEM((HEADS, R + SEQ_PAD, HEAD_DIM), F32),
            pltpu.VMEM((HEADS, HEAD_DIM, R), F32),
            pltpu.VMEM((HEADS, R, HEAD_DIM), BF16),
            pltpu.VMEM((HEADS, R, HEAD_DIM), F32),
            pltpu.VMEM((HEADS, R, HEAD_DIM), F32),
            pltpu.VMEM((HEADS, R, LANES), F32),
            pltpu.VMEM((HEADS, R, LANES), F32),
        ],
        compiler_params=pltpu.CompilerParams(
            dimension_semantics=("arbitrary",), vmem_limit_bytes=VMEM_LIMIT),
        name="mixer_sample",
    )(x, st, c, n, mrow, *weights)


def _ffn_sample_kernel(x_ref, p_ref, st_ref, g_pre_ref, w_up_ref, w_conv_ref, b_conv_ref, w_down_ref,
                       g_post_ref, g_ple_ref, w_ple_ref, w_pg_ref,
                       y_ref, tail_out_ref,
                       ubuf_ref, f_ref, *, seq_len):
    NB = SAMPLE_SEQS
    R = NB * SEQ_PAD
    x = x_ref[...]
    h = _rms(x, g_pre_ref[...]).astype(BF16)
    lo = SEQ_PAD - FFN_TAIL
    for c in range(D_FF // FFN_CHUNK):
        halves = []
        for half in range(2):
            c0 = half * D_FF + c * FFN_CHUNK
            cs = slice(c0, c0 + FFN_CHUNK)
            u = _dot(h, w_up_ref[:, cs])
            ubuf_ref[half, :, lo:SEQ_PAD, :] = st_ref[:, :, cs]
            ubuf_ref[half, :, SEQ_PAD:2 * SEQ_PAD, :] = u.reshape(NB, SEQ_PAD, FFN_CHUNK)
            tail_out_ref[:, :, cs] = ubuf_ref[half, :, lo + seq_len:SEQ_PAD + seq_len, :]
            y = (w_conv_ref[0:1, cs][None] * ubuf_ref[half, :, pl.ds(lo, SEQ_PAD), :]
                 + w_conv_ref[1:2, cs][None] * ubuf_ref[half, :, pl.ds(lo + 1, SEQ_PAD), :]
                 + w_conv_ref[2:3, cs][None] * ubuf_ref[half, :, pl.ds(lo + 2, SEQ_PAD), :]
                 + b_conv_ref[:, cs][None])
            halves.append(y.reshape(R, FFN_CHUNK))
        f_ref[:, c * FFN_CHUNK:(c + 1) * FFN_CHUNK] = (_gelu_tanh(halves[0]) * halves[1]).astype(BF16)
    y_ref[...] = _ffn_tail(x, f_ref, p_ref, w_down_ref, g_post_ref, g_ple_ref, w_ple_ref, w_pg_ref)


def _ffn_sample(x, p, st, wts, seq_len):
    NB = SAMPLE_SEQS
    R = NB * SEQ_PAD
    nseq = st.shape[0]
    weights = (wts["g_ffn_pre"], wts["w_up"], wts["w_conv_ffn"], wts["b_conv_ffn"], wts["w_down"],
               wts["g_ffn_post"], wts["g_ple"], wts["w_ple"], wts["w_ple_gate"])
    st_spec = pl.BlockSpec((NB, FFN_TAIL, 2 * D_FF), lambda i: (i, 0, 0))
    return pl.pallas_call(
        functools.partial(_ffn_sample_kernel, seq_len=seq_len),
        grid=(nseq // NB,),
        in_specs=[pl.BlockSpec((R, D_MODEL), lambda i: (i, 0)), pl.BlockSpec((R, PLE_DIM), lambda i: (i, 0)),
                  st_spec] + [_full(w.shape) for w in weights],
        out_specs=(pl.BlockSpec((R, D_MODEL), lambda i: (i, 0)), st_spec),
        out_shape=(jax.ShapeDtypeStruct(x.shape, F32), jax.ShapeDtypeStruct(st.shape, F32)),
        scratch_shapes=[
            pltpu.VMEM((2, NB, 2 * SEQ_PAD, FFN_CHUNK), F32),
            pltpu.VMEM((R, D_FF), BF16),
        ],
        compiler_params=pltpu.CompilerParams(
            dimension_semantics=("arbitrary",), vmem_limit_bytes=VMEM_LIMIT),
        name="ffn_sample",
    )(x, p, st, *weights)


def _pad_seq(a):
    nseq, seq_len, width = a.shape
    return jnp.pad(a, ((0, 0), (0, SEQ_PAD - seq_len), (0, 0))).reshape(nseq * SEQ_PAD, width)


def _layer_weights(l, g_mix_pre, w_in, b_igate, b_fgate, w_conv_mix, b_conv_mix, g_conv_norm,
                   b_conv_norm, g_mlstm_norm, w_out, g_mix_post, g_ffn_pre, w_up, w_conv_ffn,
                   b_conv_ffn, w_down, g_ffn_post, g_ple, w_ple, w_ple_gate):
    row = lambda v: v[l][None, :].astype(F32)
    n_gate = 2 * HEADS
    w_gate = jnp.pad(w_in[l][:, MAIN_COLS:], ((0, 0), (0, LANES - n_gate))).astype(BF16)
    b_gate = jnp.pad(jnp.concatenate([b_igate[l], b_fgate[l]]), (0, LANES - n_gate))[None, :].astype(F32)
    return {
        "g_mix_pre": row(g_mix_pre), "w_in": w_in[l][:, :MAIN_COLS].astype(BF16),
        "w_gate": w_gate, "b_gate": b_gate,
        "w_conv_mix": w_conv_mix[l].astype(F32), "b_conv_mix": row(b_conv_mix),
        "g_conv_norm": row(g_conv_norm), "b_conv_norm": row(b_conv_norm),
        "g_mlstm_norm": row(g_mlstm_norm), "w_out": w_out[l].astype(BF16), "g_mix_post": row(g_mix_post),
        "g_ffn_pre": row(g_ffn_pre), "w_up": w_up[l].astype(BF16), "w_conv_ffn": w_conv_ffn[l].astype(F32),
        "b_conv_ffn": row(b_conv_ffn), "w_down": w_down[l].astype(BF16), "g_ffn_post": row(g_ffn_post),
        "g_ple": row(g_ple), "w_ple": w_ple[l].astype(BF16), "w_ple_gate": w_ple_gate[l].astype(BF16),
    }


def kernel(x_prompt, x_sample, p_prompt, p_sample, state_conv_mix, state_mlstm_C, state_mlstm_n, state_mlstm_m, state_conv_ffn, g_mix_pre, w_in, b_igate, b_fgate, w_conv_mix, b_conv_mix, g_conv_norm, b_conv_norm, g_mlstm_norm, w_out, g_mix_post, g_ffn_pre, w_up, w_conv_ffn, b_conv_ffn, w_down, g_ffn_post, g_ple, w_ple, w_ple_gate):
    depth = w_in.shape[0]
    nseq, seq_len, _ = x_sample.shape
    assert FFN_TAIL <= seq_len <= SEQ_PAD and nseq % SAMPLE_SEQS == 0
    assert x_prompt.shape[1] % PROMPT_TILE == 0
    xp = x_prompt
    xs = _pad_seq(x_sample)
    pc, pC, pn, pm, pf = [], [], [], [], []
    sc, sC, sn, sm, sf = [], [], [], [], []
    for l in range(depth):
        wts = _layer_weights(l, g_mix_pre, w_in, b_igate, b_fgate, w_conv_mix, b_conv_mix, g_conv_norm,
                             b_conv_norm, g_mlstm_norm, w_out, g_mix_post, g_ffn_pre, w_up, w_conv_ffn,
                             b_conv_ffn, w_down, g_ffn_post, g_ple, w_ple, w_ple_gate)
        xp, c1, C1, n1, m1 = _mixer_prompt(xp, wts)
        xp, f1 = _ffn_prompt(xp, p_prompt[l], wts)
        pc.append(c1); pC.append(C1); pn.append(n1); pm.append(m1[:, :HEADS, 0]); pf.append(f1)

        mrow = jnp.pad(jnp.repeat(state_mlstm_m[l].astype(F32), SEQ_PAD, axis=0),
                       ((0, 0), (HEADS, LANES - 2 * HEADS)))
        xs, c2, C2, n2, m2 = _mixer_sample(xs, state_conv_mix[l], state_mlstm_C[l], state_mlstm_n[l], mrow,
                                           wts, seq_len)
        xs, f2 = _ffn_sample(xs, _pad_seq(p_sample[l]), state_conv_ffn[l], wts, seq_len)
        sc.append(c2); sC.append(C2); sn.append(n2); sf.append(f2)
        sm.append(m2.reshape(nseq, SEQ_PAD, LANES)[:, 0, :HEADS])
    ys = xs.reshape(nseq, SEQ_PAD, D_MODEL)[:, :seq_len]
    return (xp, ys, jnp.stack(pc), jnp.stack(pC), jnp.stack(pn), jnp.stack(pm), jnp.stack(pf),
            jnp.stack(sc), jnp.stack(sC), jnp.stack(sn), jnp.stack(sm), jnp.stack(sf))
```

```python
import functools

import jax
import jax.numpy as jnp
from jax import lax
from jax.experimental import pallas as pl
from jax.experimental.pallas import tpu as pltpu

F32 = jnp.float32
BF16 = jnp.bfloat16

D_MODEL = 1024
CONV_CH = 512
CONV_WIDTH = 31
CONV_TAIL = CONV_WIDTH - 1
CONV_GROUPS = 4
HEADS = 4
HEAD_DIM = 128
MLSTM_WIDTH = HEADS * HEAD_DIM
D_FF = 2816
FFN_TAIL = 2
PLE_DIM = 256
EPS = 1e-6
MAIN_COLS = 2 * CONV_CH + 4 * MLSTM_WIDTH
LANES = 128
SUBLANES = 8
NEG = -1e30

PROMPT_TILE = 256
SAMPLE_SEQS = 16
SEQ_PAD = SUBLANES
HIST = 32
SHIFT_ROWS = PROMPT_TILE + HIST - SUBLANES
CONV_ROWS = 64
FFN_CHUNK = 256
SAMPLE_EXT_ROWS = -(-(CONV_TAIL + SEQ_PAD) // SUBLANES) * SUBLANES
VMEM_LIMIT = 56 * 1024 * 1024


def _dot(a, b):
    return jnp.dot(a, b, preferred_element_type=F32)


def _dot_exact(sel, x):
    hi = x.astype(BF16)
    r1 = x - hi.astype(F32)
    mid = r1.astype(BF16)
    lo = (r1 - mid.astype(F32)).astype(BF16)
    y = _dot(jnp.where(sel, 1.0, 0.0).astype(BF16), jnp.concatenate([hi, mid, lo], axis=1))
    return y[:, 0:LANES] + y[:, LANES:2 * LANES] + y[:, 2 * LANES:3 * LANES]


def _rms(x, g):
    ms = jnp.mean(x * x, axis=-1, keepdims=True)
    return x * lax.rsqrt(ms + EPS) * g


def _layernorm(x):
    mu = jnp.mean(x, axis=-1, keepdims=True)
    xc = x - mu
    var = jnp.mean(xc * xc, axis=-1, keepdims=True)
    return xc * lax.rsqrt(var + EPS)


def _sigmoid(x):
    return 1.0 / (1.0 + jnp.exp(-x))


def _log_sigmoid(x):
    return jnp.minimum(x, 0.0) - jnp.log(1.0 + jnp.exp(-jnp.abs(x)))


def _gelu_tanh(x):
    return 0.5 * x * (1.0 + jnp.tanh(0.7978845608028654 * (x + 0.044715 * (x * x * x))))


def _conv_branch_post(acc, g_ref, b_ref):
    parts = []
    for g in range(CONV_GROUPS):
        sl = slice(g * LANES, (g + 1) * LANES)
        y = _layernorm(acc[:, sl]) * g_ref[:, sl] + b_ref[:, sl]
        parts.append(y * _sigmoid(y))
    return parts


def _mlstm_intra(q_bf, k_bf, v_bf, d, inter):
    m_t = jnp.maximum(inter, jnp.max(d, axis=1, keepdims=True))
    w_intra = jnp.exp(d - m_t)
    w_inter = jnp.exp(inter - m_t)
    s = lax.dot_general(q_bf, k_bf, (((1,), (1,)), ((), ())), preferred_element_type=F32) * w_intra
    num = _dot(s.astype(BF16), v_bf)
    den = jnp.sum(s, axis=1, keepdims=True)
    return m_t, w_inter, num, den


def _head_out(num, den, m_t, g_mn, zo):
    hh = num / jnp.maximum(jnp.abs(den), jnp.exp(-m_t))
    return _layernorm(hh) * g_mn * _sigmoid(zo)


def _mixer_prompt_kernel(x_ref, g_pre_ref, w_in_ref, w_gate_ref, b_gate_ref, w_conv_ref, b_conv_ref,
                         g_cn_ref, b_cn_ref, g_mn_ref, w_out_ref, g_post_ref,
                         y_ref, conv_out_ref, c_out_ref, n_out_ref, m_out_ref,
                         ext_ref, sh_ref, mix_ref, cn_ref, m_ref, h_ref):
    T = PROMPT_TILE
    s_idx = pl.program_id(1)
    last = pl.num_programs(1) - 1

    @pl.when(s_idx == 0)
    def _():
        ext_ref[0:HIST, :] = jnp.zeros((HIST, CONV_CH), F32)
        cn_ref[...] = jnp.zeros(cn_ref.shape, F32)
        m_ref[...] = jnp.zeros(m_ref.shape, F32)

    x = x_ref[...]
    h_ref[...] = _rms(x, g_pre_ref[...]).astype(BF16)

    zv = _dot(h_ref[...], w_in_ref[:, 0:CONV_CH])
    zg = _dot(h_ref[...], w_in_ref[:, CONV_CH:2 * CONV_CH])
    ext_ref[HIST:HIST + T, :] = zv * _sigmoid(zg)
    for r in range(1, SUBLANES):
        sh_ref[r - 1] = ext_ref[pl.ds(r, SHIFT_ROWS), :]
    for g in range(CONV_GROUPS):
        cs = slice(g * LANES, (g + 1) * LANES)
        for rb in range(T // CONV_ROWS):
            acc = jnp.broadcast_to(b_conv_ref[:, cs], (CONV_ROWS, LANES))
            for j in range(CONV_WIDTH):
                off = HIST - CONV_TAIL + j
                r, base = off % SUBLANES, rb * CONV_ROWS + off - off % SUBLANES
                src = ext_ref if r == 0 else sh_ref.at[r - 1]
                acc = acc + w_conv_ref[j:j + 1, cs] * src[base:base + CONV_ROWS, cs]
            y = _layernorm(acc) * g_cn_ref[:, cs] + b_cn_ref[:, cs]
            mix_ref[rb * CONV_ROWS:(rb + 1) * CONV_ROWS, cs] = (y * _sigmoid(y)).astype(BF16)

    @pl.when(s_idx == last)
    def _():
        conv_out_ref[...] = ext_ref[pl.ds(HIST + T - CONV_TAIL, CONV_TAIL), :]

    ext_ref[0:HIST, :] = ext_ref[T:T + HIST, :]

    gates = _dot(h_ref[...], w_gate_ref[...]) + b_gate_ref[...]
    row = lax.broadcasted_iota(jnp.int32, (T, T), 0)
    col = lax.broadcasted_iota(jnp.int32, (T, T), 1)
    causal = col <= row
    bcum = _dot_exact(causal, _log_sigmoid(gates))
    gates_t = gates.T
    bcum_t = bcum.T
    q_off = 2 * CONV_CH
    zq_all = _dot(h_ref[...], w_in_ref[:, q_off:q_off + MLSTM_WIDTH])
    zk_all = _dot(h_ref[...], w_in_ref[:, q_off + MLSTM_WIDTH:q_off + 2 * MLSTM_WIDTH])
    zv_all = _dot(h_ref[...], w_in_ref[:, q_off + 2 * MLSTM_WIDTH:q_off + 3 * MLSTM_WIDTH])
    zo_all = _dot(h_ref[...], w_in_ref[:, q_off + 3 * MLSTM_WIDTH:q_off + 4 * MLSTM_WIDTH])
    for hd in range(HEADS):
        c0 = hd * HEAD_DIM
        zvv = zv_all[:, c0:c0 + HEAD_DIM]
        zo = zo_all[:, c0:c0 + HEAD_DIM]
        q_bf = (zq_all[:, c0:c0 + HEAD_DIM] * (HEAD_DIM ** -0.5)).astype(BF16)
        k_bf = zk_all[:, c0:c0 + HEAD_DIM].astype(BF16)
        v_bf = zvv.astype(BF16)

        i_row = gates_t[hd:hd + 1, :]
        i_col = gates[:, hd:hd + 1]
        b_row = bcum_t[HEADS + hd:HEADS + hd + 1, :]
        b_col = bcum[:, HEADS + hd:HEADS + hd + 1]
        m_prev = m_ref[hd:hd + 1, 0:1]
        d = jnp.where(causal, b_col - b_row + i_row, NEG)
        inter = b_col + m_prev
        m_t, w_inter, num, den = _mlstm_intra(q_bf, k_bf, v_bf, d, inter)

        cn = cn_ref[hd]
        carried = _dot(q_bf, cn.astype(BF16))
        num = num + carried[:, 0:HEAD_DIM] * w_inter
        den = den + carried[:, HEAD_DIM:HEAD_DIM + 1] * w_inter
        mix_ref[:, CONV_CH + c0:CONV_CH + c0 + HEAD_DIM] = _head_out(
            num, den, m_t, g_mn_ref[:, c0:c0 + HEAD_DIM], zo).astype(BF16)

        b_last = b_col[T - 1:T, :]
        m_new = m_t[T - 1:T, :]
        decay = w_inter[T - 1:T, :]
        ws = jnp.exp(b_last - b_col + i_col - m_new)
        vp = jnp.concatenate([ws * zvv, jnp.broadcast_to(ws, (T, HEAD_DIM))], axis=1).astype(BF16)
        cn_new = decay * cn + lax.dot_general(k_bf, vp, (((0,), (0,)), ((), ())),
                                              preferred_element_type=F32)
        cn_ref[hd] = cn_new
        m_ref[hd:hd + 1, :] = jnp.broadcast_to(m_new, (1, LANES))

        @pl.when(s_idx == last)
        def _():
            c_out_ref[hd] = cn_new[:, 0:HEAD_DIM]
            n_out_ref[hd:hd + 1, :] = cn_new[:, HEAD_DIM:].T[0:1, :]

    @pl.when(s_idx == last)
    def _():
        m_out_ref[...] = m_ref[...]

    y_ref[...] = x + _rms(_dot(mix_ref[...], w_out_ref[...]), g_post_ref[...])


def _full(shape):
    n = len(shape)
    return pl.BlockSpec(shape, lambda *_: (0,) * n)


def _mixer_prompt(x, wts, time_major):
    D = D_MODEL
    if time_major:
        S, B = x.shape[0], x.shape[1] // D
    else:
        B, S, _ = x.shape
    T = PROMPT_TILE
    weights = (wts["g_mix_pre"], wts["w_in"], wts["w_gate"], wts["b_gate"], wts["w_conv_mix"],
               wts["b_conv_mix"], wts["g_conv_norm"], wts["b_conv_norm"], wts["g_mlstm_norm"],
               wts["w_out"], wts["g_mix_post"])
    tm_spec = pl.BlockSpec((T, D), lambda b, s: (s, b))
    out_shape = (
        jax.ShapeDtypeStruct((S, B * D), F32),
        jax.ShapeDtypeStruct((B, CONV_TAIL, CONV_CH), F32),
        jax.ShapeDtypeStruct((B, HEADS, HEAD_DIM, HEAD_DIM), F32),
        jax.ShapeDtypeStruct((B, HEADS, HEAD_DIM), F32),
        jax.ShapeDtypeStruct((B, SUBLANES, LANES), F32),
    )
    out_specs = (
        tm_spec,
        pl.BlockSpec((None, CONV_TAIL, CONV_CH), lambda b, s: (b, 0, 0)),
        pl.BlockSpec((None, HEADS, HEAD_DIM, HEAD_DIM), lambda b, s: (b, 0, 0, 0)),
        pl.BlockSpec((None, HEADS, HEAD_DIM), lambda b, s: (b, 0, 0)),
        pl.BlockSpec((None, SUBLANES, LANES), lambda b, s: (b, 0, 0)),
    )
    return pl.pallas_call(
        _mixer_prompt_kernel,
        grid=(B, S // T),
        in_specs=[tm_spec if time_major else pl.BlockSpec((None, T, D), lambda b, s: (b, s, 0))]
                 + [_full(w.shape) for w in weights],
        out_specs=out_specs,
        out_shape=out_shape,
        scratch_shapes=[
            pltpu.VMEM((HIST + T, CONV_CH), F32),
            pltpu.VMEM((SUBLANES - 1, SHIFT_ROWS, CONV_CH), F32),
            pltpu.VMEM((T, D_MODEL), BF16),
            pltpu.VMEM((HEADS, HEAD_DIM, 2 * HEAD_DIM), F32),
            pltpu.VMEM((SUBLANES, LANES), F32),
            pltpu.VMEM((T, D_MODEL), BF16),
        ],
        compiler_params=pltpu.CompilerParams(
            dimension_semantics=("arbitrary", "arbitrary"), vmem_limit_bytes=VMEM_LIMIT),
        name="mixer_prompt",
    )(x, *weights)


def _ffn_tail(x, f_ref, p, w_down_ref, g_post_ref, g_ple_ref, w_ple_ref, w_pg_ref):
    x2 = x + _rms(_dot(f_ref[...], w_down_ref[...]), g_post_ref[...])
    emb = _dot(p.astype(BF16), w_ple_ref[...])
    gate = _sigmoid(_dot(_rms(x2, g_ple_ref[...]).astype(BF16), w_pg_ref[...]))
    return x2 + emb * gate


def _ffn_prompt_kernel(x_ref, p_ref, g_pre_ref, w_up_ref, w_conv_ref, b_conv_ref, w_down_ref,
                       g_post_ref, g_ple_ref, w_ple_ref, w_pg_ref,
                       y_ref, tail_out_ref,
                       hist_ref, f_ref, *, batch_major_out):
    B, steps = p_ref.shape[0], p_ref.shape[1]
    T = B * steps
    s_idx = pl.program_id(0)
    last = pl.num_programs(0) - 1

    @pl.when(s_idx == 0)
    def _():
        hist_ref[...] = jnp.zeros(hist_ref.shape, F32)

    x = x_ref[...]
    p = jnp.concatenate([p_ref[:, t, :] for t in range(steps)], axis=0)
    h = _rms(x, g_pre_ref[...]).astype(BF16)
    for c in range(D_FF // FFN_CHUNK):
        halves = []
        for half in range(2):
            c0 = half * D_FF + c * FFN_CHUNK
            cs = slice(c0, c0 + FFN_CHUNK)
            u = _dot(h, w_up_ref[:, cs])
            prev = hist_ref[:, cs]
            hist_ref[:, cs] = u[T - FFN_TAIL * B:T, :]
            u1 = jnp.concatenate([prev[B:2 * B], u[0:T - B]], axis=0)
            u2 = jnp.concatenate([prev, u[0:T - 2 * B]], axis=0)
            halves.append(w_conv_ref[0:1, cs] * u2 + w_conv_ref[1:2, cs] * u1
                          + w_conv_ref[2:3, cs] * u + b_conv_ref[:, cs])
        f_ref[:, c * FFN_CHUNK:(c + 1) * FFN_CHUNK] = (_gelu_tanh(halves[0]) * halves[1]).astype(BF16)

    @pl.when(s_idx == last)
    def _():
        for k in range(FFN_TAIL):
            tail_out_ref[:, k, :] = hist_ref[k * B:(k + 1) * B, :]

    out = _ffn_tail(x, f_ref, p, w_down_ref, g_post_ref, g_ple_ref, w_ple_ref, w_pg_ref)
    if batch_major_out:
        for t in range(steps):
            y_ref[:, t, :] = out[t * B:(t + 1) * B, :]
    else:
        y_ref[...] = out


def _ffn_prompt(x, p, wts, batch_major_out, layer):
    D = D_MODEL
    S, B = x.shape[0], x.shape[1] // D
    assert B % SUBLANES == 0 and PROMPT_TILE % B == 0
    steps = PROMPT_TILE // B
    T = PROMPT_TILE
    weights = (wts["g_ffn_pre"], wts["w_up"], wts["w_conv_ffn"], wts["b_conv_ffn"], wts["w_down"],
               wts["g_ffn_post"], wts["g_ple"], wts["w_ple"], wts["w_ple_gate"])
    if batch_major_out:
        y_shape, y_spec = (B, S, D), pl.BlockSpec((B, steps, D), lambda s: (0, s, 0))
    else:
        y_shape, y_spec = (S * B, D), pl.BlockSpec((T, D), lambda s: (s, 0))
    y, tail = pl.pallas_call(
        functools.partial(_ffn_prompt_kernel, batch_major_out=batch_major_out),
        grid=(S // steps,),
        in_specs=[pl.BlockSpec((T, D), lambda s: (s, 0)),
                  pl.BlockSpec((None, B, steps, PLE_DIM), lambda s: (layer, 0, s, 0))]
                 + [_full(w.shape) for w in weights],
        out_specs=(y_spec, pl.BlockSpec((B, FFN_TAIL, 2 * D_FF), lambda s: (0, 0, 0))),
        out_shape=(jax.ShapeDtypeStruct(y_shape, F32),
                   jax.ShapeDtypeStruct((B, FFN_TAIL, 2 * D_FF), F32)),
        scratch_shapes=[
            pltpu.VMEM((FFN_TAIL * B, 2 * D_FF), F32),
            pltpu.VMEM((PROMPT_TILE, D_FF), BF16),
        ],
        compiler_params=pltpu.CompilerParams(
            dimension_semantics=("arbitrary",), vmem_limit_bytes=VMEM_LIMIT),
        name="ffn_prompt",
    )(x.reshape(S * B, D), p, *weights)
    return (y if batch_major_out else y.reshape(S, B * D)), tail


def _mixer_sample_kernel(x_ref, st_ref, c_ref, n_ref, mrow_ref,
                         g_pre_ref, w_in_ref, w_gate_ref, b_gate_ref, w_conv_ref, b_conv_ref,
                         g_cn_ref, b_cn_ref, g_mn_ref, w_out_ref, g_post_ref,
                         y_ref, conv_out_ref, c_out_ref, n_out_ref, m_out_ref,
                         ext_ref, q_ref, kt_ref, wv_ref, wk_ref, carried_ref, qn_ref, dec_ref, *, seq_len):
    NB = SAMPLE_SEQS
    R = NB * SEQ_PAD
    x = x_ref[...]
    h = _rms(x, g_pre_ref[...]).astype(BF16)

    zv = _dot(h, w_in_ref[:, 0:CONV_CH])
    zg = _dot(h, w_in_ref[:, CONV_CH:2 * CONV_CH])
    a = zv * _sigmoid(zg)
    ext_ref[:, 0:CONV_TAIL, :] = st_ref[...]
    ext_ref[:, CONV_TAIL:CONV_TAIL + SEQ_PAD, :] = a.reshape(NB, SEQ_PAD, CONV_CH)
    ext_ref[:, CONV_TAIL + SEQ_PAD:, :] = jnp.zeros((NB, SAMPLE_EXT_ROWS - CONV_TAIL - SEQ_PAD, CONV_CH), F32)
    acc = jnp.broadcast_to(b_conv_ref[...][None], (NB, SEQ_PAD, CONV_CH))
    for j in range(CONV_WIDTH):
        acc = acc + w_conv_ref[j:j + 1, :][None] * ext_ref[:, pl.ds(j, SEQ_PAD), :]
    conv_out_ref[...] = ext_ref[:, seq_len:seq_len + CONV_TAIL, :]
    mix_parts = _conv_branch_post(acc.reshape(R, CONV_CH), g_cn_ref, b_cn_ref)

    gates = _dot(h, w_gate_ref[...]) + b_gate_ref[...]
    row = lax.broadcasted_iota(jnp.int32, (R, R), 0)
    col = lax.broadcasted_iota(jnp.int32, (R, R), 1)
    same_seq = (row // SEQ_PAD) == (col // SEQ_PAD)
    causal = same_seq & (col <= row)
    bcum = _dot_exact(causal, _log_sigmoid(gates))
    mask = causal & ((col % SEQ_PAD) < seq_len)
    pick_last = same_seq & ((col % SEQ_PAD) == seq_len - 1)
    inter_all = bcum + mrow_ref[...]
    gates_t = gates.T
    bcum_t = bcum.T
    lane = lax.broadcasted_iota(jnp.int32, (R, LANES), 1)
    row_valid = (lax.broadcasted_iota(jnp.int32, (R, 1), 0) % SEQ_PAD) < seq_len
    stats = jnp.where((lane >= HEADS) & (lane < 2 * HEADS), bcum, 0.0)
    q_off = 2 * CONV_CH
    saved = []
    for hd in range(HEADS):
        c0 = hd * HEAD_DIM
        zq = _dot(h, w_in_ref[:, q_off + c0:q_off + c0 + HEAD_DIM]) * (HEAD_DIM ** -0.5)
        zk = _dot(h, w_in_ref[:, q_off + MLSTM_WIDTH + c0:q_off + MLSTM_WIDTH + c0 + HEAD_DIM])
        zvv = _dot(h, w_in_ref[:, q_off + 2 * MLSTM_WIDTH + c0:q_off + 2 * MLSTM_WIDTH + c0 + HEAD_DIM])
        zo = _dot(h, w_in_ref[:, q_off + 3 * MLSTM_WIDTH + c0:q_off + 3 * MLSTM_WIDTH + c0 + HEAD_DIM])
        i_row = gates_t[hd:hd + 1, :]
        b_row = bcum_t[HEADS + hd:HEADS + hd + 1, :]
        b_col = bcum[:, HEADS + hd:HEADS + hd + 1]
        d = jnp.where(mask, b_col - b_row + i_row, NEG)
        inter = inter_all[:, HEADS + hd:HEADS + hd + 1]
        m_t, w_inter, num, den = _mlstm_intra(zq.astype(BF16), zk.astype(BF16), zvv.astype(BF16), d, inter)
        stats = jnp.where(lane == hd, m_t, stats)
        stats = jnp.where(lane == 2 * HEADS + hd, w_inter, stats)
        q_ref[hd, 0:R, :] = zq
        q_ref[hd, R:R + SEQ_PAD, :] = jnp.zeros((SEQ_PAD, HEAD_DIM), F32)
        kt_ref[hd] = zk.T
        saved.append((m_t, w_inter, num, den, zo, zk, zvv))

    per_seq = _dot_exact(pick_last, stats)
    m_out_ref[...] = per_seq
    for hd in range(HEADS):
        zk, zvv = saved[hd][5], saved[hd][6]
        m_new = per_seq[:, hd:hd + 1]
        b_last = per_seq[:, HEADS + hd:HEADS + hd + 1]
        decay = per_seq[:, 2 * HEADS + hd:2 * HEADS + hd + 1]
        b_col = bcum[:, HEADS + hd:HEADS + hd + 1]
        i_col = gates[:, hd:hd + 1]
        ws = jnp.where(row_valid, jnp.exp(b_last - b_col + i_col - m_new), 0.0)
        wv_ref[hd] = (ws * zvv).astype(BF16)
        wk_ref[hd] = ws * zk
        dec_ref[hd] = jnp.broadcast_to(decay, (R, LANES))

    col_seq = lax.broadcasted_iota(jnp.int32, (HEAD_DIM, R), 1) // SEQ_PAD

    def per_sequence(b, carry):
        r0 = pl.multiple_of(b * SEQ_PAD, SEQ_PAD)
        for hd in range(HEADS):
            c_old = c_ref[b, hd]
            n_old = n_ref[b, hd:hd + 1, :]
            q2 = q_ref[hd, pl.ds(r0, 2 * SEQ_PAD), :]
            carried_ref[hd, pl.ds(r0, SEQ_PAD), :] = _dot(q2.astype(BF16), c_old.astype(BF16))[0:SEQ_PAD, :]
            qn = jnp.sum(q2[0:SEQ_PAD, :] * n_old, axis=1, keepdims=True)
            qn_ref[hd, pl.ds(r0, SEQ_PAD), :] = jnp.broadcast_to(qn, (SEQ_PAD, LANES))
            dec = dec_ref[hd, pl.ds(r0, 1), :]
            kt_b = jnp.where(col_seq == b, kt_ref[hd], 0.0).astype(BF16)
            c_out_ref[b, hd] = dec * c_old + _dot(kt_b, wv_ref[hd])
            n_out_ref[b, hd:hd + 1, :] = dec * n_old + jnp.sum(wk_ref[hd, pl.ds(r0, SEQ_PAD), :], axis=0,
                                                                keepdims=True)
        return carry

    lax.fori_loop(0, NB, per_sequence, 0)

    for hd in range(HEADS):
        c0 = hd * HEAD_DIM
        m_t, w_inter, num, den, zo = saved[hd][:5]
        num = num + carried_ref[hd] * w_inter
        den = den + qn_ref[hd][:, 0:1] * w_inter
        mix_parts.append(_head_out(num, den, m_t, g_mn_ref[:, c0:c0 + HEAD_DIM], zo))

    mix = jnp.concatenate(mix_parts, axis=1).astype(BF16)
    y_ref[...] = x + _rms(_dot(mix, w_out_ref[...]), g_post_ref[...])


def _mixer_sample(x, st, c, n, mrow, wts, seq_len, layer):
    NB = SAMPLE_SEQS
    R = NB * SEQ_PAD
    nseq = c.shape[1]
    weights = (wts["g_mix_pre"], wts["w_in"], wts["w_gate"], wts["b_gate"], wts["w_conv_mix"],
               wts["b_conv_mix"], wts["g_conv_norm"], wts["b_conv_norm"], wts["g_mlstm_norm"],
               wts["w_out"], wts["g_mix_post"])
    rows = lambda width: pl.BlockSpec((R, width), lambda i: (i, 0))
    st_spec = pl.BlockSpec((NB, CONV_TAIL, CONV_CH), lambda i: (i, 0, 0))
    c_spec = pl.BlockSpec((NB, HEADS, HEAD_DIM, HEAD_DIM), lambda i: (i, 0, 0, 0))
    n_spec = pl.BlockSpec((NB, HEADS, HEAD_DIM), lambda i: (i, 0, 0))
    st_in = pl.BlockSpec((None, NB, CONV_TAIL, CONV_CH), lambda i: (layer, i, 0, 0))
    c_in = pl.BlockSpec((None, NB, HEADS, HEAD_DIM, HEAD_DIM), lambda i: (layer, i, 0, 0, 0))
    n_in = pl.BlockSpec((None, NB, HEADS, HEAD_DIM), lambda i: (layer, i, 0, 0))
    return pl.pallas_call(
        functools.partial(_mixer_sample_kernel, seq_len=seq_len),
        grid=(nseq // NB,),
        in_specs=[rows(D_MODEL), st_in, c_in, n_in, rows(LANES)] + [_full(w.shape) for w in weights],
        out_specs=(rows(D_MODEL), st_spec, c_spec, n_spec, rows(LANES)),
        out_shape=(jax.ShapeDtypeStruct(x.shape, F32), jax.ShapeDtypeStruct(st.shape[1:], F32),
                   jax.ShapeDtypeStruct(c.shape[1:], F32), jax.ShapeDtypeStruct(n.shape[1:], F32),
                   jax.ShapeDtypeStruct(mrow.shape, F32)),
        scratch_shapes=[
            pltpu.VMEM((NB, SAMPLE_EXT_ROWS, CONV_CH), F32),
            pltpu.VMEM((HEADS, R + SEQ_PAD, HEAD_DIM), F32),
            pltpu.VMEM((HEADS, HEAD_DIM, R), F32),
            pltpu.VMEM((HEADS, R, HEAD_DIM), BF16),
            pltpu.VMEM((HEADS, R, HEAD_DIM), F32),
            pltpu.VMEM((HEADS, R, HEAD_DIM), F32),
            pltpu.VMEM((HEADS, R, LANES), F32),
            pltpu.VMEM((HEADS, R, LANES), F32),
        ],
        compiler_params=pltpu.CompilerParams(
            dimension_semantics=("arbitrary",), vmem_limit_bytes=VMEM_LIMIT),
        name="mixer_sample",
    )(x, st, c, n, mrow, *weights)


def _ffn_sample_kernel(x_ref, p_ref, st_ref, g_pre_ref, w_up_ref, w_conv_ref, b_conv_ref, w_down_ref,
                       g_post_ref, g_ple_ref, w_ple_ref, w_pg_ref,
                       y_ref, tail_out_ref,
                       ubuf_ref, f_ref, *, seq_len):
    NB = SAMPLE_SEQS
    R = NB * SEQ_PAD
    x = x_ref[...]
    h = _rms(x, g_pre_ref[...]).astype(BF16)
    lo = SEQ_PAD - FFN_TAIL
    for c in range(D_FF // FFN_CHUNK):
        halves = []
        for half in range(2):
            c0 = half * D_FF + c * FFN_CHUNK
            cs = slice(c0, c0 + FFN_CHUNK)
            u = _dot(h, w_up_ref[:, cs])
            ubuf_ref[half, :, lo:SEQ_PAD, :] = st_ref[:, :, cs]
            ubuf_ref[half, :, SEQ_PAD:2 * SEQ_PAD, :] = u.reshape(NB, SEQ_PAD, FFN_CHUNK)
            tail_out_ref[:, :, cs] = ubuf_ref[half, :, lo + seq_len:SEQ_PAD + seq_len, :]
            y = (w_conv_ref[0:1, cs][None] * ubuf_ref[half, :, pl.ds(lo, SEQ_PAD), :]
                 + w_conv_ref[1:2, cs][None] * ubuf_ref[half, :, pl.ds(lo + 1, SEQ_PAD), :]
                 + w_conv_ref[2:3, cs][None] * ubuf_ref[half, :, pl.ds(lo + 2, SEQ_PAD), :]
                 + b_conv_ref[:, cs][None])
            halves.append(y.reshape(R, FFN_CHUNK))
        f_ref[:, c * FFN_CHUNK:(c + 1) * FFN_CHUNK] = (_gelu_tanh(halves[0]) * halves[1]).astype(BF16)
    y_ref[...] = _ffn_tail(x, f_ref, p_ref[...], w_down_ref, g_post_ref, g_ple_ref, w_ple_ref, w_pg_ref)


def _ffn_sample(x, p, st, wts, seq_len, layer):
    NB = SAMPLE_SEQS
    R = NB * SEQ_PAD
    nseq = st.shape[1]
    weights = (wts["g_ffn_pre"], wts["w_up"], wts["w_conv_ffn"], wts["b_conv_ffn"], wts["w_down"],
               wts["g_ffn_post"], wts["g_ple"], wts["w_ple"], wts["w_ple_gate"])
    st_spec = pl.BlockSpec((NB, FFN_TAIL, 2 * D_FF), lambda i: (i, 0, 0))
    st_in = pl.BlockSpec((None, NB, FFN_TAIL, 2 * D_FF), lambda i: (layer, i, 0, 0))
    return pl.pallas_call(
        functools.partial(_ffn_sample_kernel, seq_len=seq_len),
        grid=(nseq // NB,),
        in_specs=[pl.BlockSpec((R, D_MODEL), lambda i: (i, 0)), pl.BlockSpec((R, PLE_DIM), lambda i: (i, 0)),
                  st_in] + [_full(w.shape) for w in weights],
        out_specs=(pl.BlockSpec((R, D_MODEL), lambda i: (i, 0)), st_spec),
        out_shape=(jax.ShapeDtypeStruct(x.shape, F32), jax.ShapeDtypeStruct(st.shape[1:], F32)),
        scratch_shapes=[
            pltpu.VMEM((2, NB, 2 * SEQ_PAD, FFN_CHUNK), F32),
            pltpu.VMEM((R, D_FF), BF16),
        ],
        compiler_params=pltpu.CompilerParams(
            dimension_semantics=("arbitrary",), vmem_limit_bytes=VMEM_LIMIT),
        name="ffn_sample",
    )(x, p, st, *weights)


def _pad_seq(a):
    nseq, seq_len, width = a.shape
    return jnp.pad(a, ((0, 0), (0, SEQ_PAD - seq_len), (0, 0))).reshape(nseq * SEQ_PAD, width)


def _layer_weights(l, g_mix_pre, w_in, b_igate, b_fgate, w_conv_mix, b_conv_mix, g_conv_norm,
                   b_conv_norm, g_mlstm_norm, w_out, g_mix_post, g_ffn_pre, w_up, w_conv_ffn,
                   b_conv_ffn, w_down, g_ffn_post, g_ple, w_ple, w_ple_gate):
    row = lambda v: v[l][None, :].astype(F32)
    n_gate = 2 * HEADS
    w_gate = jnp.pad(w_in[l][:, MAIN_COLS:], ((0, 0), (0, LANES - n_gate))).astype(BF16)
    b_gate = jnp.pad(jnp.concatenate([b_igate[l], b_fgate[l]]), (0, LANES - n_gate))[None, :].astype(F32)
    return {
        "g_mix_pre": row(g_mix_pre), "w_in": w_in[l][:, :MAIN_COLS].astype(BF16),
        "w_gate": w_gate, "b_gate": b_gate,
        "w_conv_mix": w_conv_mix[l].astype(F32), "b_conv_mix": row(b_conv_mix),
        "g_conv_norm": row(g_conv_norm), "b_conv_norm": row(b_conv_norm),
        "g_mlstm_norm": row(g_mlstm_norm), "w_out": w_out[l].astype(BF16), "g_mix_post": row(g_mix_post),
        "g_ffn_pre": row(g_ffn_pre), "w_up": w_up[l].astype(BF16), "w_conv_ffn": w_conv_ffn[l].astype(F32),
        "b_conv_ffn": row(b_conv_ffn), "w_down": w_down[l].astype(BF16), "g_ffn_post": row(g_ffn_post),
        "g_ple": row(g_ple), "w_ple": w_ple[l].astype(BF16), "w_ple_gate": w_ple_gate[l].astype(BF16),
    }


def kernel(x_prompt, x_sample, p_prompt, p_sample, state_conv_mix, state_mlstm_C, state_mlstm_n, state_mlstm_m, state_conv_ffn, g_mix_pre, w_in, b_igate, b_fgate, w_conv_mix, b_conv_mix, g_conv_norm, b_conv_norm, g_mlstm_norm, w_out, g_mix_post, g_ffn_pre, w_up, w_conv_ffn, b_conv_ffn, w_down, g_ffn_post, g_ple, w_ple, w_ple_gate):
    depth = w_in.shape[0]
    nseq, seq_len, _ = x_sample.shape
    assert FFN_TAIL <= seq_len <= SEQ_PAD and nseq % SAMPLE_SEQS == 0
    assert x_prompt.shape[1] % PROMPT_TILE == 0
    xp = x_prompt
    xs = _pad_seq(x_sample)
    pc, pC, pn, pm, pf = [], [], [], [], []
    sc, sC, sn, sm, sf = [], [], [], [], []
    for l in range(depth):
        wts = _layer_weights(l, g_mix_pre, w_in, b_igate, b_fgate, w_conv_mix, b_conv_mix, g_conv_norm,
                             b_conv_norm, g_mlstm_norm, w_out, g_mix_post, g_ffn_pre, w_up, w_conv_ffn,
                             b_conv_ffn, w_down, g_ffn_post, g_ple, w_ple, w_ple_gate)
        xp, c1, C1, n1, m1 = _mixer_prompt(xp, wts, time_major=l > 0)
        xp, f1 = _ffn_prompt(xp, p_prompt, wts, batch_major_out=l == depth - 1, layer=l)
        pc.append(c1); pC.append(C1); pn.append(n1); pm.append(m1[:, :HEADS, 0]); pf.append(f1)

        mrow = jnp.pad(jnp.repeat(state_mlstm_m[l].astype(F32), SEQ_PAD, axis=0),
                       ((0, 0), (HEADS, LANES - 2 * HEADS)))
        xs, c2, C2, n2, m2 = _mixer_sample(xs, state_conv_mix, state_mlstm_C, state_mlstm_n, mrow,
                                           wts, seq_len, l)
        xs, f2 = _ffn_sample(xs, _pad_seq(p_sample[l]), state_conv_ffn, wts, seq_len, l)
        sc.append(c2); sC.append(C2); sn.append(n2); sf.append(f2)
        sm.append(m2.reshape(nseq, SEQ_PAD, LANES)[:, 0, :HEADS])
    ys = xs.reshape(nseq, SEQ_PAD, D_MODEL)[:, :seq_len]
    return (xp, ys, jnp.stack(pc), jnp.stack(pC), jnp.stack(pn), jnp.stack(pm), jnp.stack(pf),
            jnp.stack(sc), jnp.stack(sC), jnp.stack(sn), jnp.stack(sm), jnp.stack(sf))
```

```python
import functools

import jax
import jax.numpy as jnp
from jax import lax
from jax.experimental import pallas as pl
from jax.experimental.pallas import tpu as pltpu

F32 = jnp.float32
BF16 = jnp.bfloat16

D_MODEL = 1024
CONV_CH = 512
CONV_WIDTH = 31
CONV_TAIL = CONV_WIDTH - 1
CONV_GROUPS = 4
HEADS = 4
HEAD_DIM = 128
MLSTM_WIDTH = HEADS * HEAD_DIM
D_FF = 2816
FFN_TAIL = 2
PLE_DIM = 256
EPS = 1e-6
MAIN_COLS = 2 * CONV_CH + 4 * MLSTM_WIDTH
LANES = 128
SUBLANES = 8
NEG = -1e30

PROMPT_TILE = 256
SAMPLE_SEQS = 16
SEQ_PAD = SUBLANES
HIST = 32
SHIFT_ROWS = PROMPT_TILE + HIST - SUBLANES
CONV_ROWS = 64
FFN_CHUNK = 256
SAMPLE_EXT_ROWS = -(-(CONV_TAIL + SEQ_PAD) // SUBLANES) * SUBLANES
VMEM_LIMIT = 56 * 1024 * 1024


def _dot(a, b):
    return jnp.dot(a, b, preferred_element_type=F32)


def _dot_exact(sel, x):
    hi = x.astype(BF16)
    r1 = x - hi.astype(F32)
    mid = r1.astype(BF16)
    lo = (r1 - mid.astype(F32)).astype(BF16)
    y = _dot(jnp.where(sel, 1.0, 0.0).astype(BF16), jnp.concatenate([hi, mid, lo], axis=1))
    return y[:, 0:LANES] + y[:, LANES:2 * LANES] + y[:, 2 * LANES:3 * LANES]


def _rms(x, g):
    ms = jnp.mean(x * x, axis=-1, keepdims=True)
    return x * lax.rsqrt(ms + EPS) * g


def _layernorm(x):
    mu = jnp.mean(x, axis=-1, keepdims=True)
    xc = x - mu
    var = jnp.mean(xc * xc, axis=-1, keepdims=True)
    return xc * lax.rsqrt(var + EPS)


def _sigmoid(x):
    return 1.0 / (1.0 + jnp.exp(-x))


def _log_sigmoid(x):
    return jnp.minimum(x, 0.0) - jnp.log(1.0 + jnp.exp(-jnp.abs(x)))


def _gelu_tanh(x):
    return 0.5 * x * (1.0 + jnp.tanh(0.7978845608028654 * (x + 0.044715 * (x * x * x))))


def _conv_branch_post(acc, g_ref, b_ref):
    parts = []
    for g in range(CONV_GROUPS):
        sl = slice(g * LANES, (g + 1) * LANES)
        y = _layernorm(acc[:, sl]) * g_ref[:, sl] + b_ref[:, sl]
        parts.append(y * _sigmoid(y))
    return parts


def _mlstm_intra(q_bf, k_bf, v_bf, d, inter):
    m_t = jnp.maximum(inter, jnp.max(d, axis=1, keepdims=True))
    w_intra = jnp.exp(d - m_t)
    w_inter = jnp.exp(inter - m_t)
    s = lax.dot_general(q_bf, k_bf, (((1,), (1,)), ((), ())), preferred_element_type=F32) * w_intra
    num = _dot(s.astype(BF16), v_bf)
    den = jnp.sum(s, axis=1, keepdims=True)
    return m_t, w_inter, num, den


def _head_out(num, den, m_t, g_mn, zo):
    hh = num / jnp.maximum(jnp.abs(den), jnp.exp(-m_t))
    return _layernorm(hh) * g_mn * _sigmoid(zo)


def _mixer_prompt_kernel(x_ref, g_pre_ref, w_in_ref, w_gate_ref, b_gate_ref, w_conv_ref, b_conv_ref,
                         g_cn_ref, b_cn_ref, g_mn_ref, w_out_ref, g_post_ref,
                         y_ref, conv_out_ref, c_out_ref, n_out_ref, m_out_ref,
                         ext_ref, sh_ref, mix_ref, cn_ref, m_ref, h_ref):
    T = PROMPT_TILE
    s_idx = pl.program_id(1)
    last = pl.num_programs(1) - 1

    @pl.when(s_idx == 0)
    def _():
        ext_ref[0:HIST, :] = jnp.zeros((HIST, CONV_CH), F32)
        cn_ref[...] = jnp.zeros(cn_ref.shape, F32)
        m_ref[...] = jnp.zeros(m_ref.shape, F32)

    x = x_ref[...]
    h_ref[...] = _rms(x, g_pre_ref[...]).astype(BF16)

    zv = _dot(h_ref[...], w_in_ref[:, 0:CONV_CH])
    zg = _dot(h_ref[...], w_in_ref[:, CONV_CH:2 * CONV_CH])
    gates = _dot(h_ref[...], w_gate_ref[...]) + b_gate_ref[...]
    row = lax.broadcasted_iota(jnp.int32, (T, T), 0)
    col = lax.broadcasted_iota(jnp.int32, (T, T), 1)
    causal = col <= row
    bcum = _dot_exact(causal, _log_sigmoid(gates))
    gates_t = gates.T
    bcum_t = bcum.T
    q_off = 2 * CONV_CH
    zq_all = _dot(h_ref[...], w_in_ref[:, q_off:q_off + MLSTM_WIDTH])
    zk_all = _dot(h_ref[...], w_in_ref[:, q_off + MLSTM_WIDTH:q_off + 2 * MLSTM_WIDTH])
    zv_all = _dot(h_ref[...], w_in_ref[:, q_off + 2 * MLSTM_WIDTH:q_off + 3 * MLSTM_WIDTH])
    zo_all = _dot(h_ref[...], w_in_ref[:, q_off + 3 * MLSTM_WIDTH:q_off + 4 * MLSTM_WIDTH])

    ext_ref[HIST:HIST + T, :] = zv * _sigmoid(zg)
    for r in range(1, SUBLANES):
        sh_ref[r - 1] = ext_ref[pl.ds(r, SHIFT_ROWS), :]
    for g in range(CONV_GROUPS):
        cs = slice(g * LANES, (g + 1) * LANES)
        for rb in range(T // CONV_ROWS):
            acc = jnp.broadcast_to(b_conv_ref[:, cs], (CONV_ROWS, LANES))
            for j in range(CONV_WIDTH):
                off = HIST - CONV_TAIL + j
                r, base = off % SUBLANES, rb * CONV_ROWS + off - off % SUBLANES
                src = ext_ref if r == 0 else sh_ref.at[r - 1]
                acc = acc + w_conv_ref[j:j + 1, cs] * src[base:base + CONV_ROWS, cs]
            y = _layernorm(acc) * g_cn_ref[:, cs] + b_cn_ref[:, cs]
            mix_ref[rb * CONV_ROWS:(rb + 1) * CONV_ROWS, cs] = (y * _sigmoid(y)).astype(BF16)
    ext_ref[0:HIST, :] = ext_ref[T:T + HIST, :]

    for hd in range(HEADS):
        c0 = hd * HEAD_DIM
        zvv = zv_all[:, c0:c0 + HEAD_DIM]
        zo = zo_all[:, c0:c0 + HEAD_DIM]
        q_bf = (zq_all[:, c0:c0 + HEAD_DIM] * (HEAD_DIM ** -0.5)).astype(BF16)
        k_bf = zk_all[:, c0:c0 + HEAD_DIM].astype(BF16)
        v_bf = zvv.astype(BF16)

        i_row = gates_t[hd:hd + 1, :]
        i_col = gates[:, hd:hd + 1]
        b_row = bcum_t[HEADS + hd:HEADS + hd + 1, :]
        b_col = bcum[:, HEADS + hd:HEADS + hd + 1]
        m_prev = m_ref[hd:hd + 1, 0:1]
        d = jnp.where(causal, b_col - b_row + i_row, NEG)
        inter = b_col + m_prev
        m_t, w_inter, num, den = _mlstm_intra(q_bf, k_bf, v_bf, d, inter)

        cn = cn_ref[hd]
        carried = _dot(q_bf, cn.astype(BF16))
        num = num + carried[:, 0:HEAD_DIM] * w_inter
        den = den + carried[:, HEAD_DIM:HEAD_DIM + 1] * w_inter
        mix_ref[:, CONV_CH + c0:CONV_CH + c0 + HEAD_DIM] = _head_out(
            num, den, m_t, g_mn_ref[:, c0:c0 + HEAD_DIM], zo).astype(BF16)

        b_last = b_col[T - 1:T, :]
        m_new = m_t[T - 1:T, :]
        decay = w_inter[T - 1:T, :]
        ws = jnp.exp(b_last - b_col + i_col - m_new)
        vp = jnp.concatenate([ws * zvv, jnp.broadcast_to(ws, (T, HEAD_DIM))], axis=1).astype(BF16)
        cn_new = decay * cn + lax.dot_general(k_bf, vp, (((0,), (0,)), ((), ())),
                                              preferred_element_type=F32)
        cn_ref[hd] = cn_new
        m_ref[hd:hd + 1, :] = jnp.broadcast_to(m_new, (1, LANES))

    y_ref[...] = x + _rms(_dot(mix_ref[...], w_out_ref[...]), g_post_ref[...])

    @pl.when(s_idx == last)
    def _():
        conv_out_ref[...] = ext_ref[pl.ds(HIST - CONV_TAIL, CONV_TAIL), :]
        m_out_ref[...] = m_ref[...]
        for hd in range(HEADS):
            cn = cn_ref[hd]
            c_out_ref[hd] = cn[:, 0:HEAD_DIM]
            n_out_ref[hd:hd + 1, :] = cn[:, HEAD_DIM:].T[0:1, :]


def _full(shape):
    n = len(shape)
    return pl.BlockSpec(shape, lambda *_: (0,) * n)


def _mixer_prompt(x, wts):
    B, S, D = x.shape
    T = PROMPT_TILE
    weights = (wts["g_mix_pre"], wts["w_in"], wts["w_gate"], wts["b_gate"], wts["w_conv_mix"],
               wts["b_conv_mix"], wts["g_conv_norm"], wts["b_conv_norm"], wts["g_mlstm_norm"],
               wts["w_out"], wts["g_mix_post"])
    out_shape = (
        jax.ShapeDtypeStruct((B, S, D), F32),
        jax.ShapeDtypeStruct((B, CONV_TAIL, CONV_CH), F32),
        jax.ShapeDtypeStruct((B, HEADS, HEAD_DIM, HEAD_DIM), F32),
        jax.ShapeDtypeStruct((B, HEADS, HEAD_DIM), F32),
        jax.ShapeDtypeStruct((B, SUBLANES, LANES), F32),
    )
    out_specs = (
        pl.BlockSpec((None, T, D), lambda b, s: (b, s, 0)),
        pl.BlockSpec((None, CONV_TAIL, CONV_CH), lambda b, s: (b, 0, 0)),
        pl.BlockSpec((None, HEADS, HEAD_DIM, HEAD_DIM), lambda b, s: (b, 0, 0, 0)),
        pl.BlockSpec((None, HEADS, HEAD_DIM), lambda b, s: (b, 0, 0)),
        pl.BlockSpec((None, SUBLANES, LANES), lambda b, s: (b, 0, 0)),
    )
    return pl.pallas_call(
        _mixer_prompt_kernel,
        grid=(B, S // T),
        in_specs=[pl.BlockSpec((None, T, D), lambda b, s: (b, s, 0))] + [_full(w.shape) for w in weights],
        out_specs=out_specs,
        out_shape=out_shape,
        scratch_shapes=[
            pltpu.VMEM((HIST + T, CONV_CH), F32),
            pltpu.VMEM((SUBLANES - 1, SHIFT_ROWS, CONV_CH), F32),
            pltpu.VMEM((T, D_MODEL), BF16),
            pltpu.VMEM((HEADS, HEAD_DIM, 2 * HEAD_DIM), F32),
            pltpu.VMEM((SUBLANES, LANES), F32),
            pltpu.VMEM((T, D_MODEL), BF16),
        ],
        compiler_params=pltpu.CompilerParams(
            dimension_semantics=("arbitrary", "arbitrary"), vmem_limit_bytes=VMEM_LIMIT),
        name="mixer_prompt",
    )(x, *weights)


def _ffn_tail(x, f_ref, p, w_down_ref, g_post_ref, g_ple_ref, w_ple_ref, w_pg_ref):
    x2 = x + _rms(_dot(f_ref[...], w_down_ref[...]), g_post_ref[...])
    emb = _dot(p.astype(BF16), w_ple_ref[...])
    gate = _sigmoid(_dot(_rms(x2, g_ple_ref[...]).astype(BF16), w_pg_ref[...]))
    return x2 + emb * gate


def _ffn_prompt_kernel(x_ref, p_ref, g_pre_ref, w_up_ref, w_conv_ref, b_conv_ref, w_down_ref,
                       g_post_ref, g_ple_ref, w_ple_ref, w_pg_ref,
                       y_ref, tail_out_ref,
                       hist_ref, ubuf_ref, f_ref):
    T = PROMPT_TILE
    s_idx = pl.program_id(1)
    last = pl.num_programs(1) - 1

    @pl.when(s_idx == 0)
    def _():
        hist_ref[...] = jnp.zeros(hist_ref.shape, F32)

    x = x_ref[...]
    h = _rms(x, g_pre_ref[...]).astype(BF16)
    for c in range(D_FF // FFN_CHUNK):
        halves = []
        for half in range(2):
            c0 = half * D_FF + c * FFN_CHUNK
            cs = slice(c0, c0 + FFN_CHUNK)
            u = _dot(h, w_up_ref[:, cs])
            ubuf_ref[half, 0:SUBLANES, :] = hist_ref[:, cs]
            ubuf_ref[half, SUBLANES:SUBLANES + T, :] = u
            hist_ref[:, cs] = u[T - SUBLANES:T, :]
            y = (w_conv_ref[0:1, cs] * ubuf_ref[half, pl.ds(SUBLANES - 2, T), :]
                 + w_conv_ref[1:2, cs] * ubuf_ref[half, pl.ds(SUBLANES - 1, T), :]
                 + w_conv_ref[2:3, cs] * u + b_conv_ref[:, cs])
            halves.append(y)
        f_ref[:, c * FFN_CHUNK:(c + 1) * FFN_CHUNK] = (_gelu_tanh(halves[0]) * halves[1]).astype(BF16)

    y_ref[...] = _ffn_tail(x, f_ref, p_ref[...], w_down_ref, g_post_ref, g_ple_ref, w_ple_ref, w_pg_ref)

    @pl.when(s_idx == last)
    def _():
        tail_out_ref[...] = hist_ref[SUBLANES - FFN_TAIL:SUBLANES, :]


def _ffn_prompt(x, p, wts, layer):
    B, S, D = x.shape
    T = PROMPT_TILE
    weights = (wts["g_ffn_pre"], wts["w_up"], wts["w_conv_ffn"], wts["b_conv_ffn"], wts["w_down"],
               wts["g_ffn_post"], wts["g_ple"], wts["w_ple"], wts["w_ple_gate"])
    return pl.pallas_call(
        _ffn_prompt_kernel,
        grid=(B, S // T),
        in_specs=[pl.BlockSpec((None, T, D), lambda b, s: (b, s, 0)),
                  pl.BlockSpec((None, None, T, PLE_DIM), lambda b, s: (layer, b, s, 0))]
                 + [_full(w.shape) for w in weights],
        out_specs=(pl.BlockSpec((None, T, D), lambda b, s: (b, s, 0)),
                   pl.BlockSpec((None, FFN_TAIL, 2 * D_FF), lambda b, s: (b, 0, 0))),
        out_shape=(jax.ShapeDtypeStruct((B, S, D), F32),
                   jax.ShapeDtypeStruct((B, FFN_TAIL, 2 * D_FF), F32)),
        scratch_shapes=[
            pltpu.VMEM((SUBLANES, 2 * D_FF), F32),
            pltpu.VMEM((2, SUBLANES + T, FFN_CHUNK), F32),
            pltpu.VMEM((T, D_FF), BF16),
        ],
        compiler_params=pltpu.CompilerParams(
            dimension_semantics=("arbitrary", "arbitrary"), vmem_limit_bytes=VMEM_LIMIT),
        name="ffn_prompt",
    )(x, p, *weights)


def _mixer_sample_kernel(x_ref, st_ref, c_ref, n_ref, mrow_ref,
                         g_pre_ref, w_in_ref, w_gate_ref, b_gate_ref, w_conv_ref, b_conv_ref,
                         g_cn_ref, b_cn_ref, g_mn_ref, w_out_ref, g_post_ref,
                         y_ref, conv_out_ref, c_out_ref, n_out_ref, m_out_ref,
                         ext_ref, q_ref, kt_ref, wv_ref, wk_ref, carried_ref, qn_ref, dec_ref, *, seq_len):
    NB = SAMPLE_SEQS
    R = NB * SEQ_PAD
    x = x_ref[...]
    h = _rms(x, g_pre_ref[...]).astype(BF16)

    zv = _dot(h, w_in_ref[:, 0:CONV_CH])
    zg = _dot(h, w_in_ref[:, CONV_CH:2 * CONV_CH])
    a = zv * _sigmoid(zg)
    ext_ref[:, 0:CONV_TAIL, :] = st_ref[...]
    ext_ref[:, CONV_TAIL:CONV_TAIL + SEQ_PAD, :] = a.reshape(NB, SEQ_PAD, CONV_CH)
    ext_ref[:, CONV_TAIL + SEQ_PAD:, :] = jnp.zeros((NB, SAMPLE_EXT_ROWS - CONV_TAIL - SEQ_PAD, CONV_CH), F32)
    acc = jnp.broadcast_to(b_conv_ref[...][None], (NB, SEQ_PAD, CONV_CH))
    for j in range(CONV_WIDTH):
        acc = acc + w_conv_ref[j:j + 1, :][None] * ext_ref[:, pl.ds(j, SEQ_PAD), :]
    conv_out_ref[...] = ext_ref[:, seq_len:seq_len + CONV_TAIL, :]
    mix_parts = _conv_branch_post(acc.reshape(R, CONV_CH), g_cn_ref, b_cn_ref)

    gates = _dot(h, w_gate_ref[...]) + b_gate_ref[...]
    row = lax.broadcasted_iota(jnp.int32, (R, R), 0)
    col = lax.broadcasted_iota(jnp.int32, (R, R), 1)
    same_seq = (row // SEQ_PAD) == (col // SEQ_PAD)
    causal = same_seq & (col <= row)
    bcum = _dot_exact(causal, _log_sigmoid(gates))
    mask = causal & ((col % SEQ_PAD) < seq_len)
    pick_last = same_seq & ((col % SEQ_PAD) == seq_len - 1)
    inter_all = bcum + mrow_ref[...]
    gates_t = gates.T
    bcum_t = bcum.T
    lane = lax.broadcasted_iota(jnp.int32, (R, LANES), 1)
    row_valid = (lax.broadcasted_iota(jnp.int32, (R, 1), 0) % SEQ_PAD) < seq_len
    stats = jnp.where((lane >= HEADS) & (lane < 2 * HEADS), bcum, 0.0)
    q_off = 2 * CONV_CH
    saved = []
    for hd in range(HEADS):
        c0 = hd * HEAD_DIM
        zq = _dot(h, w_in_ref[:, q_off + c0:q_off + c0 + HEAD_DIM]) * (HEAD_DIM ** -0.5)
        zk = _dot(h, w_in_ref[:, q_off + MLSTM_WIDTH + c0:q_off + MLSTM_WIDTH + c0 + HEAD_DIM])
        zvv = _dot(h, w_in_ref[:, q_off + 2 * MLSTM_WIDTH + c0:q_off + 2 * MLSTM_WIDTH + c0 + HEAD_DIM])
        zo = _dot(h, w_in_ref[:, q_off + 3 * MLSTM_WIDTH + c0:q_off + 3 * MLSTM_WIDTH + c0 + HEAD_DIM])
        i_row = gates_t[hd:hd + 1, :]
        b_row = bcum_t[HEADS + hd:HEADS + hd + 1, :]
        b_col = bcum[:, HEADS + hd:HEADS + hd + 1]
        d = jnp.where(mask, b_col - b_row + i_row, NEG)
        inter = inter_all[:, HEADS + hd:HEADS + hd + 1]
        m_t, w_inter, num, den = _mlstm_intra(zq.astype(BF16), zk.astype(BF16), zvv.astype(BF16), d, inter)
        stats = jnp.where(lane == hd, m_t, stats)
        stats = jnp.where(lane == 2 * HEADS + hd, w_inter, stats)
        q_ref[hd, 0:R, :] = zq
        q_ref[hd, R:R + SEQ_PAD, :] = jnp.zeros((SEQ_PAD, HEAD_DIM), F32)
        kt_ref[hd] = zk.T
        saved.append((m_t, w_inter, num, den, zo, zk, zvv))

    per_seq = _dot_exact(pick_last, stats)
    m_out_ref[...] = per_seq
    for hd in range(HEADS):
        zk, zvv = saved[hd][5], saved[hd][6]
        m_new = per_seq[:, hd:hd + 1]
        b_last = per_seq[:, HEADS + hd:HEADS + hd + 1]
        decay = per_seq[:, 2 * HEADS + hd:2 * HEADS + hd + 1]
        b_col = bcum[:, HEADS + hd:HEADS + hd + 1]
        i_col = gates[:, hd:hd + 1]
        ws = jnp.where(row_valid, jnp.exp(b_last - b_col + i_col - m_new), 0.0)
        wv_ref[hd] = (ws * zvv).astype(BF16)
        wk_ref[hd] = ws * zk
        dec_ref[hd] = jnp.broadcast_to(decay, (R, LANES))

    col_seq = lax.broadcasted_iota(jnp.int32, (HEAD_DIM, R), 1) // SEQ_PAD

    def per_sequence(b, carry):
        r0 = pl.multiple_of(b * SEQ_PAD, SEQ_PAD)
        for hd in range(HEADS):
            c_old = c_ref[b, hd]
            n_old = n_ref[b, hd:hd + 1, :]
            q2 = q_ref[hd, pl.ds(r0, 2 * SEQ_PAD), :]
            carried_ref[hd, pl.ds(r0, SEQ_PAD), :] = _dot(q2.astype(BF16), c_old.astype(BF16))[0:SEQ_PAD, :]
            qn = jnp.sum(q2[0:SEQ_PAD, :] * n_old, axis=1, keepdims=True)
            qn_ref[hd, pl.ds(r0, SEQ_PAD), :] = jnp.broadcast_to(qn, (SEQ_PAD, LANES))
            dec = dec_ref[hd, pl.ds(r0, 1), :]
            kt_b = jnp.where(col_seq == b, kt_ref[hd], 0.0).astype(BF16)
            c_out_ref[b, hd] = dec * c_old + _dot(kt_b, wv_ref[hd])
            n_out_ref[b, hd:hd + 1, :] = dec * n_old + jnp.sum(wk_ref[hd, pl.ds(r0, SEQ_PAD), :], axis=0,
                                                                keepdims=True)
        return carry

    lax.fori_loop(0, NB, per_sequence, 0)

    for hd in range(HEADS):
        c0 = hd * HEAD_DIM
        m_t, w_inter, num, den, zo = saved[hd][:5]
        num = num + carried_ref[hd] * w_inter
        den = den + qn_ref[hd][:, 0:1] * w_inter
        mix_parts.append(_head_out(num, den, m_t, g_mn_ref[:, c0:c0 + HEAD_DIM], zo))

    mix = jnp.concatenate(mix_parts, axis=1).astype(BF16)
    y_ref[...] = x + _rms(_dot(mix, w_out_ref[...]), g_post_ref[...])


def _mixer_sample(x, st, c, n, mrow, wts, seq_len, layer):
    NB = SAMPLE_SEQS
    R = NB * SEQ_PAD
    nseq = c.shape[1]
    weights = (wts["g_mix_pre"], wts["w_in"], wts["w_gate"], wts["b_gate"], wts["w_conv_mix"],
               wts["b_conv_mix"], wts["g_conv_norm"], wts["b_conv_norm"], wts["g_mlstm_norm"],
               wts["w_out"], wts["g_mix_post"])
    rows = lambda width: pl.BlockSpec((R, width), lambda i: (i, 0))
    st_spec = pl.BlockSpec((NB, CONV_TAIL, CONV_CH), lambda i: (i, 0, 0))
    c_spec = pl.BlockSpec((NB, HEADS, HEAD_DIM, HEAD_DIM), lambda i: (i, 0, 0, 0))
    n_spec = pl.BlockSpec((NB, HEADS, HEAD_DIM), lambda i: (i, 0, 0))
    st_in = pl.BlockSpec((None, NB, CONV_TAIL, CONV_CH), lambda i: (layer, i, 0, 0))
    c_in = pl.BlockSpec((None, NB, HEADS, HEAD_DIM, HEAD_DIM), lambda i: (layer, i, 0, 0, 0))
    n_in = pl.BlockSpec((None, NB, HEADS, HEAD_DIM), lambda i: (layer, i, 0, 0))
    return pl.pallas_call(
        functools.partial(_mixer_sample_kernel, seq_len=seq_len),
        grid=(nseq // NB,),
        in_specs=[rows(D_MODEL), st_in, c_in, n_in, rows(LANES)] + [_full(w.shape) for w in weights],
        out_specs=(rows(D_MODEL), st_spec, c_spec, n_spec, rows(LANES)),
        out_shape=(jax.ShapeDtypeStruct(x.shape, F32), jax.ShapeDtypeStruct(st.shape[1:], F32),
                   jax.ShapeDtypeStruct(c.shape[1:], F32), jax.ShapeDtypeStruct(n.shape[1:], F32),
                   jax.ShapeDtypeStruct(mrow.shape, F32)),
        scratch_shapes=[
            pltpu.VMEM((NB, SAMPLE_EXT_ROWS, CONV_CH), F32),
            pltpu.VMEM((HEADS, R + SEQ_PAD, HEAD_DIM), F32),
            pltpu.VMEM((HEADS, HEAD_DIM, R), F32),
            pltpu.VMEM((HEADS, R, HEAD_DIM), BF16),
            pltpu.VMEM((HEADS, R, HEAD_DIM), F32),
            pltpu.VMEM((HEADS, R, HEAD_DIM), F32),
            pltpu.VMEM((HEADS, R, LANES), F32),
            pltpu.VMEM((HEADS, R, LANES), F32),
        ],
        compiler_params=pltpu.CompilerParams(
            dimension_semantics=("arbitrary",), vmem_limit_bytes=VMEM_LIMIT),
        name="mixer_sample",
    )(x, st, c, n, mrow, *weights)


def _ffn_sample_kernel(x_ref, p_ref, st_ref, g_pre_ref, w_up_ref, w_conv_ref, b_conv_ref, w_down_ref,
                       g_post_ref, g_ple_ref, w_ple_ref, w_pg_ref,
                       y_ref, tail_out_ref,
                       ubuf_ref, f_ref, *, seq_len):
    NB = SAMPLE_SEQS
    R = NB * SEQ_PAD
    x = x_ref[...]
    h = _rms(x, g_pre_ref[...]).astype(BF16)
    lo = SEQ_PAD - FFN_TAIL
    for c in range(D_FF // FFN_CHUNK):
        halves = []
        for half in range(2):
            c0 = half * D_FF + c * FFN_CHUNK
            cs = slice(c0, c0 + FFN_CHUNK)
            u = _dot(h, w_up_ref[:, cs])
            ubuf_ref[half, :, lo:SEQ_PAD, :] = st_ref[:, :, cs]
            ubuf_ref[half, :, SEQ_PAD:2 * SEQ_PAD, :] = u.reshape(NB, SEQ_PAD, FFN_CHUNK)
            tail_out_ref[:, :, cs] = ubuf_ref[half, :, lo + seq_len:SEQ_PAD + seq_len, :]
            y = (w_conv_ref[0:1, cs][None] * ubuf_ref[half, :, pl.ds(lo, SEQ_PAD), :]
                 + w_conv_ref[1:2, cs][None] * ubuf_ref[half, :, pl.ds(lo + 1, SEQ_PAD), :]
                 + w_conv_ref[2:3, cs][None] * ubuf_ref[half, :, pl.ds(lo + 2, SEQ_PAD), :]
                 + b_conv_ref[:, cs][None])
            halves.append(y.reshape(R, FFN_CHUNK))
        f_ref[:, c * FFN_CHUNK:(c + 1) * FFN_CHUNK] = (_gelu_tanh(halves[0]) * halves[1]).astype(BF16)
    y_ref[...] = _ffn_tail(x, f_ref, p_ref[...], w_down_ref, g_post_ref, g_ple_ref, w_ple_ref, w_pg_ref)


def _ffn_sample(x, p, st, wts, seq_len, layer):
    NB = SAMPLE_SEQS
    R = NB * SEQ_PAD
    nseq = st.shape[1]
    weights = (wts["g_ffn_pre"], wts["w_up"], wts["w_conv_ffn"], wts["b_conv_ffn"], wts["w_down"],
               wts["g_ffn_post"], wts["g_ple"], wts["w_ple"], wts["w_ple_gate"])
    st_spec = pl.BlockSpec((NB, FFN_TAIL, 2 * D_FF), lambda i: (i, 0, 0))
    st_in = pl.BlockSpec((None, NB, FFN_TAIL, 2 * D_FF), lambda i: (layer, i, 0, 0))
    return pl.pallas_call(
        functools.partial(_ffn_sample_kernel, seq_len=seq_len),
        grid=(nseq // NB,),
        in_specs=[pl.BlockSpec((R, D_MODEL), lambda i: (i, 0)), pl.BlockSpec((R, PLE_DIM), lambda i: (i, 0)),
                  st_in] + [_full(w.shape) for w in weights],
        out_specs=(pl.BlockSpec((R, D_MODEL), lambda i: (i, 0)), st_spec),
        out_shape=(jax.ShapeDtypeStruct(x.shape, F32), jax.ShapeDtypeStruct(st.shape[1:], F32)),
        scratch_shapes=[
            pltpu.VMEM((2, NB, 2 * SEQ_PAD, FFN_CHUNK), F32),
            pltpu.VMEM((R, D_FF), BF16),
        ],
        compiler_params=pltpu.CompilerParams(
            dimension_semantics=("arbitrary",), vmem_limit_bytes=VMEM_LIMIT),
        name="ffn_sample",
    )(x, p, st, *weights)


def _pad_seq(a):
    nseq, seq_len, width = a.shape
    return jnp.pad(a, ((0, 0), (0, SEQ_PAD - seq_len), (0, 0))).reshape(nseq * SEQ_PAD, width)


def _layer_weights(l, g_mix_pre, w_in, b_igate, b_fgate, w_conv_mix, b_conv_mix, g_conv_norm,
                   b_conv_norm, g_mlstm_norm, w_out, g_mix_post, g_ffn_pre, w_up, w_conv_ffn,
                   b_conv_ffn, w_down, g_ffn_post, g_ple, w_ple, w_ple_gate):
    row = lambda v: v[l][None, :].astype(F32)
    n_gate = 2 * HEADS
    w_gate = jnp.pad(w_in[l][:, MAIN_COLS:], ((0, 0), (0, LANES - n_gate))).astype(BF16)
    b_gate = jnp.pad(jnp.concatenate([b_igate[l], b_fgate[l]]), (0, LANES - n_gate))[None, :].astype(F32)
    return {
        "g_mix_pre": row(g_mix_pre), "w_in": w_in[l][:, :MAIN_COLS].astype(BF16),
        "w_gate": w_gate, "b_gate": b_gate,
        "w_conv_mix": w_conv_mix[l].astype(F32), "b_conv_mix": row(b_conv_mix),
        "g_conv_norm": row(g_conv_norm), "b_conv_norm": row(b_conv_norm),
        "g_mlstm_norm": row(g_mlstm_norm), "w_out": w_out[l].astype(BF16), "g_mix_post": row(g_mix_post),
        "g_ffn_pre": row(g_ffn_pre), "w_up": w_up[l].astype(BF16), "w_conv_ffn": w_conv_ffn[l].astype(F32),
        "b_conv_ffn": row(b_conv_ffn), "w_down": w_down[l].astype(BF16), "g_ffn_post": row(g_ffn_post),
        "g_ple": row(g_ple), "w_ple": w_ple[l].astype(BF16), "w_ple_gate": w_ple_gate[l].astype(BF16),
    }


def kernel(x_prompt, x_sample, p_prompt, p_sample, state_conv_mix, state_mlstm_C, state_mlstm_n, state_mlstm_m, state_conv_ffn, g_mix_pre, w_in, b_igate, b_fgate, w_conv_mix, b_conv_mix, g_conv_norm, b_conv_norm, g_mlstm_norm, w_out, g_mix_post, g_ffn_pre, w_up, w_conv_ffn, b_conv_ffn, w_down, g_ffn_post, g_ple, w_ple, w_ple_gate):
    depth = w_in.shape[0]
    nseq, seq_len, _ = x_sample.shape
    assert FFN_TAIL <= seq_len <= SEQ_PAD and nseq % SAMPLE_SEQS == 0
    assert x_prompt.shape[1] % PROMPT_TILE == 0
    xp = x_prompt
    xs = _pad_seq(x_sample)
    pc, pC, pn, pm, pf = [], [], [], [], []
    sc, sC, sn, sm, sf = [], [], [], [], []
    for l in range(depth):
        wts = _layer_weights(l, g_mix_pre, w_in, b_igate, b_fgate, w_conv_mix, b_conv_mix, g_conv_norm,
                             b_conv_norm, g_mlstm_norm, w_out, g_mix_post, g_ffn_pre, w_up, w_conv_ffn,
                             b_conv_ffn, w_down, g_ffn_post, g_ple, w_ple, w_ple_gate)
        xp, c1, C1, n1, m1 = _mixer_prompt(xp, wts)
        xp, f1 = _ffn_prompt(xp, p_prompt, wts, l)
        pc.append(c1); pC.append(C1); pn.append(n1); pm.append(m1[:, :HEADS, 0]); pf.append(f1)

        mrow = jnp.pad(jnp.repeat(state_mlstm_m[l].astype(F32), SEQ_PAD, axis=0),
                       ((0, 0), (HEADS, LANES - 2 * HEADS)))
        xs, c2, C2, n2, m2 = _mixer_sample(xs, state_conv_mix, state_mlstm_C, state_mlstm_n, mrow,
                                           wts, seq_len, l)
        xs, f2 = _ffn_sample(xs, _pad_seq(p_sample[l]), state_conv_ffn, wts, seq_len, l)
        sc.append(c2); sC.append(C2); sn.append(n2); sf.append(f2)
        sm.append(m2.reshape(nseq, SEQ_PAD, LANES)[:, 0, :HEADS])
    ys = xs.reshape(nseq, SEQ_PAD, D_MODEL)[:, :seq_len]
    return (xp, ys, jnp.stack(pc), jnp.stack(pC), jnp.stack(pn), jnp.stack(pm), jnp.stack(pf),
            jnp.stack(sc), jnp.stack(sC), jnp.stack(sn), jnp.stack(sm), jnp.stack(sf))
```

```python
import functools

import jax
import jax.numpy as jnp
from jax import lax
from jax.experimental import pallas as pl
from jax.experimental.pallas import tpu as pltpu

F32 = jnp.float32
BF16 = jnp.bfloat16

D_MODEL = 1024
CONV_CH = 512
CONV_WIDTH = 31
CONV_TAIL = CONV_WIDTH - 1
CONV_GROUPS = 4
HEADS = 4
HEAD_DIM = 128
MLSTM_WIDTH = HEADS * HEAD_DIM
D_FF = 2816
FFN_TAIL = 2
PLE_DIM = 256
EPS = 1e-6
MAIN_COLS = 2 * CONV_CH + 4 * MLSTM_WIDTH
LANES = 128
SUBLANES = 8
NEG = -1e30

PROMPT_TILE = 256
SAMPLE_SEQS = 16
SAMPLE_SEQS_FFN = 32
SEQ_PAD = SUBLANES
HIST = 32
SHIFT_ROWS = PROMPT_TILE + HIST - SUBLANES
CONV_ROWS = 64
CONV_IN_FLIGHT = 2
FFN_ROWS = 64
FFN_IN_FLIGHT = 3
FFN_CHUNK = 256
SAMPLE_EXT_ROWS = -(-(CONV_TAIL + SEQ_PAD) // SUBLANES) * SUBLANES
VMEM_LIMIT = 56 * 1024 * 1024


def _dot(a, b):
    return jnp.dot(a, b, preferred_element_type=F32)


def _dot_exact(sel, x):
    hi = x.astype(BF16)
    r1 = x - hi.astype(F32)
    mid = r1.astype(BF16)
    lo = (r1 - mid.astype(F32)).astype(BF16)
    y = _dot(jnp.where(sel, 1.0, 0.0).astype(BF16), jnp.concatenate([hi, mid, lo], axis=1))
    return y[:, 0:LANES] + y[:, LANES:2 * LANES] + y[:, 2 * LANES:3 * LANES]


def _rms(x, g):
    ms = jnp.mean(x * x, axis=-1, keepdims=True)
    return x * lax.rsqrt(ms + EPS) * g


def _layernorm(x):
    mu = jnp.mean(x, axis=-1, keepdims=True)
    xc = x - mu
    var = jnp.mean(xc * xc, axis=-1, keepdims=True)
    return xc * lax.rsqrt(var + EPS)


def _sigmoid(x):
    return 1.0 / (1.0 + jnp.exp(-x))


def _log_sigmoid(x):
    return jnp.minimum(x, 0.0) - jnp.log(1.0 + jnp.exp(-jnp.abs(x)))


def _exact_zero(v):
    bits = pltpu.bitcast(v, jnp.uint32)
    bits = lax.shift_right_logical(lax.shift_right_logical(bits, jnp.uint32(16)), jnp.uint32(16))
    return pltpu.bitcast(bits, F32)


def _gelu_tanh(x):
    return 0.5 * x * (1.0 + jnp.tanh(0.7978845608028654 * (x + 0.044715 * (x * x * x))))


def _conv_branch_post(acc, g_ref, b_ref):
    parts = []
    for g in range(CONV_GROUPS):
        sl = slice(g * LANES, (g + 1) * LANES)
        y = _layernorm(acc[:, sl]) * g_ref[:, sl] + b_ref[:, sl]
        parts.append(y * _sigmoid(y))
    return parts


def _qk(q_bf, k_bf):
    return lax.dot_general(q_bf, k_bf, (((1,), (1,)), ((), ())), preferred_element_type=F32)


def _mlstm_weighted(scores, v_bf, d, inter):
    m_t = jnp.maximum(inter, jnp.max(d, axis=1, keepdims=True))
    w_intra = jnp.exp(d - m_t)
    w_inter = jnp.exp(inter - m_t)
    s = scores * w_intra
    num = _dot(s.astype(BF16), v_bf)
    den = jnp.sum(s, axis=1, keepdims=True)
    return m_t, w_inter, num, den


def _mlstm_intra(q_bf, k_bf, v_bf, d, inter):
    return _mlstm_weighted(_qk(q_bf, k_bf), v_bf, d, inter)


def _head_out(num, den, m_t, g_mn, zo):
    hh = num / jnp.maximum(jnp.abs(den), jnp.exp(-m_t))
    return _layernorm(hh) * g_mn * _sigmoid(zo)


def _mixer_prompt_kernel(x_ref, g_pre_ref, w_in_ref, w_gate_ref, b_gate_ref, w_conv_ref, b_conv_ref,
                         g_cn_ref, b_cn_ref, g_mn_ref, w_out_ref, g_post_ref,
                         y_ref, conv_out_ref, c_out_ref, n_out_ref, m_out_ref,
                         ext_ref, sh_ref, mix_ref, cn_ref, m_ref, h_ref, z_ref):
    T = PROMPT_TILE
    s_idx = pl.program_id(1)
    last = pl.num_programs(1) - 1

    @pl.when(s_idx == 0)
    def _():
        ext_ref[0:HIST, :] = jnp.zeros((HIST, CONV_CH), F32)
        cn_ref[...] = jnp.zeros(cn_ref.shape, F32)
        m_ref[...] = jnp.zeros(m_ref.shape, F32)

    x = x_ref[...]
    h_ref[...] = _rms(x, g_pre_ref[...]).astype(BF16)

    zv = _dot(h_ref[...], w_in_ref[:, 0:CONV_CH])
    zg = _dot(h_ref[...], w_in_ref[:, CONV_CH:2 * CONV_CH])
    gates = _dot(h_ref[...], w_gate_ref[...]) + b_gate_ref[...]
    q_off = 2 * CONV_CH
    slot = lax.rem(s_idx, 2)
    for part in range(4):
        ps = slice(part * MLSTM_WIDTH, (part + 1) * MLSTM_WIDTH)
        z_ref[slot, :, ps] = _dot(h_ref[...],
                                  w_in_ref[:, q_off + part * MLSTM_WIDTH:q_off + (part + 1) * MLSTM_WIDTH])
    row = lax.broadcasted_iota(jnp.int32, (T, T), 0)
    col = lax.broadcasted_iota(jnp.int32, (T, T), 1)
    causal = col <= row
    bcum = _dot_exact(causal, _log_sigmoid(gates))
    gates_t = gates.T
    bcum_t = bcum.T

    ext_ref[HIST:HIST + T, :] = zv * _sigmoid(zg)
    for r in range(1, SUBLANES):
        sh_ref[r - 1] = ext_ref[pl.ds(r, SHIFT_ROWS), :]
    recent = []
    for g in range(CONV_GROUPS):
        cs = slice(g * LANES, (g + 1) * LANES)
        for rb in range(T // CONV_ROWS):
            acc = jnp.broadcast_to(b_conv_ref[:, cs], (CONV_ROWS, LANES))
            for j in range(CONV_WIDTH):
                off = HIST - CONV_TAIL + j
                r, base = off % SUBLANES, rb * CONV_ROWS + off - off % SUBLANES
                src = ext_ref if r == 0 else sh_ref.at[r - 1]
                w_row = w_conv_ref[j:j + 1, cs]
                if j == 0 and len(recent) == CONV_IN_FLIGHT:
                    w_row = w_row + _exact_zero(recent.pop(0))
                acc = acc + w_row * src[base:base + CONV_ROWS, cs]
            y = _layernorm(acc) * g_cn_ref[:, cs] + b_cn_ref[:, cs]
            mix_ref[rb * CONV_ROWS:(rb + 1) * CONV_ROWS, cs] = (y * _sigmoid(y)).astype(BF16)
            recent.append(acc[0:1, :])
    ext_ref[0:HIST, :] = ext_ref[T:T + HIST, :]

    heads = []
    for hd in range(HEADS):
        c0 = hd * HEAD_DIM
        zvv = z_ref[slot, :, 2 * MLSTM_WIDTH + c0:2 * MLSTM_WIDTH + c0 + HEAD_DIM]
        q_bf = (z_ref[slot, :, c0:c0 + HEAD_DIM] * (HEAD_DIM ** -0.5)).astype(BF16)
        k_bf = z_ref[slot, :, MLSTM_WIDTH + c0:MLSTM_WIDTH + c0 + HEAD_DIM].astype(BF16)
        cn = cn_ref[hd]
        scores = _qk(q_bf, k_bf)
        carried = _dot(q_bf, cn.astype(BF16))
        heads.append((zvv, k_bf, cn, scores, carried))

    finished = []
    for hd in range(HEADS):
        zvv, k_bf, cn, scores, carried = heads[hd]
        i_row = gates_t[hd:hd + 1, :]
        i_col = gates[:, hd:hd + 1]
        b_row = bcum_t[HEADS + hd:HEADS + hd + 1, :]
        b_col = bcum[:, HEADS + hd:HEADS + hd + 1]
        m_prev = m_ref[hd:hd + 1, 0:1]
        d = jnp.where(causal, b_col - b_row + i_row, NEG)
        inter = b_col + m_prev
        m_t, w_inter, num, den = _mlstm_weighted(scores, zvv.astype(BF16), d, inter)
        b_last = b_col[T - 1:T, :]
        m_new = m_t[T - 1:T, :]
        decay = w_inter[T - 1:T, :]
        ws = jnp.exp(b_last - b_col + i_col - m_new)
        vp = jnp.concatenate([ws * zvv, jnp.broadcast_to(ws, (T, HEAD_DIM))], axis=1).astype(BF16)
        kv = lax.dot_general(k_bf, vp, (((0,), (0,)), ((), ())), preferred_element_type=F32)
        finished.append((m_t, w_inter, num, den, m_new, decay, kv))

    for hd in range(HEADS):
        c0 = hd * HEAD_DIM
        _, _, cn, _, carried = heads[hd]
        m_t, w_inter, num, den, m_new, decay, kv = finished[hd]
        zo = z_ref[slot, :, 3 * MLSTM_WIDTH + c0:3 * MLSTM_WIDTH + c0 + HEAD_DIM]
        num = num + carried[:, 0:HEAD_DIM] * w_inter
        den = den + carried[:, HEAD_DIM:HEAD_DIM + 1] * w_inter
        mix_ref[:, CONV_CH + c0:CONV_CH + c0 + HEAD_DIM] = _head_out(
            num, den, m_t, g_mn_ref[:, c0:c0 + HEAD_DIM], zo).astype(BF16)
        cn_ref[hd] = decay * cn + kv
        m_ref[hd:hd + 1, :] = jnp.broadcast_to(m_new, (1, LANES))

    y_ref[...] = x + _rms(_dot(mix_ref[...], w_out_ref[...]), g_post_ref[...])

    @pl.when(s_idx == last)
    def _():
        conv_out_ref[...] = ext_ref[pl.ds(HIST - CONV_TAIL, CONV_TAIL), :]
        m_out_ref[...] = m_ref[...]
        for hd in range(HEADS):
            cn = cn_ref[hd]
            c_out_ref[hd] = cn[:, 0:HEAD_DIM]
            n_out_ref[hd:hd + 1, :] = cn[:, HEAD_DIM:].T[0:1, :]


def _full(shape):
    n = len(shape)
    return pl.BlockSpec(shape, lambda *_: (0,) * n)


def _mixer_prompt(x, wts):
    B, S, D = x.shape
    T = PROMPT_TILE
    weights = (wts["g_mix_pre"], wts["w_in"], wts["w_gate"], wts["b_gate"], wts["w_conv_mix"],
               wts["b_conv_mix"], wts["g_conv_norm"], wts["b_conv_norm"], wts["g_mlstm_norm"],
               wts["w_out"], wts["g_mix_post"])
    out_shape = (
        jax.ShapeDtypeStruct((B, S, D), F32),
        jax.ShapeDtypeStruct((B, CONV_TAIL, CONV_CH), F32),
        jax.ShapeDtypeStruct((B, HEADS, HEAD_DIM, HEAD_DIM), F32),
        jax.ShapeDtypeStruct((B, HEADS, HEAD_DIM), F32),
        jax.ShapeDtypeStruct((B, SUBLANES, LANES), F32),
    )
    out_specs = (
        pl.BlockSpec((None, T, D), lambda b, s: (b, s, 0)),
        pl.BlockSpec((None, CONV_TAIL, CONV_CH), lambda b, s: (b, 0, 0)),
        pl.BlockSpec((None, HEADS, HEAD_DIM, HEAD_DIM), lambda b, s: (b, 0, 0, 0)),
        pl.BlockSpec((None, HEADS, HEAD_DIM), lambda b, s: (b, 0, 0)),
        pl.BlockSpec((None, SUBLANES, LANES), lambda b, s: (b, 0, 0)),
    )
    return pl.pallas_call(
        _mixer_prompt_kernel,
        grid=(B, S // T),
        in_specs=[pl.BlockSpec((None, T, D), lambda b, s: (b, s, 0))] + [_full(w.shape) for w in weights],
        out_specs=out_specs,
        out_shape=out_shape,
        scratch_shapes=[
            pltpu.VMEM((HIST + T, CONV_CH), F32),
            pltpu.VMEM((SUBLANES - 1, SHIFT_ROWS, CONV_CH), F32),
            pltpu.VMEM((T, D_MODEL), BF16),
            pltpu.VMEM((HEADS, HEAD_DIM, 2 * HEAD_DIM), F32),
            pltpu.VMEM((SUBLANES, LANES), F32),
            pltpu.VMEM((T, D_MODEL), BF16),
            pltpu.VMEM((2, T, 4 * MLSTM_WIDTH), F32),
        ],
        compiler_params=pltpu.CompilerParams(
            dimension_semantics=("arbitrary", "arbitrary"), vmem_limit_bytes=VMEM_LIMIT),
        name="mixer_prompt",
    )(x, *weights)


def _ffn_tail(x, f_ref, p, w_down_ref, g_post_ref, g_ple_ref, w_ple_ref, w_pg_ref):
    x2 = x + _rms(_dot(f_ref[...], w_down_ref[...]), g_post_ref[...])
    emb = _dot(p.astype(BF16), w_ple_ref[...])
    gate = _sigmoid(_dot(_rms(x2, g_ple_ref[...]).astype(BF16), w_pg_ref[...]))
    return x2 + emb * gate


def _ffn_prompt_kernel(x_ref, p_ref, g_pre_ref, w_up_ref, w_conv_ref, b_conv_ref, w_down_ref,
                       g_post_ref, g_ple_ref, w_ple_ref, w_pg_ref,
                       y_ref, tail_out_ref,
                       hist_ref, f_ref):
    T = PROMPT_TILE
    G = T // SUBLANES
    s_idx = pl.program_id(1)
    last = pl.num_programs(1) - 1

    @pl.when(s_idx == 0)
    def _():
        hist_ref[...] = jnp.zeros(hist_ref.shape, F32)

    x = x_ref[...]
    h = _rms(x, g_pre_ref[...]).astype(BF16)
    sub = lax.broadcasted_iota(jnp.int32, (T, FFN_CHUNK), 0) % SUBLANES

    def delayed(u, prev, k):
        rot = jnp.concatenate([pltpu.roll(prev, k, 0)]
                              + [pltpu.roll(u[g * SUBLANES:(g + 1) * SUBLANES, :], k, 0) for g in range(G)], axis=0)
        return jnp.where(sub < k, rot[0:T, :], rot[SUBLANES:SUBLANES + T, :])

    for c in range(D_FF // FFN_CHUNK):
        halves = []
        for half in range(2):
            c0 = half * D_FF + c * FFN_CHUNK
            cs = slice(c0, c0 + FFN_CHUNK)
            u = _dot(h, w_up_ref[:, cs])
            prev = hist_ref[:, cs]
            hist_ref[:, cs] = u[T - SUBLANES:T, :]
            halves.append(w_conv_ref[0:1, cs] * delayed(u, prev, 2) + w_conv_ref[1:2, cs] * delayed(u, prev, 1)
                          + w_conv_ref[2:3, cs] * u + b_conv_ref[:, cs])
        f_ref[:, c * FFN_CHUNK:(c + 1) * FFN_CHUNK] = (_gelu_tanh(halves[0]) * halves[1]).astype(BF16)

    y_ref[...] = _ffn_tail(x, f_ref, p_ref[...], w_down_ref, g_post_ref, g_ple_ref, w_ple_ref, w_pg_ref)

    @pl.when(s_idx == last)
    def _():
        tail_out_ref[...] = hist_ref[SUBLANES - FFN_TAIL:SUBLANES, :]


def _ffn_prompt(x, p, wts, layer):
    B, S, D = x.shape
    T = PROMPT_TILE
    weights = (wts["g_ffn_pre"], wts["w_up"], wts["w_conv_ffn"], wts["b_conv_ffn"], wts["w_down"],
               wts["g_ffn_post"], wts["g_ple"], wts["w_ple"], wts["w_ple_gate"])
    return pl.pallas_call(
        _ffn_prompt_kernel,
        grid=(B, S // T),
        in_specs=[pl.BlockSpec((None, T, D), lambda b, s: (b, s, 0)),
                  pl.BlockSpec((None, None, T, PLE_DIM), lambda b, s: (layer, b, s, 0))]
                 + [_full(w.shape) for w in weights],
        out_specs=(pl.BlockSpec((None, T, D), lambda b, s: (b, s, 0)),
                   pl.BlockSpec((None, FFN_TAIL, 2 * D_FF), lambda b, s: (b, 0, 0))),
        out_shape=(jax.ShapeDtypeStruct((B, S, D), F32),
                   jax.ShapeDtypeStruct((B, FFN_TAIL, 2 * D_FF), F32)),
        scratch_shapes=[
            pltpu.VMEM((SUBLANES, 2 * D_FF), F32),
            pltpu.VMEM((T, D_FF), BF16),
        ],
        compiler_params=pltpu.CompilerParams(
            dimension_semantics=("arbitrary", "arbitrary"), vmem_limit_bytes=VMEM_LIMIT),
        name="ffn_prompt",
    )(x, p, *weights)


def _mixer_sample_kernel(x_ref, st_ref, c_ref, n_ref, mrow_ref,
                         g_pre_ref, w_in_ref, w_gate_ref, b_gate_ref, w_conv_ref, b_conv_ref,
                         g_cn_ref, b_cn_ref, g_mn_ref, w_out_ref, g_post_ref,
                         y_ref, conv_out_ref, c_out_ref, n_out_ref, m_out_ref,
                         ext_ref, q_ref, kt_ref, wv_ref, wk_ref, carried_ref, qn_ref, dec_ref, *, seq_len):
    NB = SAMPLE_SEQS
    R = NB * SEQ_PAD
    x = x_ref[...]
    h = _rms(x, g_pre_ref[...]).astype(BF16)

    zv = _dot(h, w_in_ref[:, 0:CONV_CH])
    zg = _dot(h, w_in_ref[:, CONV_CH:2 * CONV_CH])
    a = zv * _sigmoid(zg)
    ext_ref[:, 0:CONV_TAIL, :] = st_ref[...]
    ext_ref[:, CONV_TAIL:CONV_TAIL + SEQ_PAD, :] = a.reshape(NB, SEQ_PAD, CONV_CH)
    ext_ref[:, CONV_TAIL + SEQ_PAD:, :] = jnp.zeros((NB, SAMPLE_EXT_ROWS - CONV_TAIL - SEQ_PAD, CONV_CH), F32)
    acc = jnp.broadcast_to(b_conv_ref[...][None], (NB, SEQ_PAD, CONV_CH))
    for j in range(CONV_WIDTH):
        acc = acc + w_conv_ref[j:j + 1, :][None] * ext_ref[:, pl.ds(j, SEQ_PAD), :]
    conv_out_ref[...] = ext_ref[:, seq_len:seq_len + CONV_TAIL, :]
    mix_parts = _conv_branch_post(acc.reshape(R, CONV_CH), g_cn_ref, b_cn_ref)

    gates = _dot(h, w_gate_ref[...]) + b_gate_ref[...]
    row = lax.broadcasted_iota(jnp.int32, (R, R), 0)
    col = lax.broadcasted_iota(jnp.int32, (R, R), 1)
    same_seq = (row // SEQ_PAD) == (col // SEQ_PAD)
    causal = same_seq & (col <= row)
    bcum = _dot_exact(causal, _log_sigmoid(gates))
    mask = causal & ((col % SEQ_PAD) < seq_len)
    pick_last = same_seq & ((col % SEQ_PAD) == seq_len - 1)
    inter_all = bcum + mrow_ref[...]
    gates_t = gates.T
    bcum_t = bcum.T
    lane = lax.broadcasted_iota(jnp.int32, (R, LANES), 1)
    row_valid = (lax.broadcasted_iota(jnp.int32, (R, 1), 0) % SEQ_PAD) < seq_len
    stats = jnp.where((lane >= HEADS) & (lane < 2 * HEADS), bcum, 0.0)
    q_off = 2 * CONV_CH
    saved = []
    for hd in range(HEADS):
        c0 = hd * HEAD_DIM
        zq = _dot(h, w_in_ref[:, q_off + c0:q_off + c0 + HEAD_DIM]) * (HEAD_DIM ** -0.5)
        zk = _dot(h, w_in_ref[:, q_off + MLSTM_WIDTH + c0:q_off + MLSTM_WIDTH + c0 + HEAD_DIM])
        zvv = _dot(h, w_in_ref[:, q_off + 2 * MLSTM_WIDTH + c0:q_off + 2 * MLSTM_WIDTH + c0 + HEAD_DIM])
        zo = _dot(h, w_in_ref[:, q_off + 3 * MLSTM_WIDTH + c0:q_off + 3 * MLSTM_WIDTH + c0 + HEAD_DIM])
        i_row = gates_t[hd:hd + 1, :]
        b_row = bcum_t[HEADS + hd:HEADS + hd + 1, :]
        b_col = bcum[:, HEADS + hd:HEADS + hd + 1]
        d = jnp.where(mask, b_col - b_row + i_row, NEG)
        inter = inter_all[:, HEADS + hd:HEADS + hd + 1]
        m_t, w_inter, num, den = _mlstm_intra(zq.astype(BF16), zk.astype(BF16), zvv.astype(BF16), d, inter)
        stats = jnp.where(lane == hd, m_t, stats)
        stats = jnp.where(lane == 2 * HEADS + hd, w_inter, stats)
        q_ref[hd, 0:R, :] = zq
        q_ref[hd, R:R + SEQ_PAD, :] = jnp.zeros((SEQ_PAD, HEAD_DIM), F32)
        kt_ref[hd] = zk.T
        saved.append((m_t, w_inter, num, den, zo, zk, zvv))

    per_seq = _dot_exact(pick_last, stats)
    m_out_ref[...] = per_seq
    for hd in range(HEADS):
        zk, zvv = saved[hd][5], saved[hd][6]
        m_new = per_seq[:, hd:hd + 1]
        b_last = per_seq[:, HEADS + hd:HEADS + hd + 1]
        decay = per_seq[:, 2 * HEADS + hd:2 * HEADS + hd + 1]
        b_col = bcum[:, HEADS + hd:HEADS + hd + 1]
        i_col = gates[:, hd:hd + 1]
        ws = jnp.where(row_valid, jnp.exp(b_last - b_col + i_col - m_new), 0.0)
        wv_ref[hd] = (ws * zvv).astype(BF16)
        wk_ref[hd] = ws * zk
        dec_ref[hd] = jnp.broadcast_to(decay, (R, LANES))

    col_seq = lax.broadcasted_iota(jnp.int32, (HEAD_DIM, R), 1) // SEQ_PAD

    def per_sequence(b, carry):
        r0 = pl.multiple_of(b * SEQ_PAD, SEQ_PAD)
        for hd in range(HEADS):
            c_old = c_ref[b, hd]
            n_old = n_ref[b, hd:hd + 1, :]
            q2 = q_ref[hd, pl.ds(r0, 2 * SEQ_PAD), :]
            carried_ref[hd, pl.ds(r0, SEQ_PAD), :] = _dot(q2.astype(BF16), c_old.astype(BF16))[0:SEQ_PAD, :]
            qn = jnp.sum(q2[0:SEQ_PAD, :] * n_old, axis=1, keepdims=True)
            qn_ref[hd, pl.ds(r0, SEQ_PAD), :] = jnp.broadcast_to(qn, (SEQ_PAD, LANES))
            dec = dec_ref[hd, pl.ds(r0, 1), :]
            kt_b = jnp.where(col_seq == b, kt_ref[hd], 0.0).astype(BF16)
            c_out_ref[b, hd] = dec * c_old + _dot(kt_b, wv_ref[hd])
            n_out_ref[b, hd:hd + 1, :] = dec * n_old + jnp.sum(wk_ref[hd, pl.ds(r0, SEQ_PAD), :], axis=0,
                                                                keepdims=True)
        return carry

    lax.fori_loop(0, NB, per_sequence, 0)

    for hd in range(HEADS):
        c0 = hd * HEAD_DIM
        m_t, w_inter, num, den, zo = saved[hd][:5]
        num = num + carried_ref[hd] * w_inter
        den = den + qn_ref[hd][:, 0:1] * w_inter
        mix_parts.append(_head_out(num, den, m_t, g_mn_ref[:, c0:c0 + HEAD_DIM], zo))

    mix = jnp.concatenate(mix_parts, axis=1).astype(BF16)
    y_ref[...] = x + _rms(_dot(mix, w_out_ref[...]), g_post_ref[...])


def _mixer_sample(x, st, c, n, mrow, wts, seq_len, layer):
    NB = SAMPLE_SEQS
    R = NB * SEQ_PAD
    nseq = c.shape[1]
    weights = (wts["g_mix_pre"], wts["w_in"], wts["w_gate"], wts["b_gate"], wts["w_conv_mix"],
               wts["b_conv_mix"], wts["g_conv_norm"], wts["b_conv_norm"], wts["g_mlstm_norm"],
               wts["w_out"], wts["g_mix_post"])
    rows = lambda width: pl.BlockSpec((R, width), lambda i: (i, 0))
    st_spec = pl.BlockSpec((NB, CONV_TAIL, CONV_CH), lambda i: (i, 0, 0))
    c_spec = pl.BlockSpec((NB, HEADS, HEAD_DIM, HEAD_DIM), lambda i: (i, 0, 0, 0))
    n_spec = pl.BlockSpec((NB, HEADS, HEAD_DIM), lambda i: (i, 0, 0))
    st_in = pl.BlockSpec((None, NB, CONV_TAIL, CONV_CH), lambda i: (layer, i, 0, 0))
    c_in = pl.BlockSpec((None, NB, HEADS, HEAD_DIM, HEAD_DIM), lambda i: (layer, i, 0, 0, 0))
    n_in = pl.BlockSpec((None, NB, HEADS, HEAD_DIM), lambda i: (layer, i, 0, 0))
    return pl.pallas_call(
        functools.partial(_mixer_sample_kernel, seq_len=seq_len),
        grid=(nseq // NB,),
        in_specs=[rows(D_MODEL), st_in, c_in, n_in, rows(LANES)] + [_full(w.shape) for w in weights],
        out_specs=(rows(D_MODEL), st_spec, c_spec, n_spec, rows(LANES)),
        out_shape=(jax.ShapeDtypeStruct(x.shape, F32), jax.ShapeDtypeStruct(st.shape[1:], F32),
                   jax.ShapeDtypeStruct(c.shape[1:], F32), jax.ShapeDtypeStruct(n.shape[1:], F32),
                   jax.ShapeDtypeStruct(mrow.shape, F32)),
        scratch_shapes=[
            pltpu.VMEM((NB, SAMPLE_EXT_ROWS, CONV_CH), F32),
            pltpu.VMEM((HEADS, R + SEQ_PAD, HEAD_DIM), F32),
            pltpu.VMEM((HEADS, HEAD_DIM, R), F32),
            pltpu.VMEM((HEADS, R, HEAD_DIM), BF16),
            pltpu.VMEM((HEADS, R, HEAD_DIM), F32),
            pltpu.VMEM((HEADS, R, HEAD_DIM), F32),
            pltpu.VMEM((HEADS, R, LANES), F32),
            pltpu.VMEM((HEADS, R, LANES), F32),
        ],
        compiler_params=pltpu.CompilerParams(
            dimension_semantics=("arbitrary",), vmem_limit_bytes=VMEM_LIMIT),
        name="mixer_sample",
    )(x, st, c, n, mrow, *weights)


def _ffn_sample_kernel(x_ref, p_ref, st_ref, g_pre_ref, w_up_ref, w_conv_ref, b_conv_ref, w_down_ref,
                       g_post_ref, g_ple_ref, w_ple_ref, w_pg_ref,
                       y_ref, tail_out_ref,
                       ubuf_ref, f_ref, *, seq_len):
    NB = SAMPLE_SEQS_FFN
    R = NB * SEQ_PAD
    x = x_ref[...]
    h = _rms(x, g_pre_ref[...]).astype(BF16)
    lo = SEQ_PAD - FFN_TAIL
    for c in range(D_FF // FFN_CHUNK):
        halves = []
        for half in range(2):
            c0 = half * D_FF + c * FFN_CHUNK
            cs = slice(c0, c0 + FFN_CHUNK)
            u = _dot(h, w_up_ref[:, cs])
            ubuf_ref[half, :, lo:SEQ_PAD, :] = st_ref[:, :, cs]
            ubuf_ref[half, :, SEQ_PAD:2 * SEQ_PAD, :] = u.reshape(NB, SEQ_PAD, FFN_CHUNK)
            tail_out_ref[:, :, cs] = ubuf_ref[half, :, lo + seq_len:SEQ_PAD + seq_len, :]
            y = (w_conv_ref[0:1, cs][None] * ubuf_ref[half, :, pl.ds(lo, SEQ_PAD), :]
                 + w_conv_ref[1:2, cs][None] * ubuf_ref[half, :, pl.ds(lo + 1, SEQ_PAD), :]
                 + w_conv_ref[2:3, cs][None] * ubuf_ref[half, :, pl.ds(lo + 2, SEQ_PAD), :]
                 + b_conv_ref[:, cs][None])
            halves.append(y.reshape(R, FFN_CHUNK))
        f_ref[:, c * FFN_CHUNK:(c + 1) * FFN_CHUNK] = (_gelu_tanh(halves[0]) * halves[1]).astype(BF16)
    y_ref[...] = _ffn_tail(x, f_ref, p_ref[...], w_down_ref, g_post_ref, g_ple_ref, w_ple_ref, w_pg_ref)


def _ffn_sample(x, p, st, wts, seq_len, layer):
    NB = SAMPLE_SEQS_FFN
    R = NB * SEQ_PAD
    nseq = st.shape[1]
    weights = (wts["g_ffn_pre"], wts["w_up"], wts["w_conv_ffn"], wts["b_conv_ffn"], wts["w_down"],
               wts["g_ffn_post"], wts["g_ple"], wts["w_ple"], wts["w_ple_gate"])
    st_spec = pl.BlockSpec((NB, FFN_TAIL, 2 * D_FF), lambda i: (i, 0, 0))
    st_in = pl.BlockSpec((None, NB, FFN_TAIL, 2 * D_FF), lambda i: (layer, i, 0, 0))
    return pl.pallas_call(
        functools.partial(_ffn_sample_kernel, seq_len=seq_len),
        grid=(nseq // NB,),
        in_specs=[pl.BlockSpec((R, D_MODEL), lambda i: (i, 0)), pl.BlockSpec((R, PLE_DIM), lambda i: (i, 0)),
                  st_in] + [_full(w.shape) for w in weights],
        out_specs=(pl.BlockSpec((R, D_MODEL), lambda i: (i, 0)), st_spec),
        out_shape=(jax.ShapeDtypeStruct(x.shape, F32), jax.ShapeDtypeStruct(st.shape[1:], F32)),
        scratch_shapes=[
            pltpu.VMEM((2, NB, 2 * SEQ_PAD, FFN_CHUNK), F32),
            pltpu.VMEM((R, D_FF), BF16),
        ],
        compiler_params=pltpu.CompilerParams(
            dimension_semantics=("arbitrary",), vmem_limit_bytes=VMEM_LIMIT),
        name="ffn_sample",
    )(x, p, st, *weights)


def _pad_seq(a):
    nseq, seq_len, width = a.shape
    return jnp.pad(a, ((0, 0), (0, SEQ_PAD - seq_len), (0, 0))).reshape(nseq * SEQ_PAD, width)


def _layer_weights(l, g_mix_pre, w_in, b_igate, b_fgate, w_conv_mix, b_conv_mix, g_conv_norm,
                   b_conv_norm, g_mlstm_norm, w_out, g_mix_post, g_ffn_pre, w_up, w_conv_ffn,
                   b_conv_ffn, w_down, g_ffn_post, g_ple, w_ple, w_ple_gate):
    row = lambda v: v[l][None, :].astype(F32)
    n_gate = 2 * HEADS
    w_gate = jnp.pad(w_in[l][:, MAIN_COLS:], ((0, 0), (0, LANES - n_gate))).astype(BF16)
    b_gate = jnp.pad(jnp.concatenate([b_igate[l], b_fgate[l]]), (0, LANES - n_gate))[None, :].astype(F32)
    return {
        "g_mix_pre": row(g_mix_pre), "w_in": w_in[l][:, :MAIN_COLS].astype(BF16),
        "w_gate": w_gate, "b_gate": b_gate,
        "w_conv_mix": w_conv_mix[l].astype(F32), "b_conv_mix": row(b_conv_mix),
        "g_conv_norm": row(g_conv_norm), "b_conv_norm": row(b_conv_norm),
        "g_mlstm_norm": row(g_mlstm_norm), "w_out": w_out[l].astype(BF16), "g_mix_post": row(g_mix_post),
        "g_ffn_pre": row(g_ffn_pre), "w_up": w_up[l].astype(BF16), "w_conv_ffn": w_conv_ffn[l].astype(F32),
        "b_conv_ffn": row(b_conv_ffn), "w_down": w_down[l].astype(BF16), "g_ffn_post": row(g_ffn_post),
        "g_ple": row(g_ple), "w_ple": w_ple[l].astype(BF16), "w_ple_gate": w_ple_gate[l].astype(BF16),
    }


def kernel(x_prompt, x_sample, p_prompt, p_sample, state_conv_mix, state_mlstm_C, state_mlstm_n, state_mlstm_m, state_conv_ffn, g_mix_pre, w_in, b_igate, b_fgate, w_conv_mix, b_conv_mix, g_conv_norm, b_conv_norm, g_mlstm_norm, w_out, g_mix_post, g_ffn_pre, w_up, w_conv_ffn, b_conv_ffn, w_down, g_ffn_post, g_ple, w_ple, w_ple_gate):
    depth = w_in.shape[0]
    nseq, seq_len, _ = x_sample.shape
    assert FFN_TAIL <= seq_len <= SEQ_PAD and nseq % SAMPLE_SEQS == 0 and nseq % SAMPLE_SEQS_FFN == 0
    assert x_prompt.shape[1] % PROMPT_TILE == 0
    xp = x_prompt
    xs = _pad_seq(x_sample)
    pc, pC, pn, pm, pf = [], [], [], [], []
    sc, sC, sn, sm, sf = [], [], [], [], []
    for l in range(depth):
        wts = _layer_weights(l, g_mix_pre, w_in, b_igate, b_fgate, w_conv_mix, b_conv_mix, g_conv_norm,
                             b_conv_norm, g_mlstm_norm, w_out, g_mix_post, g_ffn_pre, w_up, w_conv_ffn,
                             b_conv_ffn, w_down, g_ffn_post, g_ple, w_ple, w_ple_gate)
        xp, c1, C1, n1, m1 = _mixer_prompt(xp, wts)
        xp, f1 = _ffn_prompt(xp, p_prompt, wts, l)
        pc.append(c1); pC.append(C1); pn.append(n1); pm.append(m1[:, :HEADS, 0]); pf.append(f1)

        mrow = jnp.pad(jnp.repeat(state_mlstm_m[l].astype(F32), SEQ_PAD, axis=0),
                       ((0, 0), (HEADS, LANES - 2 * HEADS)))
        xs, c2, C2, n2, m2 = _mixer_sample(xs, state_conv_mix, state_mlstm_C, state_mlstm_n, mrow,
                                           wts, seq_len, l)
        xs, f2 = _ffn_sample(xs, _pad_seq(p_sample[l]), state_conv_ffn, wts, seq_len, l)
        sc.append(c2); sC.append(C2); sn.append(n2); sf.append(f2)
        sm.append(m2.reshape(nseq, SEQ_PAD, LANES)[:, 0, :HEADS])
    ys = xs.reshape(nseq, SEQ_PAD, D_MODEL)[:, :seq_len]
    return (xp, ys, jnp.stack(pc), jnp.stack(pC), jnp.stack(pn), jnp.stack(pm), jnp.stack(pf),
            jnp.stack(sc), jnp.stack(sC), jnp.stack(sn), jnp.stack(sm), jnp.stack(sf))
```

```python
import functools

import jax
import jax.numpy as jnp
from jax import lax
from jax.experimental import pallas as pl
from jax.experimental.pallas import tpu as pltpu

F32 = jnp.float32
BF16 = jnp.bfloat16

D_MODEL = 1024
CONV_CH = 512
CONV_WIDTH = 31
CONV_TAIL = CONV_WIDTH - 1
CONV_GROUPS = 4
HEADS = 4
HEAD_DIM = 128
MLSTM_WIDTH = HEADS * HEAD_DIM
D_FF = 2816
FFN_TAIL = 2
PLE_DIM = 256
EPS = 1e-6
MAIN_COLS = 2 * CONV_CH + 4 * MLSTM_WIDTH
LANES = 128
SUBLANES = 8
NEG = -1e30

PROMPT_TILE = 256
FFN_TILE = 256
SAMPLE_SEQS = 16
SAMPLE_SEQS_FFN = 32
SEQ_PAD = SUBLANES
HIST = 32
SHIFT_ROWS = PROMPT_TILE + HIST - SUBLANES
CONV_ROWS = 64
CONV_IN_FLIGHT = 2
FFN_ROWS = 64
FFN_IN_FLIGHT = 3
FFN_CHUNK = 256
SAMPLE_EXT_ROWS = -(-(CONV_TAIL + SEQ_PAD) // SUBLANES) * SUBLANES
VMEM_LIMIT = 56 * 1024 * 1024


def _dot(a, b):
    return jnp.dot(a, b, preferred_element_type=F32)


def _dot_exact(sel, x):
    hi = x.astype(BF16)
    r1 = x - hi.astype(F32)
    mid = r1.astype(BF16)
    lo = (r1 - mid.astype(F32)).astype(BF16)
    y = _dot(jnp.where(sel, 1.0, 0.0).astype(BF16), jnp.concatenate([hi, mid, lo], axis=1))
    return y[:, 0:LANES] + y[:, LANES:2 * LANES] + y[:, 2 * LANES:3 * LANES]


def _rms(x, g):
    ms = jnp.mean(x * x, axis=-1, keepdims=True)
    return x * lax.rsqrt(ms + EPS) * g


def _layernorm(x):
    mu = jnp.mean(x, axis=-1, keepdims=True)
    xc = x - mu
    var = jnp.mean(xc * xc, axis=-1, keepdims=True)
    return xc * lax.rsqrt(var + EPS)


def _sigmoid(x):
    return 1.0 / (1.0 + jnp.exp(-x))


def _log_sigmoid(x):
    return jnp.minimum(x, 0.0) - jnp.log(1.0 + jnp.exp(-jnp.abs(x)))


def _exact_zero(v):
    bits = pltpu.bitcast(v, jnp.uint32)
    bits = lax.shift_right_logical(lax.shift_right_logical(bits, jnp.uint32(16)), jnp.uint32(16))
    return pltpu.bitcast(bits, F32)


def _gelu_tanh(x):
    return 0.5 * x * (1.0 + jnp.tanh(0.7978845608028654 * (x + 0.044715 * (x * x * x))))


def _conv_branch_post(acc, g_ref, b_ref):
    parts = []
    for g in range(CONV_GROUPS):
        sl = slice(g * LANES, (g + 1) * LANES)
        y = _layernorm(acc[:, sl]) * g_ref[:, sl] + b_ref[:, sl]
        parts.append(y * _sigmoid(y))
    return parts


def _qk(q_bf, k_bf):
    return lax.dot_general(q_bf, k_bf, (((1,), (1,)), ((), ())), preferred_element_type=F32)


def _mlstm_weighted(scores, v_bf, d, inter):
    m_t = jnp.maximum(inter, jnp.max(d, axis=1, keepdims=True))
    w_intra = jnp.exp(d - m_t)
    w_inter = jnp.exp(inter - m_t)
    s = scores * w_intra
    num = _dot(s.astype(BF16), v_bf)
    den = jnp.sum(s, axis=1, keepdims=True)
    return m_t, w_inter, num, den


def _mlstm_intra(q_bf, k_bf, v_bf, d, inter):
    return _mlstm_weighted(_qk(q_bf, k_bf), v_bf, d, inter)


def _head_out(num, den, m_t, g_mn, zo):
    hh = num / jnp.maximum(jnp.abs(den), jnp.exp(-m_t))
    return _layernorm(hh) * g_mn * _sigmoid(zo)


def _mixer_prompt_kernel(x_ref, g_pre_ref, w_in_ref, w_gate_ref, b_gate_ref, w_conv_ref, b_conv_ref,
                         g_cn_ref, b_cn_ref, g_mn_ref, w_out_ref, g_post_ref,
                         y_ref, conv_out_ref, c_out_ref, n_out_ref, m_out_ref,
                         ext_ref, sh_ref, mix_ref, cn_ref, m_ref, h_ref, z_ref):
    T = PROMPT_TILE
    s_idx = pl.program_id(1)
    last = pl.num_programs(1) - 1

    @pl.when(s_idx == 0)
    def _():
        ext_ref[0:HIST, :] = jnp.zeros((HIST, CONV_CH), F32)
        cn_ref[...] = jnp.zeros(cn_ref.shape, F32)
        m_ref[...] = jnp.zeros(m_ref.shape, F32)

    x = x_ref[...]
    h_ref[...] = _rms(x, g_pre_ref[...]).astype(BF16)

    zv = _dot(h_ref[...], w_in_ref[:, 0:CONV_CH])
    zg = _dot(h_ref[...], w_in_ref[:, CONV_CH:2 * CONV_CH])
    gates = _dot(h_ref[...], w_gate_ref[...]) + b_gate_ref[...]
    q_off = 2 * CONV_CH
    slot = lax.rem(s_idx, 2)
    for part in range(4):
        ps = slice(part * MLSTM_WIDTH, (part + 1) * MLSTM_WIDTH)
        z_ref[slot, :, ps] = _dot(h_ref[...],
                                  w_in_ref[:, q_off + part * MLSTM_WIDTH:q_off + (part + 1) * MLSTM_WIDTH])
    row = lax.broadcasted_iota(jnp.int32, (T, T), 0)
    col = lax.broadcasted_iota(jnp.int32, (T, T), 1)
    causal = col <= row
    bcum = _dot_exact(causal, _log_sigmoid(gates))
    gates_t = gates.T
    bcum_t = bcum.T

    ext_ref[HIST:HIST + T, :] = zv * _sigmoid(zg)
    for r in range(1, SUBLANES):
        sh_ref[r - 1] = ext_ref[pl.ds(r, SHIFT_ROWS), :]
    recent = []
    for g in range(CONV_GROUPS):
        cs = slice(g * LANES, (g + 1) * LANES)
        for rb in range(T // CONV_ROWS):
            acc = jnp.broadcast_to(b_conv_ref[:, cs], (CONV_ROWS, LANES))
            for j in range(CONV_WIDTH):
                off = HIST - CONV_TAIL + j
                r, base = off % SUBLANES, rb * CONV_ROWS + off - off % SUBLANES
                src = ext_ref if r == 0 else sh_ref.at[r - 1]
                w_row = w_conv_ref[j:j + 1, cs]
                if j == 0 and len(recent) == CONV_IN_FLIGHT:
                    w_row = w_row + _exact_zero(recent.pop(0))
                acc = acc + w_row * src[base:base + CONV_ROWS, cs]
            y = _layernorm(acc) * g_cn_ref[:, cs] + b_cn_ref[:, cs]
            mix_ref[rb * CONV_ROWS:(rb + 1) * CONV_ROWS, cs] = (y * _sigmoid(y)).astype(BF16)
            recent.append(acc[0:1, :])
    ext_ref[0:HIST, :] = ext_ref[T:T + HIST, :]

    heads = []
    for hd in range(HEADS):
        c0 = hd * HEAD_DIM
        zvv = z_ref[slot, :, 2 * MLSTM_WIDTH + c0:2 * MLSTM_WIDTH + c0 + HEAD_DIM]
        q_bf = (z_ref[slot, :, c0:c0 + HEAD_DIM] * (HEAD_DIM ** -0.5)).astype(BF16)
        k_bf = z_ref[slot, :, MLSTM_WIDTH + c0:MLSTM_WIDTH + c0 + HEAD_DIM].astype(BF16)
        cn = cn_ref[hd]
        scores = _qk(q_bf, k_bf)
        carried = _dot(q_bf, cn.astype(BF16))
        heads.append((zvv, k_bf, cn, scores, carried))

    finished = []
    for hd in range(HEADS):
        zvv, k_bf, cn, scores, carried = heads[hd]
        i_row = gates_t[hd:hd + 1, :]
        i_col = gates[:, hd:hd + 1]
        b_row = bcum_t[HEADS + hd:HEADS + hd + 1, :]
        b_col = bcum[:, HEADS + hd:HEADS + hd + 1]
        m_prev = m_ref[hd:hd + 1, 0:1]
        d = jnp.where(causal, b_col - b_row + i_row, NEG)
        inter = b_col + m_prev
        m_t, w_inter, num, den = _mlstm_weighted(scores, zvv.astype(BF16), d, inter)
        b_last = b_col[T - 1:T, :]
        m_new = m_t[T - 1:T, :]
        decay = w_inter[T - 1:T, :]
        ws = jnp.exp(b_last - b_col + i_col - m_new)
        vp = jnp.concatenate([ws * zvv, jnp.broadcast_to(ws, (T, HEAD_DIM))], axis=1).astype(BF16)
        kv = lax.dot_general(k_bf, vp, (((0,), (0,)), ((), ())), preferred_element_type=F32)
        finished.append((m_t, w_inter, num, den, m_new, decay, kv))

    for hd in range(HEADS):
        c0 = hd * HEAD_DIM
        _, _, cn, _, carried = heads[hd]
        m_t, w_inter, num, den, m_new, decay, kv = finished[hd]
        zo = z_ref[slot, :, 3 * MLSTM_WIDTH + c0:3 * MLSTM_WIDTH + c0 + HEAD_DIM]
        num = num + carried[:, 0:HEAD_DIM] * w_inter
        den = den + carried[:, HEAD_DIM:HEAD_DIM + 1] * w_inter
        mix_ref[:, CONV_CH + c0:CONV_CH + c0 + HEAD_DIM] = _head_out(
            num, den, m_t, g_mn_ref[:, c0:c0 + HEAD_DIM], zo).astype(BF16)
        cn_ref[hd] = decay * cn + kv
        m_ref[hd:hd + 1, :] = jnp.broadcast_to(m_new, (1, LANES))

    y_ref[...] = x + _rms(_dot(mix_ref[...], w_out_ref[...]), g_post_ref[...])

    @pl.when(s_idx == last)
    def _():
        conv_out_ref[...] = ext_ref[pl.ds(HIST - CONV_TAIL, CONV_TAIL), :]
        m_out_ref[...] = m_ref[...]
        for hd in range(HEADS):
            cn = cn_ref[hd]
            c_out_ref[hd] = cn[:, 0:HEAD_DIM]
            n_out_ref[hd:hd + 1, :] = cn[:, HEAD_DIM:].T[0:1, :]


def _full(shape):
    n = len(shape)
    return pl.BlockSpec(shape, lambda *_: (0,) * n, pipeline_mode=pl.Buffered(1))


def _mixer_prompt(x, wts):
    B, S, D = x.shape
    T = PROMPT_TILE
    weights = (wts["g_mix_pre"], wts["w_in"], wts["w_gate"], wts["b_gate"], wts["w_conv_mix"],
               wts["b_conv_mix"], wts["g_conv_norm"], wts["b_conv_norm"], wts["g_mlstm_norm"],
               wts["w_out"], wts["g_mix_post"])
    out_shape = (
        jax.ShapeDtypeStruct((B, S, D), F32),
        jax.ShapeDtypeStruct((B, CONV_TAIL, CONV_CH), F32),
        jax.ShapeDtypeStruct((B, HEADS, HEAD_DIM, HEAD_DIM), F32),
        jax.ShapeDtypeStruct((B, HEADS, HEAD_DIM), F32),
        jax.ShapeDtypeStruct((B, SUBLANES, LANES), F32),
    )
    out_specs = (
        pl.BlockSpec((None, T, D), lambda b, s: (b, s, 0)),
        pl.BlockSpec((None, CONV_TAIL, CONV_CH), lambda b, s: (b, 0, 0)),
        pl.BlockSpec((None, HEADS, HEAD_DIM, HEAD_DIM), lambda b, s: (b, 0, 0, 0)),
        pl.BlockSpec((None, HEADS, HEAD_DIM), lambda b, s: (b, 0, 0)),
        pl.BlockSpec((None, SUBLANES, LANES), lambda b, s: (b, 0, 0)),
    )
    return pl.pallas_call(
        _mixer_prompt_kernel,
        grid=(B, S // T),
        in_specs=[pl.BlockSpec((None, T, D), lambda b, s: (b, s, 0))] + [_full(w.shape) for w in weights],
        out_specs=out_specs,
        out_shape=out_shape,
        scratch_shapes=[
            pltpu.VMEM((HIST + T, CONV_CH), F32),
            pltpu.VMEM((SUBLANES - 1, SHIFT_ROWS, CONV_CH), F32),
            pltpu.VMEM((T, D_MODEL), BF16),
            pltpu.VMEM((HEADS, HEAD_DIM, 2 * HEAD_DIM), F32),
            pltpu.VMEM((SUBLANES, LANES), F32),
            pltpu.VMEM((T, D_MODEL), BF16),
            pltpu.VMEM((2, T, 4 * MLSTM_WIDTH), F32),
        ],
        compiler_params=pltpu.CompilerParams(
            dimension_semantics=("arbitrary", "arbitrary"), vmem_limit_bytes=VMEM_LIMIT),
        name="mixer_prompt",
    )(x, *weights)


def _ffn_tail(x, f_ref, p, w_down_ref, g_post_ref, g_ple_ref, w_ple_ref, w_pg_ref):
    x2 = x + _rms(_dot(f_ref[...], w_down_ref[...]), g_post_ref[...])
    emb = _dot(p.astype(BF16), w_ple_ref[...])
    gate = _sigmoid(_dot(_rms(x2, g_ple_ref[...]).astype(BF16), w_pg_ref[...]))
    return x2 + emb * gate


def _ffn_prompt_kernel(x_ref, p_ref, g_pre_ref, w_up_ref, w_conv_ref, b_conv_ref, w_down_ref,
                       g_post_ref, g_ple_ref, w_ple_ref, w_pg_ref,
                       y_ref, tail_out_ref,
                       hist_ref, f_ref):
    T = FFN_TILE
    G = T // SUBLANES
    s_idx = pl.program_id(1)
    last = pl.num_programs(1) - 1

    @pl.when(s_idx == 0)
    def _():
        hist_ref[...] = jnp.zeros(hist_ref.shape, F32)

    x = x_ref[...]
    h = _rms(x, g_pre_ref[...]).astype(BF16)
    sub = lax.broadcasted_iota(jnp.int32, (T, FFN_CHUNK), 0) % SUBLANES

    def delayed(u, prev, k):
        rot = jnp.concatenate([pltpu.roll(prev, k, 0)]
                              + [pltpu.roll(u[g * SUBLANES:(g + 1) * SUBLANES, :], k, 0) for g in range(G)], axis=0)
        return jnp.where(sub < k, rot[0:T, :], rot[SUBLANES:SUBLANES + T, :])

    for c in range(D_FF // FFN_CHUNK):
        halves = []
        for half in range(2):
            c0 = half * D_FF + c * FFN_CHUNK
            cs = slice(c0, c0 + FFN_CHUNK)
            u = _dot(h, w_up_ref[:, cs])
            prev = hist_ref[:, cs]
            hist_ref[:, cs] = u[T - SUBLANES:T, :]
            halves.append(w_conv_ref[0:1, cs] * delayed(u, prev, 2) + w_conv_ref[1:2, cs] * delayed(u, prev, 1)
                          + w_conv_ref[2:3, cs] * u + b_conv_ref[:, cs])
        f_ref[:, c * FFN_CHUNK:(c + 1) * FFN_CHUNK] = (_gelu_tanh(halves[0]) * halves[1]).astype(BF16)

    y_ref[...] = _ffn_tail(x, f_ref, p_ref[...], w_down_ref, g_post_ref, g_ple_ref, w_ple_ref, w_pg_ref)

    @pl.when(s_idx == last)
    def _():
        tail_out_ref[...] = hist_ref[SUBLANES - FFN_TAIL:SUBLANES, :]


def _ffn_prompt(x, p, wts, layer):
    B, S, D = x.shape
    T = FFN_TILE
    weights = (wts["g_ffn_pre"], wts["w_up"], wts["w_conv_ffn"], wts["b_conv_ffn"], wts["w_down"],
               wts["g_ffn_post"], wts["g_ple"], wts["w_ple"], wts["w_ple_gate"])
    return pl.pallas_call(
        _ffn_prompt_kernel,
        grid=(B, S // T),
        in_specs=[pl.BlockSpec((None, T, D), lambda b, s: (b, s, 0)),
                  pl.BlockSpec((None, None, T, PLE_DIM), lambda b, s: (layer, b, s, 0))]
                 + [_full(w.shape) for w in weights],
        out_specs=(pl.BlockSpec((None, T, D), lambda b, s: (b, s, 0)),
                   pl.BlockSpec((None, FFN_TAIL, 2 * D_FF), lambda b, s: (b, 0, 0))),
        out_shape=(jax.ShapeDtypeStruct((B, S, D), F32),
                   jax.ShapeDtypeStruct((B, FFN_TAIL, 2 * D_FF), F32)),
        scratch_shapes=[
            pltpu.VMEM((SUBLANES, 2 * D_FF), F32),
            pltpu.VMEM((T, D_FF), BF16),
        ],
        compiler_params=pltpu.CompilerParams(
            dimension_semantics=("arbitrary", "arbitrary"), vmem_limit_bytes=VMEM_LIMIT),
        name="ffn_prompt",
    )(x, p, *weights)


def _layer_prompt_kernel(x_ref, p_ref,
                         g_pre_ref, w_in_ref, w_gate_ref, b_gate_ref, w_cm_ref, b_cm_ref,
                         g_cn_ref, b_cn_ref, g_mn_ref, w_out_ref, g_post_ref,
                         g_fpre_ref, w_up_ref, w_cf_ref, b_cf_ref, w_down_ref, g_fpost_ref,
                         g_ple_ref, w_ple_ref, w_pg_ref,
                         y_ref, conv_out_ref, c_out_ref, n_out_ref, m_out_ref, tail_out_ref,
                         ext_ref, sh_ref, mix_ref, cn_ref, m_ref, h_ref, z_ref, x1_ref, hist_ref, ubuf_ref, f_ref,
                         *, tiles_per_seq):
    T = PROMPT_TILE
    i = pl.program_id(0)
    n_tiles = pl.num_programs(0) - 1
    s_mix = lax.rem(jnp.minimum(i, n_tiles - 1), tiles_per_seq)
    s_ffn = lax.rem(jnp.maximum(i - 1, 0), tiles_per_seq)
    slot = lax.rem(i, 2)

    @pl.when(i == 0)
    def _():
        x1_ref[...] = jnp.zeros(x1_ref.shape, F32)

    @pl.when(s_mix == 0)
    def _():
        ext_ref[0:HIST, :] = jnp.zeros((HIST, CONV_CH), F32)
        cn_ref[...] = jnp.zeros(cn_ref.shape, F32)
        m_ref[...] = jnp.zeros(m_ref.shape, F32)

    @pl.when(s_ffn == 0)
    def _():
        hist_ref[...] = jnp.zeros(hist_ref.shape, F32)

    x = x_ref[...]
    h_ref[...] = _rms(x, g_pre_ref[...]).astype(BF16)
    zv = _dot(h_ref[...], w_in_ref[:, 0:CONV_CH])
    zg = _dot(h_ref[...], w_in_ref[:, CONV_CH:2 * CONV_CH])
    gates = _dot(h_ref[...], w_gate_ref[...]) + b_gate_ref[...]
    q_off = 2 * CONV_CH
    for part in range(4):
        ps = slice(part * MLSTM_WIDTH, (part + 1) * MLSTM_WIDTH)
        z_ref[slot, :, ps] = _dot(h_ref[...],
                                  w_in_ref[:, q_off + part * MLSTM_WIDTH:q_off + (part + 1) * MLSTM_WIDTH])

    ext_ref[HIST:HIST + T, :] = zv * _sigmoid(zg)
    for r in range(1, SUBLANES):
        sh_ref[r - 1] = ext_ref[pl.ds(r, SHIFT_ROWS), :]
    recent = []
    for g in range(CONV_GROUPS):
        cs = slice(g * LANES, (g + 1) * LANES)
        for rb in range(T // CONV_ROWS):
            acc = jnp.broadcast_to(b_cm_ref[:, cs], (CONV_ROWS, LANES))
            for j in range(CONV_WIDTH):
                off = HIST - CONV_TAIL + j
                r, base = off % SUBLANES, rb * CONV_ROWS + off - off % SUBLANES
                src = ext_ref if r == 0 else sh_ref.at[r - 1]
                w_row = w_cm_ref[j:j + 1, cs]
                if j == 0 and len(recent) == CONV_IN_FLIGHT:
                    w_row = w_row + _exact_zero(recent.pop(0))
                acc = acc + w_row * src[base:base + CONV_ROWS, cs]
            y = _layernorm(acc) * g_cn_ref[:, cs] + b_cn_ref[:, cs]
            mix_ref[rb * CONV_ROWS:(rb + 1) * CONV_ROWS, cs] = (y * _sigmoid(y)).astype(BF16)
            recent.append(acc[0:1, :])
    ext_ref[0:HIST, :] = ext_ref[T:T + HIST, :]

    x1 = x1_ref[1 - slot]
    h2 = _rms(x1, g_fpre_ref[...]).astype(BF16)
    for c in range(D_FF // FFN_CHUNK):
        halves = []
        for half in range(2):
            c0 = half * D_FF + c * FFN_CHUNK
            cs = slice(c0, c0 + FFN_CHUNK)
            u = _dot(h2, w_up_ref[:, cs])
            ubuf_ref[half, 0:SUBLANES, :] = hist_ref[:, cs]
            ubuf_ref[half, SUBLANES:SUBLANES + T, :] = u
            hist_ref[:, cs] = u[T - SUBLANES:T, :]
            halves.append(w_cf_ref[0:1, cs] * ubuf_ref[half, pl.ds(SUBLANES - 2, T), :]
                          + w_cf_ref[1:2, cs] * ubuf_ref[half, pl.ds(SUBLANES - 1, T), :]
                          + w_cf_ref[2:3, cs] * u + b_cf_ref[:, cs])
        f_ref[:, c * FFN_CHUNK:(c + 1) * FFN_CHUNK] = (_gelu_tanh(halves[0]) * halves[1]).astype(BF16)

    row = lax.broadcasted_iota(jnp.int32, (T, T), 0)
    col = lax.broadcasted_iota(jnp.int32, (T, T), 1)
    causal = col <= row
    bcum = _dot_exact(causal, _log_sigmoid(gates))
    gates_t = gates.T
    bcum_t = bcum.T
    heads = []
    for hd in range(HEADS):
        c0 = hd * HEAD_DIM
        zvv = z_ref[slot, :, 2 * MLSTM_WIDTH + c0:2 * MLSTM_WIDTH + c0 + HEAD_DIM]
        q_bf = (z_ref[slot, :, c0:c0 + HEAD_DIM] * (HEAD_DIM ** -0.5)).astype(BF16)
        k_bf = z_ref[slot, :, MLSTM_WIDTH + c0:MLSTM_WIDTH + c0 + HEAD_DIM].astype(BF16)
        cn = cn_ref[hd]
        heads.append((zvv, k_bf, cn, _qk(q_bf, k_bf), _dot(q_bf, cn.astype(BF16))))

    finished = []
    for hd in range(HEADS):
        zvv, k_bf, cn, scores, carried = heads[hd]
        i_row = gates_t[hd:hd + 1, :]
        i_col = gates[:, hd:hd + 1]
        b_row = bcum_t[HEADS + hd:HEADS + hd + 1, :]
        b_col = bcum[:, HEADS + hd:HEADS + hd + 1]
        m_prev = m_ref[hd:hd + 1, 0:1]
        d = jnp.where(causal, b_col - b_row + i_row, NEG)
        m_t, w_inter, num, den = _mlstm_weighted(scores, zvv.astype(BF16), d, b_col + m_prev)
        m_new = m_t[T - 1:T, :]
        ws = jnp.exp(b_col[T - 1:T, :] - b_col + i_col - m_new)
        vp = jnp.concatenate([ws * zvv, jnp.broadcast_to(ws, (T, HEAD_DIM))], axis=1).astype(BF16)
        kv = lax.dot_general(k_bf, vp, (((0,), (0,)), ((), ())), preferred_element_type=F32)
        finished.append((m_t, w_inter, num, den, m_new, w_inter[T - 1:T, :], kv))

    for hd in range(HEADS):
        c0 = hd * HEAD_DIM
        _, _, cn, _, carried = heads[hd]
        m_t, w_inter, num, den, m_new, decay, kv = finished[hd]
        zo = z_ref[slot, :, 3 * MLSTM_WIDTH + c0:3 * MLSTM_WIDTH + c0 + HEAD_DIM]
        num = num + carried[:, 0:HEAD_DIM] * w_inter
        den = den + carried[:, HEAD_DIM:HEAD_DIM + 1] * w_inter
        mix_ref[:, CONV_CH + c0:CONV_CH + c0 + HEAD_DIM] = _head_out(
            num, den, m_t, g_mn_ref[:, c0:c0 + HEAD_DIM], zo).astype(BF16)
        cn_ref[hd] = decay * cn + kv
        m_ref[hd:hd + 1, :] = jnp.broadcast_to(m_new, (1, LANES))

    y_ref[...] = _ffn_tail(x1, f_ref, p_ref[...], w_down_ref, g_fpost_ref, g_ple_ref, w_ple_ref, w_pg_ref)

    x1_ref[slot] = x + _rms(_dot(mix_ref[...], w_out_ref[...]), g_post_ref[...])

    @pl.when((s_mix == tiles_per_seq - 1) & (i < n_tiles))
    def _():
        conv_out_ref[...] = ext_ref[pl.ds(HIST - CONV_TAIL, CONV_TAIL), :]
        m_out_ref[...] = m_ref[...]
        for hd in range(HEADS):
            cn = cn_ref[hd]
            c_out_ref[hd] = cn[:, 0:HEAD_DIM]
            n_out_ref[hd:hd + 1, :] = cn[:, HEAD_DIM:].T[0:1, :]

    @pl.when((s_ffn == tiles_per_seq - 1) & (i >= 1))
    def _():
        tail_out_ref[...] = hist_ref[SUBLANES - FFN_TAIL:SUBLANES, :]


def _layer_prompt(x, p, wts, layer):
    B, S, D = x.shape
    T = PROMPT_TILE
    tps = S // T
    n_tiles = B * tps
    weights = (wts["g_mix_pre"], wts["w_in"], wts["w_gate"], wts["b_gate"], wts["w_conv_mix"],
               wts["b_conv_mix"], wts["g_conv_norm"], wts["b_conv_norm"], wts["g_mlstm_norm"],
               wts["w_out"], wts["g_mix_post"],
               wts["g_ffn_pre"], wts["w_up"], wts["w_conv_ffn"], wts["b_conv_ffn"], wts["w_down"],
               wts["g_ffn_post"], wts["g_ple"], wts["w_ple"], wts["w_ple_gate"])
    mix_tile = lambda i: jnp.minimum(i, n_tiles - 1)
    ffn_tile = lambda i: jnp.maximum(i - 1, 0)
    out_shape = (
        jax.ShapeDtypeStruct((B, S, D), F32),
        jax.ShapeDtypeStruct((B, CONV_TAIL, CONV_CH), F32),
        jax.ShapeDtypeStruct((B, HEADS, HEAD_DIM, HEAD_DIM), F32),
        jax.ShapeDtypeStruct((B, HEADS, HEAD_DIM), F32),
        jax.ShapeDtypeStruct((B, SUBLANES, LANES), F32),
        jax.ShapeDtypeStruct((B, FFN_TAIL, 2 * D_FF), F32),
    )
    out_specs = (
        pl.BlockSpec((None, T, D), lambda i: (ffn_tile(i) // tps, ffn_tile(i) % tps, 0)),
        pl.BlockSpec((None, CONV_TAIL, CONV_CH), lambda i: (mix_tile(i) // tps, 0, 0)),
        pl.BlockSpec((None, HEADS, HEAD_DIM, HEAD_DIM), lambda i: (mix_tile(i) // tps, 0, 0, 0)),
        pl.BlockSpec((None, HEADS, HEAD_DIM), lambda i: (mix_tile(i) // tps, 0, 0)),
        pl.BlockSpec((None, SUBLANES, LANES), lambda i: (mix_tile(i) // tps, 0, 0)),
        pl.BlockSpec((None, FFN_TAIL, 2 * D_FF), lambda i: (ffn_tile(i) // tps, 0, 0)),
    )
    return pl.pallas_call(
        functools.partial(_layer_prompt_kernel, tiles_per_seq=tps),
        grid=(n_tiles + 1,),
        in_specs=[pl.BlockSpec((None, T, D), lambda i: (mix_tile(i) // tps, mix_tile(i) % tps, 0)),
                  pl.BlockSpec((None, None, T, PLE_DIM),
                               lambda i: (layer, ffn_tile(i) // tps, ffn_tile(i) % tps, 0))]
                 + [_full(w.shape) for w in weights],
        out_specs=out_specs,
        out_shape=out_shape,
        scratch_shapes=[
            pltpu.VMEM((HIST + T, CONV_CH), F32),
            pltpu.VMEM((SUBLANES - 1, SHIFT_ROWS, CONV_CH), F32),
            pltpu.VMEM((T, D_MODEL), BF16),
            pltpu.VMEM((HEADS, HEAD_DIM, 2 * HEAD_DIM), F32),
            pltpu.VMEM((SUBLANES, LANES), F32),
            pltpu.VMEM((T, D_MODEL), BF16),
            pltpu.VMEM((2, T, 4 * MLSTM_WIDTH), F32),
            pltpu.VMEM((2, T, D_MODEL), F32),
            pltpu.VMEM((SUBLANES, 2 * D_FF), F32),
            pltpu.VMEM((2, SUBLANES + T, FFN_CHUNK), F32),
            pltpu.VMEM((T, D_FF), BF16),
        ],
        compiler_params=pltpu.CompilerParams(
            dimension_semantics=("arbitrary",), vmem_limit_bytes=VMEM_LIMIT),
        name="layer_prompt",
    )(x, p, *weights)


def _mixer_sample_kernel(x_ref, st_ref, c_ref, n_ref, mrow_ref,
                         g_pre_ref, w_in_ref, w_gate_ref, b_gate_ref, w_conv_ref, b_conv_ref,
                         g_cn_ref, b_cn_ref, g_mn_ref, w_out_ref, g_post_ref,
                         y_ref, conv_out_ref, c_out_ref, n_out_ref, m_out_ref,
                         ext_ref, q_ref, kt_ref, wv_ref, wk_ref, carried_ref, qn_ref, dec_ref, *, seq_len):
    NB = SAMPLE_SEQS
    R = NB * SEQ_PAD
    x = x_ref[...]
    h = _rms(x, g_pre_ref[...]).astype(BF16)

    zv = _dot(h, w_in_ref[:, 0:CONV_CH])
    zg = _dot(h, w_in_ref[:, CONV_CH:2 * CONV_CH])
    a = zv * _sigmoid(zg)
    ext_ref[:, 0:CONV_TAIL, :] = st_ref[...]
    ext_ref[:, CONV_TAIL:CONV_TAIL + SEQ_PAD, :] = a.reshape(NB, SEQ_PAD, CONV_CH)
    ext_ref[:, CONV_TAIL + SEQ_PAD:, :] = jnp.zeros((NB, SAMPLE_EXT_ROWS - CONV_TAIL - SEQ_PAD, CONV_CH), F32)
    acc = jnp.broadcast_to(b_conv_ref[...][None], (NB, SEQ_PAD, CONV_CH))
    for j in range(CONV_WIDTH):
        acc = acc + w_conv_ref[j:j + 1, :][None] * ext_ref[:, pl.ds(j, SEQ_PAD), :]
    conv_out_ref[...] = ext_ref[:, seq_len:seq_len + CONV_TAIL, :]
    mix_parts = _conv_branch_post(acc.reshape(R, CONV_CH), g_cn_ref, b_cn_ref)

    gates = _dot(h, w_gate_ref[...]) + b_gate_ref[...]
    row = lax.broadcasted_iota(jnp.int32, (R, R), 0)
    col = lax.broadcasted_iota(jnp.int32, (R, R), 1)
    same_seq = (row // SEQ_PAD) == (col // SEQ_PAD)
    causal = same_seq & (col <= row)
    bcum = _dot_exact(causal, _log_sigmoid(gates))
    mask = causal & ((col % SEQ_PAD) < seq_len)
    pick_last = same_seq & ((col % SEQ_PAD) == seq_len - 1)
    inter_all = bcum + mrow_ref[...]
    gates_t = gates.T
    bcum_t = bcum.T
    lane = lax.broadcasted_iota(jnp.int32, (R, LANES), 1)
    row_valid = (lax.broadcasted_iota(jnp.int32, (R, 1), 0) % SEQ_PAD) < seq_len
    stats = jnp.where((lane >= HEADS) & (lane < 2 * HEADS), bcum, 0.0)
    q_off = 2 * CONV_CH
    saved = []
    for hd in range(HEADS):
        c0 = hd * HEAD_DIM
        zq = _dot(h, w_in_ref[:, q_off + c0:q_off + c0 + HEAD_DIM]) * (HEAD_DIM ** -0.5)
        zk = _dot(h, w_in_ref[:, q_off + MLSTM_WIDTH + c0:q_off + MLSTM_WIDTH + c0 + HEAD_DIM])
        zvv = _dot(h, w_in_ref[:, q_off + 2 * MLSTM_WIDTH + c0:q_off + 2 * MLSTM_WIDTH + c0 + HEAD_DIM])
        zo = _dot(h, w_in_ref[:, q_off + 3 * MLSTM_WIDTH + c0:q_off + 3 * MLSTM_WIDTH + c0 + HEAD_DIM])
        i_row = gates_t[hd:hd + 1, :]
        b_row = bcum_t[HEADS + hd:HEADS + hd + 1, :]
        b_col = bcum[:, HEADS + hd:HEADS + hd + 1]
        d = jnp.where(mask, b_col - b_row + i_row, NEG)
        inter = inter_all[:, HEADS + hd:HEADS + hd + 1]
        m_t, w_inter, num, den = _mlstm_intra(zq.astype(BF16), zk.astype(BF16), zvv.astype(BF16), d, inter)
        stats = jnp.where(lane == hd, m_t, stats)
        stats = jnp.where(lane == 2 * HEADS + hd, w_inter, stats)
        q_ref[hd, 0:R, :] = zq
        q_ref[hd, R:R + SEQ_PAD, :] = jnp.zeros((SEQ_PAD, HEAD_DIM), F32)
        kt_ref[hd] = zk.T
        saved.append((m_t, w_inter, num, den, zo, zk, zvv))

    per_seq = _dot_exact(pick_last, stats)
    m_out_ref[...] = per_seq
    for hd in range(HEADS):
        zk, zvv = saved[hd][5], saved[hd][6]
        m_new = per_seq[:, hd:hd + 1]
        b_last = per_seq[:, HEADS + hd:HEADS + hd + 1]
        decay = per_seq[:, 2 * HEADS + hd:2 * HEADS + hd + 1]
        b_col = bcum[:, HEADS + hd:HEADS + hd + 1]
        i_col = gates[:, hd:hd + 1]
        ws = jnp.where(row_valid, jnp.exp(b_last - b_col + i_col - m_new), 0.0)
        wv_ref[hd] = (ws * zvv).astype(BF16)
        wk_ref[hd] = ws * zk
        dec_ref[hd] = jnp.broadcast_to(decay, (R, LANES))

    col_seq = lax.broadcasted_iota(jnp.int32, (HEAD_DIM, R), 1) // SEQ_PAD

    def per_sequence(b, carry):
        r0 = pl.multiple_of(b * SEQ_PAD, SEQ_PAD)
        for hd in range(HEADS):
            c_old = c_ref[b, hd]
            n_old = n_ref[b, hd:hd + 1, :]
            q2 = q_ref[hd, pl.ds(r0, 2 * SEQ_PAD), :]
            carried_ref[hd, pl.ds(r0, SEQ_PAD), :] = _dot(q2.astype(BF16), c_old.astype(BF16))[0:SEQ_PAD, :]
            qn = jnp.sum(q2[0:SEQ_PAD, :] * n_old, axis=1, keepdims=True)
            qn_ref[hd, pl.ds(r0, SEQ_PAD), :] = jnp.broadcast_to(qn, (SEQ_PAD, LANES))
            dec = dec_ref[hd, pl.ds(r0, 1), :]
            kt_b = jnp.where(col_seq == b, kt_ref[hd], 0.0).astype(BF16)
            c_out_ref[b, hd] = dec * c_old + _dot(kt_b, wv_ref[hd])
            n_out_ref[b, hd:hd + 1, :] = dec * n_old + jnp.sum(wk_ref[hd, pl.ds(r0, SEQ_PAD), :], axis=0,
                                                                keepdims=True)
        return carry

    lax.fori_loop(0, NB, per_sequence, 0)

    for hd in range(HEADS):
        c0 = hd * HEAD_DIM
        m_t, w_inter, num, den, zo = saved[hd][:5]
        num = num + carried_ref[hd] * w_inter
        den = den + qn_ref[hd][:, 0:1] * w_inter
        mix_parts.append(_head_out(num, den, m_t, g_mn_ref[:, c0:c0 + HEAD_DIM], zo))

    mix = jnp.concatenate(mix_parts, axis=1).astype(BF16)
    y_ref[...] = x + _rms(_dot(mix, w_out_ref[...]), g_post_ref[...])


def _mixer_sample(x, st, c, n, mrow, wts, seq_len, layer):
    NB = SAMPLE_SEQS
    R = NB * SEQ_PAD
    nseq = c.shape[1]
    weights = (wts["g_mix_pre"], wts["w_in"], wts["w_gate"], wts["b_gate"], wts["w_conv_mix"],
               wts["b_conv_mix"], wts["g_conv_norm"], wts["b_conv_norm"], wts["g_mlstm_norm"],
               wts["w_out"], wts["g_mix_post"])
    rows = lambda width: pl.BlockSpec((R, width), lambda i: (i, 0))
    st_spec = pl.BlockSpec((NB, CONV_TAIL, CONV_CH), lambda i: (i, 0, 0))
    c_spec = pl.BlockSpec((NB, HEADS, HEAD_DIM, HEAD_DIM), lambda i: (i, 0, 0, 0))
    n_spec = pl.BlockSpec((NB, HEADS, HEAD_DIM), lambda i: (i, 0, 0))
    st_in = pl.BlockSpec((None, NB, CONV_TAIL, CONV_CH), lambda i: (layer, i, 0, 0))
    c_in = pl.BlockSpec((None, NB, HEADS, HEAD_DIM, HEAD_DIM), lambda i: (layer, i, 0, 0, 0))
    n_in = pl.BlockSpec((None, NB, HEADS, HEAD_DIM), lambda i: (layer, i, 0, 0))
    return pl.pallas_call(
        functools.partial(_mixer_sample_kernel, seq_len=seq_len),
        grid=(nseq // NB,),
        in_specs=[rows(D_MODEL), st_in, c_in, n_in, rows(LANES)] + [_full(w.shape) for w in weights],
        out_specs=(rows(D_MODEL), st_spec, c_spec, n_spec, rows(LANES)),
        out_shape=(jax.ShapeDtypeStruct(x.shape, F32), jax.ShapeDtypeStruct(st.shape[1:], F32),
                   jax.ShapeDtypeStruct(c.shape[1:], F32), jax.ShapeDtypeStruct(n.shape[1:], F32),
                   jax.ShapeDtypeStruct(mrow.shape, F32)),
        scratch_shapes=[
            pltpu.VMEM((NB, SAMPLE_EXT_ROWS, CONV_CH), F32),
            pltpu.VMEM((HEADS, R + SEQ_PAD, HEAD_DIM), F32),
            pltpu.VMEM((HEADS, HEAD_DIM, R), F32),
            pltpu.VMEM((HEADS, R, HEAD_DIM), BF16),
            pltpu.VMEM((HEADS, R, HEAD_DIM), F32),
            pltpu.VMEM((HEADS, R, HEAD_DIM), F32),
            pltpu.VMEM((HEADS, R, LANES), F32),
            pltpu.VMEM((HEADS, R, LANES), F32),
        ],
        compiler_params=pltpu.CompilerParams(
            dimension_semantics=("arbitrary",), vmem_limit_bytes=VMEM_LIMIT),
        name="mixer_sample",
    )(x, st, c, n, mrow, *weights)


def _ffn_sample_kernel(x_ref, p_ref, st_ref, g_pre_ref, w_up_ref, w_conv_ref, b_conv_ref, w_down_ref,
                       g_post_ref, g_ple_ref, w_ple_ref, w_pg_ref,
                       y_ref, tail_out_ref,
                       ubuf_ref, f_ref, *, seq_len):
    NB = SAMPLE_SEQS_FFN
    R = NB * SEQ_PAD
    x = x_ref[...]
    h = _rms(x, g_pre_ref[...]).astype(BF16)
    lo = SEQ_PAD - FFN_TAIL
    for c in range(D_FF // FFN_CHUNK):
        halves = []
        for half in range(2):
            c0 = half * D_FF + c * FFN_CHUNK
            cs = slice(c0, c0 + FFN_CHUNK)
            u = _dot(h, w_up_ref[:, cs])
            ubuf_ref[half, :, lo:SEQ_PAD, :] = st_ref[:, :, cs]
            ubuf_ref[half, :, SEQ_PAD:2 * SEQ_PAD, :] = u.reshape(NB, SEQ_PAD, FFN_CHUNK)
            tail_out_ref[:, :, cs] = ubuf_ref[half, :, lo + seq_len:SEQ_PAD + seq_len, :]
            y = (w_conv_ref[0:1, cs][None] * ubuf_ref[half, :, pl.ds(lo, SEQ_PAD), :]
                 + w_conv_ref[1:2, cs][None] * ubuf_ref[half, :, pl.ds(lo + 1, SEQ_PAD), :]
                 + w_conv_ref[2:3, cs][None] * ubuf_ref[half, :, pl.ds(lo + 2, SEQ_PAD), :]
                 + b_conv_ref[:, cs][None])
            halves.append(y.reshape(R, FFN_CHUNK))
        f_ref[:, c * FFN_CHUNK:(c + 1) * FFN_CHUNK] = (_gelu_tanh(halves[0]) * halves[1]).astype(BF16)
    y_ref[...] = _ffn_tail(x, f_ref, p_ref[...], w_down_ref, g_post_ref, g_ple_ref, w_ple_ref, w_pg_ref)


def _ffn_sample(x, p, st, wts, seq_len, layer):
    NB = SAMPLE_SEQS_FFN
    R = NB * SEQ_PAD
    nseq = st.shape[1]
    weights = (wts["g_ffn_pre"], wts["w_up"], wts["w_conv_ffn"], wts["b_conv_ffn"], wts["w_down"],
               wts["g_ffn_post"], wts["g_ple"], wts["w_ple"], wts["w_ple_gate"])
    st_spec = pl.BlockSpec((NB, FFN_TAIL, 2 * D_FF), lambda i: (i, 0, 0))
    st_in = pl.BlockSpec((None, NB, FFN_TAIL, 2 * D_FF), lambda i: (layer, i, 0, 0))
    return pl.pallas_call(
        functools.partial(_ffn_sample_kernel, seq_len=seq_len),
        grid=(nseq // NB,),
        in_specs=[pl.BlockSpec((R, D_MODEL), lambda i: (i, 0)), pl.BlockSpec((R, PLE_DIM), lambda i: (i, 0)),
                  st_in] + [_full(w.shape) for w in weights],
        out_specs=(pl.BlockSpec((R, D_MODEL), lambda i: (i, 0)), st_spec),
        out_shape=(jax.ShapeDtypeStruct(x.shape, F32), jax.ShapeDtypeStruct(st.shape[1:], F32)),
        scratch_shapes=[
            pltpu.VMEM((2, NB, 2 * SEQ_PAD, FFN_CHUNK), F32),
            pltpu.VMEM((R, D_FF), BF16),
        ],
        compiler_params=pltpu.CompilerParams(
            dimension_semantics=("arbitrary",), vmem_limit_bytes=VMEM_LIMIT),
        name="ffn_sample",
    )(x, p, st, *weights)


def _pad_seq(a):
    nseq, seq_len, width = a.shape
    return jnp.pad(a, ((0, 0), (0, SEQ_PAD - seq_len), (0, 0))).reshape(nseq * SEQ_PAD, width)


def _layer_weights(l, g_mix_pre, w_in, b_igate, b_fgate, w_conv_mix, b_conv_mix, g_conv_norm,
                   b_conv_norm, g_mlstm_norm, w_out, g_mix_post, g_ffn_pre, w_up, w_conv_ffn,
                   b_conv_ffn, w_down, g_ffn_post, g_ple, w_ple, w_ple_gate):
    row = lambda v: v[l][None, :].astype(F32)
    n_gate = 2 * HEADS
    w_gate = jnp.pad(w_in[l][:, MAIN_COLS:], ((0, 0), (0, LANES - n_gate))).astype(BF16)
    b_gate = jnp.pad(jnp.concatenate([b_igate[l], b_fgate[l]]), (0, LANES - n_gate))[None, :].astype(F32)
    return {
        "g_mix_pre": row(g_mix_pre), "w_in": w_in[l][:, :MAIN_COLS].astype(BF16),
        "w_gate": w_gate, "b_gate": b_gate,
        "w_conv_mix": w_conv_mix[l].astype(F32), "b_conv_mix": row(b_conv_mix),
        "g_conv_norm": row(g_conv_norm), "b_conv_norm": row(b_conv_norm),
        "g_mlstm_norm": row(g_mlstm_norm), "w_out": w_out[l].astype(BF16), "g_mix_post": row(g_mix_post),
        "g_ffn_pre": row(g_ffn_pre), "w_up": w_up[l].astype(BF16), "w_conv_ffn": w_conv_ffn[l].astype(F32),
        "b_conv_ffn": row(b_conv_ffn), "w_down": w_down[l].astype(BF16), "g_ffn_post": row(g_ffn_post),
        "g_ple": row(g_ple), "w_ple": w_ple[l].astype(BF16), "w_ple_gate": w_ple_gate[l].astype(BF16),
    }


def kernel(x_prompt, x_sample, p_prompt, p_sample, state_conv_mix, state_mlstm_C, state_mlstm_n, state_mlstm_m, state_conv_ffn, g_mix_pre, w_in, b_igate, b_fgate, w_conv_mix, b_conv_mix, g_conv_norm, b_conv_norm, g_mlstm_norm, w_out, g_mix_post, g_ffn_pre, w_up, w_conv_ffn, b_conv_ffn, w_down, g_ffn_post, g_ple, w_ple, w_ple_gate):
    depth = w_in.shape[0]
    nseq, seq_len, _ = x_sample.shape
    assert FFN_TAIL <= seq_len <= SEQ_PAD and nseq % SAMPLE_SEQS == 0 and nseq % SAMPLE_SEQS_FFN == 0
    assert x_prompt.shape[1] % PROMPT_TILE == 0
    xp = x_prompt
    xs = _pad_seq(x_sample)
    pc, pC, pn, pm, pf = [], [], [], [], []
    sc, sC, sn, sm, sf = [], [], [], [], []
    for l in range(depth):
        wts = _layer_weights(l, g_mix_pre, w_in, b_igate, b_fgate, w_conv_mix, b_conv_mix, g_conv_norm,
                             b_conv_norm, g_mlstm_norm, w_out, g_mix_post, g_ffn_pre, w_up, w_conv_ffn,
                             b_conv_ffn, w_down, g_ffn_post, g_ple, w_ple, w_ple_gate)
        xp, c1, C1, n1, m1, f1 = _layer_prompt(xp, p_prompt, wts, l)
        pc.append(c1); pC.append(C1); pn.append(n1); pm.append(m1[:, :HEADS, 0]); pf.append(f1)

        mrow = jnp.pad(jnp.repeat(state_mlstm_m[l].astype(F32), SEQ_PAD, axis=0),
                       ((0, 0), (HEADS, LANES - 2 * HEADS)))
        xs, c2, C2, n2, m2 = _mixer_sample(xs, state_conv_mix, state_mlstm_C, state_mlstm_n, mrow,
                                           wts, seq_len, l)
        xs, f2 = _ffn_sample(xs, _pad_seq(p_sample[l]), state_conv_ffn, wts, seq_len, l)
        sc.append(c2); sC.append(C2); sn.append(n2); sf.append(f2)
        sm.append(m2.reshape(nseq, SEQ_PAD, LANES)[:, 0, :HEADS])
    ys = xs.reshape(nseq, SEQ_PAD, D_MODEL)[:, :seq_len]
    return (xp, ys, jnp.stack(pc), jnp.stack(pC), jnp.stack(pn), jnp.stack(pm), jnp.stack(pf),
            jnp.stack(sc), jnp.stack(sC), jnp.stack(sn), jnp.stack(sm), jnp.stack(sf))
```

```python
import functools

import jax
import jax.numpy as jnp
from jax import lax
from jax.experimental import pallas as pl
from jax.experimental.pallas import tpu as pltpu

F32 = jnp.float32
BF16 = jnp.bfloat16

D_MODEL = 1024
CONV_CH = 512
CONV_WIDTH = 31
CONV_TAIL = CONV_WIDTH - 1
CONV_GROUPS = 4
HEADS = 4
HEAD_DIM = 128
MLSTM_WIDTH = HEADS * HEAD_DIM
D_FF = 2816
FFN_TAIL = 2
PLE_DIM = 256
EPS = 1e-6
MAIN_COLS = 2 * CONV_CH + 4 * MLSTM_WIDTH
LANES = 128
SUBLANES = 8
NEG = -1e30

PROMPT_TILE = 256
FFN_TILE = 256
SAMPLE_SEQS = 16
SAMPLE_SEQS_FFN = 32
SEQ_PAD = SUBLANES
HIST = 32
SHIFT_ROWS = PROMPT_TILE + HIST - SUBLANES
CONV_ROWS = 64
CONV_IN_FLIGHT = 2
FFN_ROWS = 64
FFN_IN_FLIGHT = 3
FFN_CHUNK = 256
SAMPLE_EXT_ROWS = -(-(CONV_TAIL + SEQ_PAD) // SUBLANES) * SUBLANES
VMEM_LIMIT = 56 * 1024 * 1024


def _dot(a, b):
    return jnp.dot(a, b, preferred_element_type=F32)


def _dot_exact(sel, x):
    hi = x.astype(BF16)
    r1 = x - hi.astype(F32)
    mid = r1.astype(BF16)
    lo = (r1 - mid.astype(F32)).astype(BF16)
    y = _dot(jnp.where(sel, 1.0, 0.0).astype(BF16), jnp.concatenate([hi, mid, lo], axis=1))
    return y[:, 0:LANES] + y[:, LANES:2 * LANES] + y[:, 2 * LANES:3 * LANES]


def _rms(x, g):
    ms = jnp.mean(x * x, axis=-1, keepdims=True)
    return x * lax.rsqrt(ms + EPS) * g


def _layernorm(x):
    mu = jnp.mean(x, axis=-1, keepdims=True)
    xc = x - mu
    var = jnp.mean(xc * xc, axis=-1, keepdims=True)
    return xc * lax.rsqrt(var + EPS)


def _sigmoid(x):
    return 1.0 / (1.0 + jnp.exp(-x))


def _log_sigmoid(x):
    return jnp.minimum(x, 0.0) - jnp.log(1.0 + jnp.exp(-jnp.abs(x)))


def _exact_zero(v):
    bits = pltpu.bitcast(v, jnp.uint32)
    bits = lax.shift_right_logical(lax.shift_right_logical(bits, jnp.uint32(16)), jnp.uint32(16))
    return pltpu.bitcast(bits, F32)


def _gelu_tanh(x):
    return 0.5 * x * (1.0 + jnp.tanh(0.7978845608028654 * (x + 0.044715 * (x * x * x))))


def _conv_branch_post(acc, g_ref, b_ref):
    parts = []
    for g in range(CONV_GROUPS):
        sl = slice(g * LANES, (g + 1) * LANES)
        y = _layernorm(acc[:, sl]) * g_ref[:, sl] + b_ref[:, sl]
        parts.append(y * _sigmoid(y))
    return parts


def _qk(q_bf, k_bf):
    return lax.dot_general(q_bf, k_bf, (((1,), (1,)), ((), ())), preferred_element_type=F32)


def _mlstm_weighted(scores, v_bf, d, inter):
    m_t = jnp.maximum(inter, jnp.max(d, axis=1, keepdims=True))
    w_intra = jnp.exp(d - m_t)
    w_inter = jnp.exp(inter - m_t)
    s = scores * w_intra
    num = _dot(s.astype(BF16), v_bf)
    den = jnp.sum(s, axis=1, keepdims=True)
    return m_t, w_inter, num, den


def _mlstm_intra(q_bf, k_bf, v_bf, d, inter):
    return _mlstm_weighted(_qk(q_bf, k_bf), v_bf, d, inter)


def _head_out(num, den, m_t, g_mn, zo):
    hh = num / jnp.maximum(jnp.abs(den), jnp.exp(-m_t))
    return _layernorm(hh) * g_mn * _sigmoid(zo)


def _mixer_prompt_kernel(x_ref, g_pre_ref, w_in_ref, w_gate_ref, b_gate_ref, w_conv_ref, b_conv_ref,
                         g_cn_ref, b_cn_ref, g_mn_ref, w_out_ref, g_post_ref,
                         y_ref, conv_out_ref, c_out_ref, n_out_ref, m_out_ref,
                         ext_ref, sh_ref, mix_ref, cn_ref, m_ref, h_ref, z_ref):
    T = PROMPT_TILE
    s_idx = pl.program_id(1)
    last = pl.num_programs(1) - 1

    @pl.when(s_idx == 0)
    def _():
        ext_ref[0:HIST, :] = jnp.zeros((HIST, CONV_CH), F32)
        cn_ref[...] = jnp.zeros(cn_ref.shape, F32)
        m_ref[...] = jnp.zeros(m_ref.shape, F32)

    x = x_ref[...]
    h_ref[...] = _rms(x, g_pre_ref[...]).astype(BF16)

    zv = _dot(h_ref[...], w_in_ref[:, 0:CONV_CH])
    zg = _dot(h_ref[...], w_in_ref[:, CONV_CH:2 * CONV_CH])
    gates = _dot(h_ref[...], w_gate_ref[...]) + b_gate_ref[...]
    q_off = 2 * CONV_CH
    slot = lax.rem(s_idx, 2)
    for part in range(4):
        ps = slice(part * MLSTM_WIDTH, (part + 1) * MLSTM_WIDTH)
        z_ref[slot, :, ps] = _dot(h_ref[...],
                                  w_in_ref[:, q_off + part * MLSTM_WIDTH:q_off + (part + 1) * MLSTM_WIDTH])
    row = lax.broadcasted_iota(jnp.int32, (T, T), 0)
    col = lax.broadcasted_iota(jnp.int32, (T, T), 1)
    causal = col <= row
    bcum = _dot_exact(causal, _log_sigmoid(gates))
    gates_t = gates.T
    bcum_t = bcum.T

    ext_ref[HIST:HIST + T, :] = zv * _sigmoid(zg)
    for r in range(1, SUBLANES):
        sh_ref[r - 1] = ext_ref[pl.ds(r, SHIFT_ROWS), :]
    recent = []
    for g in range(CONV_GROUPS):
        cs = slice(g * LANES, (g + 1) * LANES)
        for rb in range(T // CONV_ROWS):
            acc = jnp.broadcast_to(b_conv_ref[:, cs], (CONV_ROWS, LANES))
            for j in range(CONV_WIDTH):
                off = HIST - CONV_TAIL + j
                r, base = off % SUBLANES, rb * CONV_ROWS + off - off % SUBLANES
                src = ext_ref if r == 0 else sh_ref.at[r - 1]
                w_row = w_conv_ref[j:j + 1, cs]
                if j == 0 and len(recent) == CONV_IN_FLIGHT:
                    w_row = w_row + _exact_zero(recent.pop(0))
                acc = acc + w_row * src[base:base + CONV_ROWS, cs]
            y = _layernorm(acc) * g_cn_ref[:, cs] + b_cn_ref[:, cs]
            mix_ref[rb * CONV_ROWS:(rb + 1) * CONV_ROWS, cs] = (y * _sigmoid(y)).astype(BF16)
            recent.append(acc[0:1, :])
    ext_ref[0:HIST, :] = ext_ref[T:T + HIST, :]

    heads = []
    for hd in range(HEADS):
        c0 = hd * HEAD_DIM
        zvv = z_ref[slot, :, 2 * MLSTM_WIDTH + c0:2 * MLSTM_WIDTH + c0 + HEAD_DIM]
        q_bf = (z_ref[slot, :, c0:c0 + HEAD_DIM] * (HEAD_DIM ** -0.5)).astype(BF16)
        k_bf = z_ref[slot, :, MLSTM_WIDTH + c0:MLSTM_WIDTH + c0 + HEAD_DIM].astype(BF16)
        cn = cn_ref[hd]
        scores = _qk(q_bf, k_bf)
        carried = _dot(q_bf, cn.astype(BF16))
        heads.append((zvv, k_bf, cn, scores, carried))

    finished = []
    for hd in range(HEADS):
        zvv, k_bf, cn, scores, carried = heads[hd]
        i_row = gates_t[hd:hd + 1, :]
        i_col = gates[:, hd:hd + 1]
        b_row = bcum_t[HEADS + hd:HEADS + hd + 1, :]
        b_col = bcum[:, HEADS + hd:HEADS + hd + 1]
        m_prev = m_ref[hd:hd + 1, 0:1]
        d = jnp.where(causal, b_col - b_row + i_row, NEG)
        inter = b_col + m_prev
        m_t, w_inter, num, den = _mlstm_weighted(scores, zvv.astype(BF16), d, inter)
        b_last = b_col[T - 1:T, :]
        m_new = m_t[T - 1:T, :]
        decay = w_inter[T - 1:T, :]
        ws = jnp.exp(b_last - b_col + i_col - m_new)
        vp = jnp.concatenate([ws * zvv, jnp.broadcast_to(ws, (T, HEAD_DIM))], axis=1).astype(BF16)
        kv = lax.dot_general(k_bf, vp, (((0,), (0,)), ((), ())), preferred_element_type=F32)
        finished.append((m_t, w_inter, num, den, m_new, decay, kv))

    for hd in range(HEADS):
        c0 = hd * HEAD_DIM
        _, _, cn, _, carried = heads[hd]
        m_t, w_inter, num, den, m_new, decay, kv = finished[hd]
        zo = z_ref[slot, :, 3 * MLSTM_WIDTH + c0:3 * MLSTM_WIDTH + c0 + HEAD_DIM]
        num = num + carried[:, 0:HEAD_DIM] * w_inter
        den = den + carried[:, HEAD_DIM:HEAD_DIM + 1] * w_inter
        mix_ref[:, CONV_CH + c0:CONV_CH + c0 + HEAD_DIM] = _head_out(
            num, den, m_t, g_mn_ref[:, c0:c0 + HEAD_DIM], zo).astype(BF16)
        cn_ref[hd] = decay * cn + kv
        m_ref[hd:hd + 1, :] = jnp.broadcast_to(m_new, (1, LANES))

    y_ref[...] = x + _rms(_dot(mix_ref[...], w_out_ref[...]), g_post_ref[...])

    @pl.when(s_idx == last)
    def _():
        conv_out_ref[...] = ext_ref[pl.ds(HIST - CONV_TAIL, CONV_TAIL), :]
        m_out_ref[...] = m_ref[...]
        for hd in range(HEADS):
            cn = cn_ref[hd]
            c_out_ref[hd] = cn[:, 0:HEAD_DIM]
            n_out_ref[hd:hd + 1, :] = cn[:, HEAD_DIM:].T[0:1, :]


def _full(shape):
    n = len(shape)
    return pl.BlockSpec(shape, lambda *_: (0,) * n, pipeline_mode=pl.Buffered(1))


def _mixer_prompt(x, wts):
    B, S, D = x.shape
    T = PROMPT_TILE
    weights = (wts["g_mix_pre"], wts["w_in"], wts["w_gate"], wts["b_gate"], wts["w_conv_mix"],
               wts["b_conv_mix"], wts["g_conv_norm"], wts["b_conv_norm"], wts["g_mlstm_norm"],
               wts["w_out"], wts["g_mix_post"])
    out_shape = (
        jax.ShapeDtypeStruct((B, S, D), F32),
        jax.ShapeDtypeStruct((B, CONV_TAIL, CONV_CH), F32),
        jax.ShapeDtypeStruct((B, HEADS, HEAD_DIM, HEAD_DIM), F32),
        jax.ShapeDtypeStruct((B, HEADS, HEAD_DIM), F32),
        jax.ShapeDtypeStruct((B, SUBLANES, LANES), F32),
    )
    out_specs = (
        pl.BlockSpec((None, T, D), lambda b, s: (b, s, 0)),
        pl.BlockSpec((None, CONV_TAIL, CONV_CH), lambda b, s: (b, 0, 0)),
        pl.BlockSpec((None, HEADS, HEAD_DIM, HEAD_DIM), lambda b, s: (b, 0, 0, 0)),
        pl.BlockSpec((None, HEADS, HEAD_DIM), lambda b, s: (b, 0, 0)),
        pl.BlockSpec((None, SUBLANES, LANES), lambda b, s: (b, 0, 0)),
    )
    return pl.pallas_call(
        _mixer_prompt_kernel,
        grid=(B, S // T),
        in_specs=[pl.BlockSpec((None, T, D), lambda b, s: (b, s, 0))] + [_full(w.shape) for w in weights],
        out_specs=out_specs,
        out_shape=out_shape,
        scratch_shapes=[
            pltpu.VMEM((HIST + T, CONV_CH), F32),
            pltpu.VMEM((SUBLANES - 1, SHIFT_ROWS, CONV_CH), F32),
            pltpu.VMEM((T, D_MODEL), BF16),
            pltpu.VMEM((HEADS, HEAD_DIM, 2 * HEAD_DIM), F32),
            pltpu.VMEM((SUBLANES, LANES), F32),
            pltpu.VMEM((T, D_MODEL), BF16),
            pltpu.VMEM((2, T, 4 * MLSTM_WIDTH), F32),
        ],
        compiler_params=pltpu.CompilerParams(
            dimension_semantics=("arbitrary", "arbitrary"), vmem_limit_bytes=VMEM_LIMIT),
        name="mixer_prompt",
    )(x, *weights)


def _ffn_tail(x, f_ref, p, w_down_ref, g_post_ref, g_ple_ref, w_ple_ref, w_pg_ref):
    x2 = x + _rms(_dot(f_ref[...], w_down_ref[...]), g_post_ref[...])
    emb = _dot(p.astype(BF16), w_ple_ref[...])
    gate = _sigmoid(_dot(_rms(x2, g_ple_ref[...]).astype(BF16), w_pg_ref[...]))
    return x2 + emb * gate


def _ffn_prompt_kernel(x_ref, p_ref, g_pre_ref, w_up_ref, w_conv_ref, b_conv_ref, w_down_ref,
                       g_post_ref, g_ple_ref, w_ple_ref, w_pg_ref,
                       y_ref, tail_out_ref,
                       hist_ref, f_ref):
    T = FFN_TILE
    G = T // SUBLANES
    s_idx = pl.program_id(1)
    last = pl.num_programs(1) - 1

    @pl.when(s_idx == 0)
    def _():
        hist_ref[...] = jnp.zeros(hist_ref.shape, F32)

    x = x_ref[...]
    h = _rms(x, g_pre_ref[...]).astype(BF16)
    sub = lax.broadcasted_iota(jnp.int32, (T, FFN_CHUNK), 0) % SUBLANES

    def delayed(u, prev, k):
        rot = jnp.concatenate([pltpu.roll(prev, k, 0)]
                              + [pltpu.roll(u[g * SUBLANES:(g + 1) * SUBLANES, :], k, 0) for g in range(G)], axis=0)
        return jnp.where(sub < k, rot[0:T, :], rot[SUBLANES:SUBLANES + T, :])

    for c in range(D_FF // FFN_CHUNK):
        halves = []
        for half in range(2):
            c0 = half * D_FF + c * FFN_CHUNK
            cs = slice(c0, c0 + FFN_CHUNK)
            u = _dot(h, w_up_ref[:, cs])
            prev = hist_ref[:, cs]
            hist_ref[:, cs] = u[T - SUBLANES:T, :]
            halves.append(w_conv_ref[0:1, cs] * delayed(u, prev, 2) + w_conv_ref[1:2, cs] * delayed(u, prev, 1)
                          + w_conv_ref[2:3, cs] * u + b_conv_ref[:, cs])
        f_ref[:, c * FFN_CHUNK:(c + 1) * FFN_CHUNK] = (_gelu_tanh(halves[0]) * halves[1]).astype(BF16)

    y_ref[...] = _ffn_tail(x, f_ref, p_ref[...], w_down_ref, g_post_ref, g_ple_ref, w_ple_ref, w_pg_ref)

    @pl.when(s_idx == last)
    def _():
        tail_out_ref[...] = hist_ref[SUBLANES - FFN_TAIL:SUBLANES, :]


def _ffn_prompt(x, p, wts, layer):
    B, S, D = x.shape
    T = FFN_TILE
    weights = (wts["g_ffn_pre"], wts["w_up"], wts["w_conv_ffn"], wts["b_conv_ffn"], wts["w_down"],
               wts["g_ffn_post"], wts["g_ple"], wts["w_ple"], wts["w_ple_gate"])
    return pl.pallas_call(
        _ffn_prompt_kernel,
        grid=(B, S // T),
        in_specs=[pl.BlockSpec((None, T, D), lambda b, s: (b, s, 0)),
                  pl.BlockSpec((None, None, T, PLE_DIM), lambda b, s: (layer, b, s, 0))]
                 + [_full(w.shape) for w in weights],
        out_specs=(pl.BlockSpec((None, T, D), lambda b, s: (b, s, 0)),
                   pl.BlockSpec((None, FFN_TAIL, 2 * D_FF), lambda b, s: (b, 0, 0))),
        out_shape=(jax.ShapeDtypeStruct((B, S, D), F32),
                   jax.ShapeDtypeStruct((B, FFN_TAIL, 2 * D_FF), F32)),
        scratch_shapes=[
            pltpu.VMEM((SUBLANES, 2 * D_FF), F32),
            pltpu.VMEM((T, D_FF), BF16),
        ],
        compiler_params=pltpu.CompilerParams(
            dimension_semantics=("arbitrary", "arbitrary"), vmem_limit_bytes=VMEM_LIMIT),
        name="ffn_prompt",
    )(x, p, *weights)


def _layer_prompt_kernel(x_ref, p_ref,
                         g_pre_ref, w_in_ref, w_gate_ref, b_gate_ref, w_cm_ref, b_cm_ref,
                         g_cn_ref, b_cn_ref, g_mn_ref, w_out_ref, g_post_ref,
                         g_fpre_ref, w_up_ref, w_cf_ref, b_cf_ref, w_down_ref, g_fpost_ref,
                         g_ple_ref, w_ple_ref, w_pg_ref,
                         y_ref, conv_out_ref, c_out_ref, n_out_ref, m_out_ref, tail_out_ref,
                         ext_ref, sh_ref, mix_ref, cn_ref, m_ref, h_ref, z_ref, x1_ref, hist_ref, fw_ref, f_ref,
                         *, tiles_per_seq):
    T = PROMPT_TILE
    i = pl.program_id(0)
    n_tiles = pl.num_programs(0) - 1
    s_mix = lax.rem(jnp.minimum(i, n_tiles - 1), tiles_per_seq)
    s_ffn = lax.rem(jnp.maximum(i - 1, 0), tiles_per_seq)
    slot = lax.rem(i, 2)

    @pl.when(i == 0)
    def _():
        x1_ref[...] = jnp.zeros(x1_ref.shape, F32)

    @pl.when(s_mix == 0)
    def _():
        ext_ref[0:HIST, :] = jnp.zeros((HIST, CONV_CH), F32)
        cn_ref[...] = jnp.zeros(cn_ref.shape, F32)
        m_ref[...] = jnp.zeros(m_ref.shape, F32)

    @pl.when(s_ffn == 0)
    def _():
        hist_ref[...] = jnp.zeros(hist_ref.shape, F32)

    x = x_ref[...]
    h_ref[...] = _rms(x, g_pre_ref[...]).astype(BF16)
    zv = _dot(h_ref[...], w_in_ref[:, 0:CONV_CH])
    zg = _dot(h_ref[...], w_in_ref[:, CONV_CH:2 * CONV_CH])
    gates = _dot(h_ref[...], w_gate_ref[...]) + b_gate_ref[...]
    q_off = 2 * CONV_CH
    for part in range(4):
        ps = slice(part * MLSTM_WIDTH, (part + 1) * MLSTM_WIDTH)
        z_ref[slot, :, ps] = _dot(h_ref[...],
                                  w_in_ref[:, q_off + part * MLSTM_WIDTH:q_off + (part + 1) * MLSTM_WIDTH])

    ext_ref[HIST:HIST + T, :] = zv * _sigmoid(zg)
    for r in range(1, SUBLANES):
        sh_ref[r - 1] = ext_ref[pl.ds(r, SHIFT_ROWS), :]
    recent = []
    for g in range(CONV_GROUPS):
        cs = slice(g * LANES, (g + 1) * LANES)
        for rb in range(T // CONV_ROWS):
            acc = jnp.broadcast_to(b_cm_ref[:, cs], (CONV_ROWS, LANES))
            for j in range(CONV_WIDTH):
                off = HIST - CONV_TAIL + j
                r, base = off % SUBLANES, rb * CONV_ROWS + off - off % SUBLANES
                src = ext_ref if r == 0 else sh_ref.at[r - 1]
                w_row = w_cm_ref[j:j + 1, cs]
                if j == 0 and len(recent) == CONV_IN_FLIGHT:
                    w_row = w_row + _exact_zero(recent.pop(0))
                acc = acc + w_row * src[base:base + CONV_ROWS, cs]
            y = _layernorm(acc) * g_cn_ref[:, cs] + b_cn_ref[:, cs]
            mix_ref[rb * CONV_ROWS:(rb + 1) * CONV_ROWS, cs] = (y * _sigmoid(y)).astype(BF16)
            recent.append(acc[0:1, :])
    ext_ref[0:HIST, :] = ext_ref[T:T + HIST, :]

    x1 = x1_ref[1 - slot]
    h2 = _rms(x1, g_fpre_ref[...]).astype(BF16)
    G = T // SUBLANES
    wr = lax.broadcasted_iota(jnp.int32, (T, T), 0)
    wc = lax.broadcasted_iota(jnp.int32, (T, T), 1)
    to_work = jnp.where(wc == (wr % SUBLANES) * G + wr // SUBLANES, 1.0, 0.0).astype(BF16)
    to_token = jnp.where(wr == (wc % SUBLANES) * G + wc // SUBLANES, 1.0, 0.0).astype(BF16)
    hp = _dot(to_work, h2).astype(BF16)
    first = lax.broadcasted_iota(jnp.int32, (SUBLANES, FFN_CHUNK), 0) == 0

    def wrapped(group, prev_group):
        return jnp.where(first, pltpu.roll(prev_group, 1, 0), pltpu.roll(group, 1, 0))

    for c in range(D_FF // FFN_CHUNK):
        halves = []
        for half in range(2):
            c0 = half * D_FF + c * FFN_CHUNK
            cs = slice(c0, c0 + FFN_CHUNK)
            u = _dot(hp, w_up_ref[:, cs])
            prev = hist_ref[:, cs]
            hist_ref[:, cs] = u[T - 2 * SUBLANES:T, :]
            w_last = wrapped(u[T - SUBLANES:T, :], prev[SUBLANES:2 * SUBLANES, :])
            w_last2 = wrapped(u[T - 2 * SUBLANES:T - SUBLANES, :], prev[0:SUBLANES, :])
            u1 = jnp.concatenate([w_last, u[0:T - SUBLANES, :]], axis=0)
            u2 = jnp.concatenate([w_last2, w_last, u[0:T - 2 * SUBLANES, :]], axis=0)
            halves.append(w_cf_ref[0:1, cs] * u2 + w_cf_ref[1:2, cs] * u1
                          + w_cf_ref[2:3, cs] * u + b_cf_ref[:, cs])
        fw_ref[:, c * FFN_CHUNK:(c + 1) * FFN_CHUNK] = (_gelu_tanh(halves[0]) * halves[1]).astype(BF16)

    row = lax.broadcasted_iota(jnp.int32, (T, T), 0)
    col = lax.broadcasted_iota(jnp.int32, (T, T), 1)
    causal = col <= row
    bcum = _dot_exact(causal, _log_sigmoid(gates))
    gates_t = gates.T
    bcum_t = bcum.T
    heads = []
    for hd in range(HEADS):
        c0 = hd * HEAD_DIM
        zvv = z_ref[slot, :, 2 * MLSTM_WIDTH + c0:2 * MLSTM_WIDTH + c0 + HEAD_DIM]
        q_bf = (z_ref[slot, :, c0:c0 + HEAD_DIM] * (HEAD_DIM ** -0.5)).astype(BF16)
        k_bf = z_ref[slot, :, MLSTM_WIDTH + c0:MLSTM_WIDTH + c0 + HEAD_DIM].astype(BF16)
        cn = cn_ref[hd]
        heads.append((zvv, k_bf, cn, _qk(q_bf, k_bf), _dot(q_bf, cn.astype(BF16))))

    finished = []
    for hd in range(HEADS):
        zvv, k_bf, cn, scores, carried = heads[hd]
        i_row = gates_t[hd:hd + 1, :]
        i_col = gates[:, hd:hd + 1]
        b_row = bcum_t[HEADS + hd:HEADS + hd + 1, :]
        b_col = bcum[:, HEADS + hd:HEADS + hd + 1]
        m_prev = m_ref[hd:hd + 1, 0:1]
        d = jnp.where(causal, b_col - b_row + i_row, NEG)
        m_t, w_inter, num, den = _mlstm_weighted(scores, zvv.astype(BF16), d, b_col + m_prev)
        m_new = m_t[T - 1:T, :]
        ws = jnp.exp(b_col[T - 1:T, :] - b_col + i_col - m_new)
        vp = jnp.concatenate([ws * zvv, jnp.broadcast_to(ws, (T, HEAD_DIM))], axis=1).astype(BF16)
        kv = lax.dot_general(k_bf, vp, (((0,), (0,)), ((), ())), preferred_element_type=F32)
        finished.append((m_t, w_inter, num, den, m_new, w_inter[T - 1:T, :], kv))

    for hd in range(HEADS):
        c0 = hd * HEAD_DIM
        _, _, cn, _, carried = heads[hd]
        m_t, w_inter, num, den, m_new, decay, kv = finished[hd]
        zo = z_ref[slot, :, 3 * MLSTM_WIDTH + c0:3 * MLSTM_WIDTH + c0 + HEAD_DIM]
        num = num + carried[:, 0:HEAD_DIM] * w_inter
        den = den + carried[:, HEAD_DIM:HEAD_DIM + 1] * w_inter
        mix_ref[:, CONV_CH + c0:CONV_CH + c0 + HEAD_DIM] = _head_out(
            num, den, m_t, g_mn_ref[:, c0:c0 + HEAD_DIM], zo).astype(BF16)
        cn_ref[hd] = decay * cn + kv
        m_ref[hd:hd + 1, :] = jnp.broadcast_to(m_new, (1, LANES))

    for c in range(D_FF // FFN_CHUNK):
        fs = slice(c * FFN_CHUNK, (c + 1) * FFN_CHUNK)
        f_ref[:, fs] = _dot(to_token, fw_ref[:, fs]).astype(BF16)
    y_ref[...] = _ffn_tail(x1, f_ref, p_ref[...], w_down_ref, g_fpost_ref, g_ple_ref, w_ple_ref, w_pg_ref)

    x1_ref[slot] = x + _rms(_dot(mix_ref[...], w_out_ref[...]), g_post_ref[...])

    @pl.when((s_mix == tiles_per_seq - 1) & (i < n_tiles))
    def _():
        conv_out_ref[...] = ext_ref[pl.ds(HIST - CONV_TAIL, CONV_TAIL), :]
        m_out_ref[...] = m_ref[...]
        for hd in range(HEADS):
            cn = cn_ref[hd]
            c_out_ref[hd] = cn[:, 0:HEAD_DIM]
            n_out_ref[hd:hd + 1, :] = cn[:, HEAD_DIM:].T[0:1, :]

    @pl.when((s_ffn == tiles_per_seq - 1) & (i >= 1))
    def _():
        tail_out_ref[0:1, :] = hist_ref[SUBLANES - 1:SUBLANES, :]
        tail_out_ref[1:2, :] = hist_ref[2 * SUBLANES - 1:2 * SUBLANES, :]


def _layer_prompt(x, p, wts, layer):
    B, S, D = x.shape
    T = PROMPT_TILE
    tps = S // T
    n_tiles = B * tps
    weights = (wts["g_mix_pre"], wts["w_in"], wts["w_gate"], wts["b_gate"], wts["w_conv_mix"],
               wts["b_conv_mix"], wts["g_conv_norm"], wts["b_conv_norm"], wts["g_mlstm_norm"],
               wts["w_out"], wts["g_mix_post"],
               wts["g_ffn_pre"], wts["w_up"], wts["w_conv_ffn"], wts["b_conv_ffn"], wts["w_down"],
               wts["g_ffn_post"], wts["g_ple"], wts["w_ple"], wts["w_ple_gate"])
    mix_tile = lambda i: jnp.minimum(i, n_tiles - 1)
    ffn_tile = lambda i: jnp.maximum(i - 1, 0)
    out_shape = (
        jax.ShapeDtypeStruct((B, S, D), F32),
        jax.ShapeDtypeStruct((B, CONV_TAIL, CONV_CH), F32),
        jax.ShapeDtypeStruct((B, HEADS, HEAD_DIM, HEAD_DIM), F32),
        jax.ShapeDtypeStruct((B, HEADS, HEAD_DIM), F32),
        jax.ShapeDtypeStruct((B, SUBLANES, LANES), F32),
        jax.ShapeDtypeStruct((B, FFN_TAIL, 2 * D_FF), F32),
    )
    out_specs = (
        pl.BlockSpec((None, T, D), lambda i: (ffn_tile(i) // tps, ffn_tile(i) % tps, 0)),
        pl.BlockSpec((None, CONV_TAIL, CONV_CH), lambda i: (mix_tile(i) // tps, 0, 0)),
        pl.BlockSpec((None, HEADS, HEAD_DIM, HEAD_DIM), lambda i: (mix_tile(i) // tps, 0, 0, 0)),
        pl.BlockSpec((None, HEADS, HEAD_DIM), lambda i: (mix_tile(i) // tps, 0, 0)),
        pl.BlockSpec((None, SUBLANES, LANES), lambda i: (mix_tile(i) // tps, 0, 0)),
        pl.BlockSpec((None, FFN_TAIL, 2 * D_FF), lambda i: (ffn_tile(i) // tps, 0, 0)),
    )
    return pl.pallas_call(
        functools.partial(_layer_prompt_kernel, tiles_per_seq=tps),
        grid=(n_tiles + 1,),
        in_specs=[pl.BlockSpec((None, T, D), lambda i: (mix_tile(i) // tps, mix_tile(i) % tps, 0)),
                  pl.BlockSpec((None, None, T, PLE_DIM),
                               lambda i: (layer, ffn_tile(i) // tps, ffn_tile(i) % tps, 0))]
                 + [_full(w.shape) for w in weights],
        out_specs=out_specs,
        out_shape=out_shape,
        scratch_shapes=[
            pltpu.VMEM((HIST + T, CONV_CH), F32),
            pltpu.VMEM((SUBLANES - 1, SHIFT_ROWS, CONV_CH), F32),
            pltpu.VMEM((T, D_MODEL), BF16),
            pltpu.VMEM((HEADS, HEAD_DIM, 2 * HEAD_DIM), F32),
            pltpu.VMEM((SUBLANES, LANES), F32),
            pltpu.VMEM((T, D_MODEL), BF16),
            pltpu.VMEM((2, T, 4 * MLSTM_WIDTH), F32),
            pltpu.VMEM((2, T, D_MODEL), F32),
            pltpu.VMEM((2 * SUBLANES, 2 * D_FF), F32),
            pltpu.VMEM((T, D_FF), BF16),
            pltpu.VMEM((T, D_FF), BF16),
        ],
        compiler_params=pltpu.CompilerParams(
            dimension_semantics=("arbitrary",), vmem_limit_bytes=VMEM_LIMIT),
        name="layer_prompt",
    )(x, p, *weights)


def _mixer_sample_kernel(x_ref, st_ref, c_ref, n_ref, mrow_ref,
                         g_pre_ref, w_in_ref, w_gate_ref, b_gate_ref, w_conv_ref, b_conv_ref,
                         g_cn_ref, b_cn_ref, g_mn_ref, w_out_ref, g_post_ref,
                         y_ref, conv_out_ref, c_out_ref, n_out_ref, m_out_ref,
                         ext_ref, q_ref, kt_ref, wv_ref, wk_ref, carried_ref, qn_ref, dec_ref,
                         *, seq_len, stacked_first):
    NB = SAMPLE_SEQS
    R = NB * SEQ_PAD
    if stacked_first:
        c_dst = c_out_ref.at[0]
        c_out_ref[1:] = jnp.zeros((c_out_ref.shape[0] - 1,) + tuple(c_out_ref.shape[1:]), F32)
    else:
        c_dst = c_out_ref
    x = x_ref[...]
    h = _rms(x, g_pre_ref[...]).astype(BF16)

    zv = _dot(h, w_in_ref[:, 0:CONV_CH])
    zg = _dot(h, w_in_ref[:, CONV_CH:2 * CONV_CH])
    a = zv * _sigmoid(zg)
    ext_ref[:, 0:CONV_TAIL, :] = st_ref[...]
    ext_ref[:, CONV_TAIL:CONV_TAIL + SEQ_PAD, :] = a.reshape(NB, SEQ_PAD, CONV_CH)
    ext_ref[:, CONV_TAIL + SEQ_PAD:, :] = jnp.zeros((NB, SAMPLE_EXT_ROWS - CONV_TAIL - SEQ_PAD, CONV_CH), F32)
    acc = jnp.broadcast_to(b_conv_ref[...][None], (NB, SEQ_PAD, CONV_CH))
    for j in range(CONV_WIDTH):
        acc = acc + w_conv_ref[j:j + 1, :][None] * ext_ref[:, pl.ds(j, SEQ_PAD), :]
    conv_out_ref[...] = ext_ref[:, seq_len:seq_len + CONV_TAIL, :]
    mix_parts = _conv_branch_post(acc.reshape(R, CONV_CH), g_cn_ref, b_cn_ref)

    gates = _dot(h, w_gate_ref[...]) + b_gate_ref[...]
    row = lax.broadcasted_iota(jnp.int32, (R, R), 0)
    col = lax.broadcasted_iota(jnp.int32, (R, R), 1)
    same_seq = (row // SEQ_PAD) == (col // SEQ_PAD)
    causal = same_seq & (col <= row)
    bcum = _dot_exact(causal, _log_sigmoid(gates))
    mask = causal & ((col % SEQ_PAD) < seq_len)
    pick_last = same_seq & ((col % SEQ_PAD) == seq_len - 1)
    inter_all = bcum + mrow_ref[...]
    gates_t = gates.T
    bcum_t = bcum.T
    lane = lax.broadcasted_iota(jnp.int32, (R, LANES), 1)
    row_valid = (lax.broadcasted_iota(jnp.int32, (R, 1), 0) % SEQ_PAD) < seq_len
    stats = jnp.where((lane >= HEADS) & (lane < 2 * HEADS), bcum, 0.0)
    q_off = 2 * CONV_CH
    z_parts = [_dot(h, w_in_ref[:, q_off + part * MLSTM_WIDTH:q_off + (part + 1) * MLSTM_WIDTH])
               for part in range(4)]
    saved = []
    for hd in range(HEADS):
        c0 = hd * HEAD_DIM
        zq = z_parts[0][:, c0:c0 + HEAD_DIM] * (HEAD_DIM ** -0.5)
        zk = z_parts[1][:, c0:c0 + HEAD_DIM]
        zvv = z_parts[2][:, c0:c0 + HEAD_DIM]
        zo = z_parts[3][:, c0:c0 + HEAD_DIM]
        i_row = gates_t[hd:hd + 1, :]
        b_row = bcum_t[HEADS + hd:HEADS + hd + 1, :]
        b_col = bcum[:, HEADS + hd:HEADS + hd + 1]
        d = jnp.where(mask, b_col - b_row + i_row, NEG)
        inter = inter_all[:, HEADS + hd:HEADS + hd + 1]
        m_t, w_inter, num, den = _mlstm_intra(zq.astype(BF16), zk.astype(BF16), zvv.astype(BF16), d, inter)
        stats = jnp.where(lane == hd, m_t, stats)
        stats = jnp.where(lane == 2 * HEADS + hd, w_inter, stats)
        q_ref[hd, 0:R, :] = zq
        q_ref[hd, R:R + SEQ_PAD, :] = jnp.zeros((SEQ_PAD, HEAD_DIM), F32)
        kt_ref[hd] = zk.T
        saved.append((m_t, w_inter, num, den, zo, zk, zvv))

    per_seq = _dot_exact(pick_last, stats)
    m_out_ref[...] = per_seq
    for hd in range(HEADS):
        zk, zvv = saved[hd][5], saved[hd][6]
        m_new = per_seq[:, hd:hd + 1]
        b_last = per_seq[:, HEADS + hd:HEADS + hd + 1]
        decay = per_seq[:, 2 * HEADS + hd:2 * HEADS + hd + 1]
        b_col = bcum[:, HEADS + hd:HEADS + hd + 1]
        i_col = gates[:, hd:hd + 1]
        ws = jnp.where(row_valid, jnp.exp(b_last - b_col + i_col - m_new), 0.0)
        wv_ref[hd] = (ws * zvv).astype(BF16)
        wk_ref[hd] = ws * zk
        dec_ref[hd] = jnp.broadcast_to(decay, (R, LANES))

    col_seq = lax.broadcasted_iota(jnp.int32, (HEAD_DIM, R), 1) // SEQ_PAD

    def per_sequence(b, carry):
        r0 = pl.multiple_of(b * SEQ_PAD, SEQ_PAD)
        for hd in range(HEADS):
            c_old = c_ref[b, hd]
            n_old = n_ref[b, hd:hd + 1, :]
            q2 = q_ref[hd, pl.ds(r0, 2 * SEQ_PAD), :]
            carried_ref[hd, pl.ds(r0, SEQ_PAD), :] = _dot(q2.astype(BF16), c_old.astype(BF16))[0:SEQ_PAD, :]
            qn = jnp.sum(q2[0:SEQ_PAD, :] * n_old, axis=1, keepdims=True)
            qn_ref[hd, pl.ds(r0, SEQ_PAD), :] = jnp.broadcast_to(qn, (SEQ_PAD, LANES))
            dec = dec_ref[hd, pl.ds(r0, 1), :]
            kt_b = jnp.where(col_seq == b, kt_ref[hd], 0.0).astype(BF16)
            c_dst[b, hd] = dec * c_old + _dot(kt_b, wv_ref[hd])
            n_out_ref[b, hd:hd + 1, :] = dec * n_old + jnp.sum(wk_ref[hd, pl.ds(r0, SEQ_PAD), :], axis=0,
                                                                keepdims=True)
        return carry

    lax.fori_loop(0, NB, per_sequence, 0, unroll=4)

    for hd in range(HEADS):
        c0 = hd * HEAD_DIM
        m_t, w_inter, num, den, zo = saved[hd][:5]
        num = num + carried_ref[hd] * w_inter
        den = den + qn_ref[hd][:, 0:1] * w_inter
        mix_parts.append(_head_out(num, den, m_t, g_mn_ref[:, c0:c0 + HEAD_DIM], zo))

    mix = jnp.concatenate(mix_parts, axis=1).astype(BF16)
    y_ref[...] = x + _rms(_dot(mix, w_out_ref[...]), g_post_ref[...])


N_MIXER_SAMPLE_INPUTS = 16


def _mixer_sample_kernel_inplace(*refs, seq_len):
    _mixer_sample_kernel(*refs[:N_MIXER_SAMPLE_INPUTS], *refs[N_MIXER_SAMPLE_INPUTS + 1:],
                         seq_len=seq_len, stacked_first=False)


def _mixer_sample(x, st, c, n, mrow, wts, seq_len, layer, c_stack):
    NB = SAMPLE_SEQS
    R = NB * SEQ_PAD
    depth, nseq = c.shape[0], c.shape[1]
    weights = (wts["g_mix_pre"], wts["w_in"], wts["w_gate"], wts["b_gate"], wts["w_conv_mix"],
               wts["b_conv_mix"], wts["g_conv_norm"], wts["b_conv_norm"], wts["g_mlstm_norm"],
               wts["w_out"], wts["g_mix_post"])
    rows = lambda width: pl.BlockSpec((R, width), lambda i: (i, 0))
    st_spec = pl.BlockSpec((NB, CONV_TAIL, CONV_CH), lambda i: (i, 0, 0))
    c_spec = pl.BlockSpec((NB, HEADS, HEAD_DIM, HEAD_DIM), lambda i: (i, 0, 0, 0))
    n_spec = pl.BlockSpec((NB, HEADS, HEAD_DIM), lambda i: (i, 0, 0))
    st_in = pl.BlockSpec((None, NB, CONV_TAIL, CONV_CH), lambda i: (layer, i, 0, 0))
    c_in = pl.BlockSpec((None, NB, HEADS, HEAD_DIM, HEAD_DIM), lambda i: (layer, i, 0, 0, 0))
    n_in = pl.BlockSpec((None, NB, HEADS, HEAD_DIM), lambda i: (layer, i, 0, 0))
    in_specs = [rows(D_MODEL), st_in, c_in, n_in, rows(LANES)] + [_full(w.shape) for w in weights]
    operands = (x, st, c, n, mrow, *weights)
    assert len(operands) == N_MIXER_SAMPLE_INPUTS
    if c_stack is None:
        body = functools.partial(_mixer_sample_kernel, seq_len=seq_len, stacked_first=True)
        c_out_spec = pl.BlockSpec((depth, NB, HEADS, HEAD_DIM, HEAD_DIM), lambda i: (0, i, 0, 0, 0))
        aliases = {}
    else:
        body = functools.partial(_mixer_sample_kernel_inplace, seq_len=seq_len)
        c_out_spec = c_in
        in_specs.append(pl.BlockSpec(memory_space=pl.ANY))
        operands = operands + (c_stack,)
        aliases = {N_MIXER_SAMPLE_INPUTS: 2}
    return pl.pallas_call(
        body,
        grid=(nseq // NB,),
        in_specs=in_specs,
        out_specs=(rows(D_MODEL), st_spec, c_out_spec, n_spec, rows(LANES)),
        out_shape=(jax.ShapeDtypeStruct(x.shape, F32), jax.ShapeDtypeStruct(st.shape[1:], F32),
                   jax.ShapeDtypeStruct(c.shape, F32), jax.ShapeDtypeStruct(n.shape[1:], F32),
                   jax.ShapeDtypeStruct(mrow.shape, F32)),
        input_output_aliases=aliases,
        scratch_shapes=[
            pltpu.VMEM((NB, SAMPLE_EXT_ROWS, CONV_CH), F32),
            pltpu.VMEM((HEADS, R + SEQ_PAD, HEAD_DIM), F32),
            pltpu.VMEM((HEADS, HEAD_DIM, R), F32),
            pltpu.VMEM((HEADS, R, HEAD_DIM), BF16),
            pltpu.VMEM((HEADS, R, HEAD_DIM), F32),
            pltpu.VMEM((HEADS, R, HEAD_DIM), F32),
            pltpu.VMEM((HEADS, R, LANES), F32),
            pltpu.VMEM((HEADS, R, LANES), F32),
        ],
        compiler_params=pltpu.CompilerParams(
            dimension_semantics=("arbitrary",), vmem_limit_bytes=VMEM_LIMIT),
        name="mixer_sample",
    )(*operands)


def _ffn_sample_kernel(x_ref, p_ref, st_ref, g_pre_ref, w_up_ref, w_conv_ref, b_conv_ref, w_down_ref,
                       g_post_ref, g_ple_ref, w_ple_ref, w_pg_ref,
                       y_ref, tail_out_ref,
                       ubuf_ref, f_ref, *, seq_len):
    NB = SAMPLE_SEQS_FFN
    R = NB * SEQ_PAD
    x = x_ref[...]
    h = _rms(x, g_pre_ref[...]).astype(BF16)
    lo = SEQ_PAD - FFN_TAIL
    for c in range(D_FF // FFN_CHUNK):
        halves = []
        for half in range(2):
            c0 = half * D_FF + c * FFN_CHUNK
            cs = slice(c0, c0 + FFN_CHUNK)
            u = _dot(h, w_up_ref[:, cs])
            ubuf_ref[half, :, lo:SEQ_PAD, :] = st_ref[:, :, cs]
            ubuf_ref[half, :, SEQ_PAD:2 * SEQ_PAD, :] = u.reshape(NB, SEQ_PAD, FFN_CHUNK)
            tail_out_ref[:, :, cs] = ubuf_ref[half, :, lo + seq_len:SEQ_PAD + seq_len, :]
            y = (w_conv_ref[0:1, cs][None] * ubuf_ref[half, :, pl.ds(lo, SEQ_PAD), :]
                 + w_conv_ref[1:2, cs][None] * ubuf_ref[half, :, pl.ds(lo + 1, SEQ_PAD), :]
                 + w_conv_ref[2:3, cs][None] * ubuf_ref[half, :, pl.ds(lo + 2, SEQ_PAD), :]
                 + b_conv_ref[:, cs][None])
            halves.append(y.reshape(R, FFN_CHUNK))
        f_ref[:, c * FFN_CHUNK:(c + 1) * FFN_CHUNK] = (_gelu_tanh(halves[0]) * halves[1]).astype(BF16)
    y_ref[...] = _ffn_tail(x, f_ref, p_ref[...], w_down_ref, g_post_ref, g_ple_ref, w_ple_ref, w_pg_ref)


def _ffn_sample(x, p, st, wts, seq_len, layer):
    NB = SAMPLE_SEQS_FFN
    R = NB * SEQ_PAD
    nseq = st.shape[1]
    weights = (wts["g_ffn_pre"], wts["w_up"], wts["w_conv_ffn"], wts["b_conv_ffn"], wts["w_down"],
               wts["g_ffn_post"], wts["g_ple"], wts["w_ple"], wts["w_ple_gate"])
    st_spec = pl.BlockSpec((NB, FFN_TAIL, 2 * D_FF), lambda i: (i, 0, 0))
    st_in = pl.BlockSpec((None, NB, FFN_TAIL, 2 * D_FF), lambda i: (layer, i, 0, 0))
    return pl.pallas_call(
        functools.partial(_ffn_sample_kernel, seq_len=seq_len),
        grid=(nseq // NB,),
        in_specs=[pl.BlockSpec((R, D_MODEL), lambda i: (i, 0)), pl.BlockSpec((R, PLE_DIM), lambda i: (i, 0)),
                  st_in] + [_full(w.shape) for w in weights],
        out_specs=(pl.BlockSpec((R, D_MODEL), lambda i: (i, 0)), st_spec),
        out_shape=(jax.ShapeDtypeStruct(x.shape, F32), jax.ShapeDtypeStruct(st.shape[1:], F32)),
        scratch_shapes=[
            pltpu.VMEM((2, NB, 2 * SEQ_PAD, FFN_CHUNK), F32),
            pltpu.VMEM((R, D_FF), BF16),
        ],
        compiler_params=pltpu.CompilerParams(
            dimension_semantics=("arbitrary",), vmem_limit_bytes=VMEM_LIMIT),
        name="ffn_sample",
    )(x, p, st, *weights)


def _pad_seq(a):
    nseq, seq_len, width = a.shape
    return jnp.pad(a, ((0, 0), (0, SEQ_PAD - seq_len), (0, 0))).reshape(nseq * SEQ_PAD, width)


def _layer_weights(l, g_mix_pre, w_in, b_igate, b_fgate, w_conv_mix, b_conv_mix, g_conv_norm,
                   b_conv_norm, g_mlstm_norm, w_out, g_mix_post, g_ffn_pre, w_up, w_conv_ffn,
                   b_conv_ffn, w_down, g_ffn_post, g_ple, w_ple, w_ple_gate):
    row = lambda v: v[l][None, :].astype(F32)
    n_gate = 2 * HEADS
    w_gate = jnp.pad(w_in[l][:, MAIN_COLS:], ((0, 0), (0, LANES - n_gate))).astype(BF16)
    b_gate = jnp.pad(jnp.concatenate([b_igate[l], b_fgate[l]]), (0, LANES - n_gate))[None, :].astype(F32)
    return {
        "g_mix_pre": row(g_mix_pre), "w_in": w_in[l][:, :MAIN_COLS].astype(BF16),
        "w_gate": w_gate, "b_gate": b_gate,
        "w_conv_mix": w_conv_mix[l].astype(F32), "b_conv_mix": row(b_conv_mix),
        "g_conv_norm": row(g_conv_norm), "b_conv_norm": row(b_conv_norm),
        "g_mlstm_norm": row(g_mlstm_norm), "w_out": w_out[l].astype(BF16), "g_mix_post": row(g_mix_post),
        "g_ffn_pre": row(g_ffn_pre), "w_up": w_up[l].astype(BF16), "w_conv_ffn": w_conv_ffn[l].astype(F32),
        "b_conv_ffn": row(b_conv_ffn), "w_down": w_down[l].astype(BF16), "g_ffn_post": row(g_ffn_post),
        "g_ple": row(g_ple), "w_ple": w_ple[l].astype(BF16), "w_ple_gate": w_ple_gate[l].astype(BF16),
    }


def kernel(x_prompt, x_sample, p_prompt, p_sample, state_conv_mix, state_mlstm_C, state_mlstm_n, state_mlstm_m, state_conv_ffn, g_mix_pre, w_in, b_igate, b_fgate, w_conv_mix, b_conv_mix, g_conv_norm, b_conv_norm, g_mlstm_norm, w_out, g_mix_post, g_ffn_pre, w_up, w_conv_ffn, b_conv_ffn, w_down, g_ffn_post, g_ple, w_ple, w_ple_gate):
    depth = w_in.shape[0]
    nseq, seq_len, _ = x_sample.shape
    assert FFN_TAIL <= seq_len <= SEQ_PAD and nseq % SAMPLE_SEQS == 0 and nseq % SAMPLE_SEQS_FFN == 0
    assert x_prompt.shape[1] % PROMPT_TILE == 0
    xp = x_prompt
    xs = _pad_seq(x_sample)
    pc, pC, pn, pm, pf = [], [], [], [], []
    sc, sn, sm, sf = [], [], [], []
    c_stack = None
    for l in range(depth):
        wts = _layer_weights(l, g_mix_pre, w_in, b_igate, b_fgate, w_conv_mix, b_conv_mix, g_conv_norm,
                             b_conv_norm, g_mlstm_norm, w_out, g_mix_post, g_ffn_pre, w_up, w_conv_ffn,
                             b_conv_ffn, w_down, g_ffn_post, g_ple, w_ple, w_ple_gate)
        xp, c1, C1, n1, m1, f1 = _layer_prompt(xp, p_prompt, wts, l)
        pc.append(c1); pC.append(C1); pn.append(n1); pm.append(m1[:, :HEADS, 0]); pf.append(f1)

        mrow = jnp.pad(jnp.repeat(state_mlstm_m[l].astype(F32), SEQ_PAD, axis=0),
                       ((0, 0), (HEADS, LANES - 2 * HEADS)))
        xs, c2, c_stack, n2, m2 = _mixer_sample(xs, state_conv_mix, state_mlstm_C, state_mlstm_n, mrow,
                                                wts, seq_len, l, c_stack)
        xs, f2 = _ffn_sample(xs, _pad_seq(p_sample[l]), state_conv_ffn, wts, seq_len, l)
        sc.append(c2); sn.append(n2); sf.append(f2)
        sm.append(m2.reshape(nseq, SEQ_PAD, LANES)[:, 0, :HEADS])
    ys = xs.reshape(nseq, SEQ_PAD, D_MODEL)[:, :seq_len]
    return (xp, ys, jnp.stack(pc), jnp.stack(pC), jnp.stack(pn), jnp.stack(pm), jnp.stack(pf),
            jnp.stack(sc), c_stack, jnp.stack(sn), jnp.stack(sm), jnp.stack(sf))
```

```python
import functools
import math

import jax
import jax.numpy as jnp
from jax import lax
from jax.experimental import pallas as pl
from jax.experimental.pallas import tpu as pltpu

F32 = jnp.float32
BF16 = jnp.bfloat16

D_MODEL = 1024
CONV_CH = 512
CONV_WIDTH = 31
CONV_TAIL = CONV_WIDTH - 1
CONV_GROUPS = 4
HEADS = 4
HEAD_DIM = 128
MLSTM_WIDTH = HEADS * HEAD_DIM
D_FF = 2816
FFN_TAIL = 2
PLE_DIM = 256
EPS = 1e-6
MAIN_COLS = 2 * CONV_CH + 4 * MLSTM_WIDTH
LANES = 128
SUBLANES = 8
NEG = -1e30

PROMPT_TILE = 256
SAMPLE_SEQS = 16
SAMPLE_SEQS_FFN = 32
SEQ_PAD = SUBLANES
CONV_ROWS = 64
CONV_IN_FLIGHT = 2
FFN_CHUNK = 256
SAMPLE_EXT_ROWS = -(-(CONV_TAIL + SEQ_PAD) // SUBLANES) * SUBLANES
VMEM_LIMIT = 56 * 1024 * 1024

GELU_A = -2.0 * math.sqrt(2.0 / math.pi)
GELU_B = GELU_A * 0.044715


def _dot(a, b):
    return jnp.dot(a, b, preferred_element_type=F32)


def _dot_exact(sel, x):
    hi = x.astype(BF16)
    r1 = x - hi.astype(F32)
    mid = r1.astype(BF16)
    lo = (r1 - mid.astype(F32)).astype(BF16)
    y = _dot(jnp.where(sel, 1.0, 0.0).astype(BF16), jnp.concatenate([hi, mid, lo], axis=1))
    return y[:, 0:LANES] + y[:, LANES:2 * LANES] + y[:, 2 * LANES:3 * LANES]


def _rms(x, g):
    ms = jnp.mean(x * x, axis=-1, keepdims=True)
    return x * lax.rsqrt(ms + EPS) * g


def _layernorm(x):
    mu = jnp.mean(x, axis=-1, keepdims=True)
    xc = x - mu
    var = jnp.mean(xc * xc, axis=-1, keepdims=True)
    return xc * lax.rsqrt(var + EPS)


def _sigmoid(x):
    return 1.0 / (1.0 + jnp.exp(-x))


def _log_sigmoid(x):
    return jnp.minimum(x, 0.0) - jnp.log(1.0 + jnp.exp(-jnp.abs(x)))


def _exact_zero(v):
    bits = pltpu.bitcast(v, jnp.uint32)
    bits = lax.shift_right_logical(lax.shift_right_logical(bits, jnp.uint32(16)), jnp.uint32(16))
    return pltpu.bitcast(bits, F32)


def _gelu_tanh(x):
    return x / (1.0 + jnp.exp(x * (GELU_A + GELU_B * (x * x))))


def _conv_branch_post(acc, g_ref, b_ref):
    parts = []
    for g in range(CONV_GROUPS):
        sl = slice(g * LANES, (g + 1) * LANES)
        y = _layernorm(acc[:, sl]) * g_ref[:, sl] + b_ref[:, sl]
        parts.append(y * _sigmoid(y))
    return parts


def _qk(q_bf, k_bf):
    return lax.dot_general(q_bf, k_bf, (((1,), (1,)), ((), ())), preferred_element_type=F32)


def _mlstm_weighted(scores, v_bf, d, inter):
    m_t = jnp.maximum(inter, jnp.max(d, axis=1, keepdims=True))
    w_intra = jnp.exp(d - m_t)
    w_inter = jnp.exp(inter - m_t)
    s = scores * w_intra
    num = _dot(s.astype(BF16), v_bf)
    den = jnp.sum(s, axis=1, keepdims=True)
    return m_t, w_inter, num, den


def _head_out(num, den, m_t, g_mn, zo):
    hh = num / jnp.maximum(jnp.abs(den), jnp.exp(-m_t))
    return _layernorm(hh) * g_mn * _sigmoid(zo)


def _ffn_tail(x, f_ref, p, w_down_ref, g_post_ref, g_ple_ref, w_ple_ref, w_pg_ref):
    x2 = x + _rms(_dot(f_ref[...], w_down_ref[...]), g_post_ref[...])
    emb = _dot(p.astype(BF16), w_ple_ref[...])
    gate = _sigmoid(_dot(_rms(x2, g_ple_ref[...]).astype(BF16), w_pg_ref[...]))
    return x2 + emb * gate


def _full(shape):
    n = len(shape)
    return pl.BlockSpec(shape, lambda *_: (0,) * n, pipeline_mode=pl.Buffered(1))


def _layer_prompt_kernel(x_ref, p_ref,
                         g_pre_ref, w_in_ref, w_gate_ref, b_gate_ref, w_cm_ref, b_cm_ref,
                         g_cn_ref, b_cn_ref, g_mn_ref, w_out_ref, g_post_ref,
                         g_fpre_ref, w_up_ref, w_cf_ref, b_cf_ref, w_down_ref, g_fpost_ref,
                         g_ple_ref, w_ple_ref, w_pg_ref,
                         y_ref, conv_out_ref, c_out_ref, n_out_ref, m_out_ref, tail_out_ref,
                         cbuf_ref, mixw_ref, mix_ref, cn_ref, m_ref, h_ref, z_ref, x1_ref, hist_ref, fw_ref, f_ref,
                         *, tiles_per_seq):
    T = PROMPT_TILE
    G = T // SUBLANES
    i = pl.program_id(0)
    n_tiles = pl.num_programs(0) - 1
    s_mix = lax.rem(jnp.minimum(i, n_tiles - 1), tiles_per_seq)
    s_ffn = lax.rem(jnp.maximum(i - 1, 0), tiles_per_seq)
    slot = lax.rem(i, 2)

    @pl.when(i == 0)
    def _():
        x1_ref[...] = jnp.zeros(x1_ref.shape, F32)

    @pl.when(s_mix == 0)
    def _():
        cbuf_ref[T:2 * T, :] = jnp.zeros((T, CONV_CH), F32)
        cn_ref[...] = jnp.zeros(cn_ref.shape, F32)
        m_ref[...] = jnp.zeros(m_ref.shape, F32)

    @pl.when(s_ffn == 0)
    def _():
        hist_ref[...] = jnp.zeros(hist_ref.shape, F32)

    wr = lax.broadcasted_iota(jnp.int32, (T, T), 0)
    wc = lax.broadcasted_iota(jnp.int32, (T, T), 1)
    to_work = jnp.where(wc == (wr % SUBLANES) * G + wr // SUBLANES, 1.0, 0.0).astype(BF16)
    to_token = jnp.where(wr == (wc % SUBLANES) * G + wc // SUBLANES, 1.0, 0.0).astype(BF16)

    def moved_down(group, prev_group):
        first = lax.broadcasted_iota(jnp.int32, group.shape, 0) == 0
        return jnp.where(first, pltpu.roll(prev_group, 1, 0), pltpu.roll(group, 1, 0))

    x = x_ref[...]
    h_ref[...] = _rms(x, g_pre_ref[...]).astype(BF16)
    hw = _dot(to_work, h_ref[...]).astype(BF16)
    zv = _dot(hw, w_in_ref[:, 0:CONV_CH])
    zg = _dot(hw, w_in_ref[:, CONV_CH:2 * CONV_CH])
    gates = _dot(h_ref[...], w_gate_ref[...]) + b_gate_ref[...]
    q_off = 2 * CONV_CH
    for part in range(4):
        ps = slice(part * MLSTM_WIDTH, (part + 1) * MLSTM_WIDTH)
        z_ref[slot, :, ps] = _dot(h_ref[...],
                                  w_in_ref[:, q_off + part * MLSTM_WIDTH:q_off + (part + 1) * MLSTM_WIDTH])

    a = zv * _sigmoid(zg)
    a_old = cbuf_ref[T:2 * T, :]
    cbuf_ref[0:T, :] = jnp.concatenate(
        [moved_down(a[g * SUBLANES:(g + 1) * SUBLANES, :], a_old[g * SUBLANES:(g + 1) * SUBLANES, :])
         for g in range(G)], axis=0)
    cbuf_ref[T:2 * T, :] = a
    recent = []
    for g in range(CONV_GROUPS):
        cs = slice(g * LANES, (g + 1) * LANES)
        for rb in range(T // CONV_ROWS):
            acc = jnp.broadcast_to(b_cm_ref[:, cs], (CONV_ROWS, LANES))
            for j in range(CONV_WIDTH):
                base = T - SUBLANES * (CONV_TAIL - j) + rb * CONV_ROWS
                w_row = w_cm_ref[j:j + 1, cs]
                if j == 0 and len(recent) == CONV_IN_FLIGHT:
                    w_row = w_row + _exact_zero(recent.pop(0))
                acc = acc + w_row * cbuf_ref[base:base + CONV_ROWS, cs]
            y = _layernorm(acc) * g_cn_ref[:, cs] + b_cn_ref[:, cs]
            mixw_ref[rb * CONV_ROWS:(rb + 1) * CONV_ROWS, cs] = (y * _sigmoid(y)).astype(BF16)
            recent.append(acc[0:1, :])

    x1 = x1_ref[1 - slot]
    h2 = _rms(x1, g_fpre_ref[...]).astype(BF16)
    hp = _dot(to_work, h2).astype(BF16)
    for c in range(D_FF // FFN_CHUNK):
        halves = []
        for half in range(2):
            c0 = half * D_FF + c * FFN_CHUNK
            cs = slice(c0, c0 + FFN_CHUNK)
            u = _dot(hp, w_up_ref[:, cs])
            prev = hist_ref[:, cs]
            hist_ref[:, cs] = u[T - 2 * SUBLANES:T, :]
            w_last = moved_down(u[T - SUBLANES:T, :], prev[SUBLANES:2 * SUBLANES, :])
            w_last2 = moved_down(u[T - 2 * SUBLANES:T - SUBLANES, :], prev[0:SUBLANES, :])
            u1 = jnp.concatenate([w_last, u[0:T - SUBLANES, :]], axis=0)
            u2 = jnp.concatenate([w_last2, w_last, u[0:T - 2 * SUBLANES, :]], axis=0)
            halves.append(w_cf_ref[0:1, cs] * u2 + w_cf_ref[1:2, cs] * u1
                          + w_cf_ref[2:3, cs] * u + b_cf_ref[:, cs])
        fw_ref[:, c * FFN_CHUNK:(c + 1) * FFN_CHUNK] = (_gelu_tanh(halves[0]) * halves[1]).astype(BF16)

    causal = wc <= wr
    bcum = _dot_exact(causal, _log_sigmoid(gates))
    gates_t = gates.T
    bcum_t = bcum.T
    heads = []
    for hd in range(HEADS):
        c0 = hd * HEAD_DIM
        zvv = z_ref[slot, :, 2 * MLSTM_WIDTH + c0:2 * MLSTM_WIDTH + c0 + HEAD_DIM]
        q_bf = (z_ref[slot, :, c0:c0 + HEAD_DIM] * (HEAD_DIM ** -0.5)).astype(BF16)
        k_bf = z_ref[slot, :, MLSTM_WIDTH + c0:MLSTM_WIDTH + c0 + HEAD_DIM].astype(BF16)
        cn = cn_ref[hd]
        heads.append((zvv, k_bf, cn, _qk(q_bf, k_bf), _dot(q_bf, cn.astype(BF16))))

    finished = []
    for hd in range(HEADS):
        zvv, k_bf, cn, scores, carried = heads[hd]
        i_row = gates_t[hd:hd + 1, :]
        i_col = gates[:, hd:hd + 1]
        b_row = bcum_t[HEADS + hd:HEADS + hd + 1, :]
        b_col = bcum[:, HEADS + hd:HEADS + hd + 1]
        m_prev = m_ref[hd:hd + 1, 0:1]
        d = jnp.where(causal, b_col - b_row + i_row, NEG)
        m_t, w_inter, num, den = _mlstm_weighted(scores, zvv.astype(BF16), d, b_col + m_prev)
        m_new = m_t[T - 1:T, :]
        ws = jnp.exp(b_col[T - 1:T, :] - b_col + i_col - m_new)
        vp = jnp.concatenate([ws * zvv, jnp.broadcast_to(ws, (T, HEAD_DIM))], axis=1).astype(BF16)
        kv = lax.dot_general(k_bf, vp, (((0,), (0,)), ((), ())), preferred_element_type=F32)
        finished.append((m_t, w_inter, num, den, m_new, w_inter[T - 1:T, :], kv))

    for hd in range(HEADS):
        c0 = hd * HEAD_DIM
        _, _, cn, _, carried = heads[hd]
        m_t, w_inter, num, den, m_new, decay, kv = finished[hd]
        zo = z_ref[slot, :, 3 * MLSTM_WIDTH + c0:3 * MLSTM_WIDTH + c0 + HEAD_DIM]
        num = num + carried[:, 0:HEAD_DIM] * w_inter
        den = den + carried[:, HEAD_DIM:HEAD_DIM + 1] * w_inter
        mix_ref[:, CONV_CH + c0:CONV_CH + c0 + HEAD_DIM] = _head_out(
            num, den, m_t, g_mn_ref[:, c0:c0 + HEAD_DIM], zo).astype(BF16)
        cn_ref[hd] = decay * cn + kv
        m_ref[hd:hd + 1, :] = jnp.broadcast_to(m_new, (1, LANES))

    for c in range(D_FF // FFN_CHUNK):
        fs = slice(c * FFN_CHUNK, (c + 1) * FFN_CHUNK)
        f_ref[:, fs] = _dot(to_token, fw_ref[:, fs]).astype(BF16)
    y_ref[...] = _ffn_tail(x1, f_ref, p_ref[...], w_down_ref, g_fpost_ref, g_ple_ref, w_ple_ref, w_pg_ref)

    mix_ref[:, 0:CONV_CH] = _dot(to_token, mixw_ref[...]).astype(BF16)
    x1_ref[slot] = x + _rms(_dot(mix_ref[...], w_out_ref[...]), g_post_ref[...])

    @pl.when((s_mix == tiles_per_seq - 1) & (i < n_tiles))
    def _():
        for j in range(CONV_TAIL):
            r = T + SUBLANES * (G - CONV_TAIL + j) + SUBLANES - 1
            conv_out_ref[j:j + 1, :] = cbuf_ref[r:r + 1, :]
        m_out_ref[...] = m_ref[...]
        for hd in range(HEADS):
            cn = cn_ref[hd]
            c_out_ref[hd] = cn[:, 0:HEAD_DIM]
            n_out_ref[hd:hd + 1, :] = cn[:, HEAD_DIM:].T[0:1, :]

    @pl.when((s_ffn == tiles_per_seq - 1) & (i >= 1))
    def _():
        tail_out_ref[0:1, :] = hist_ref[SUBLANES - 1:SUBLANES, :]
        tail_out_ref[1:2, :] = hist_ref[2 * SUBLANES - 1:2 * SUBLANES, :]


def _layer_prompt(x, p, wts, layer):
    B, S, D = x.shape
    T = PROMPT_TILE
    tps = S // T
    n_tiles = B * tps
    weights = (wts["g_mix_pre"], wts["w_in"], wts["w_gate"], wts["b_gate"], wts["w_conv_mix"],
               wts["b_conv_mix"], wts["g_conv_norm"], wts["b_conv_norm"], wts["g_mlstm_norm"],
               wts["w_out"], wts["g_mix_post"],
               wts["g_ffn_pre"], wts["w_up"], wts["w_conv_ffn"], wts["b_conv_ffn"], wts["w_down"],
               wts["g_ffn_post"], wts["g_ple"], wts["w_ple"], wts["w_ple_gate"])
    mix_tile = lambda i: jnp.minimum(i, n_tiles - 1)
    ffn_tile = lambda i: jnp.maximum(i - 1, 0)
    out_shape = (
        jax.ShapeDtypeStruct((B, S, D), F32),
        jax.ShapeDtypeStruct((B, CONV_TAIL, CONV_CH), F32),
        jax.ShapeDtypeStruct((B, HEADS, HEAD_DIM, HEAD_DIM), F32),
        jax.ShapeDtypeStruct((B, HEADS, HEAD_DIM), F32),
        jax.ShapeDtypeStruct((B, SUBLANES, LANES), F32),
        jax.ShapeDtypeStruct((B, FFN_TAIL, 2 * D_FF), F32),
    )
    out_specs = (
        pl.BlockSpec((None, T, D), lambda i: (ffn_tile(i) // tps, ffn_tile(i) % tps, 0)),
        pl.BlockSpec((None, CONV_TAIL, CONV_CH), lambda i: (mix_tile(i) // tps, 0, 0)),
        pl.BlockSpec((None, HEADS, HEAD_DIM, HEAD_DIM), lambda i: (mix_tile(i) // tps, 0, 0, 0)),
        pl.BlockSpec((None, HEADS, HEAD_DIM), lambda i: (mix_tile(i) // tps, 0, 0)),
        pl.BlockSpec((None, SUBLANES, LANES), lambda i: (mix_tile(i) // tps, 0, 0)),
        pl.BlockSpec((None, FFN_TAIL, 2 * D_FF), lambda i: (ffn_tile(i) // tps, 0, 0)),
    )
    return pl.pallas_call(
        functools.partial(_layer_prompt_kernel, tiles_per_seq=tps),
        grid=(n_tiles + 1,),
        in_specs=[pl.BlockSpec((None, T, D), lambda i: (mix_tile(i) // tps, mix_tile(i) % tps, 0)),
                  pl.BlockSpec((None, None, T, PLE_DIM),
                               lambda i: (layer, ffn_tile(i) // tps, ffn_tile(i) % tps, 0))]
                 + [_full(w.shape) for w in weights],
        out_specs=out_specs,
        out_shape=out_shape,
        scratch_shapes=[
            pltpu.VMEM((2 * T, CONV_CH), F32),
            pltpu.VMEM((T, CONV_CH), BF16),
            pltpu.VMEM((T, D_MODEL), BF16),
            pltpu.VMEM((HEADS, HEAD_DIM, 2 * HEAD_DIM), F32),
            pltpu.VMEM((SUBLANES, LANES), F32),
            pltpu.VMEM((T, D_MODEL), BF16),
            pltpu.VMEM((2, T, 4 * MLSTM_WIDTH), F32),
            pltpu.VMEM((2, T, D_MODEL), F32),
            pltpu.VMEM((2 * SUBLANES, 2 * D_FF), F32),
            pltpu.VMEM((T, D_FF), BF16),
            pltpu.VMEM((T, D_FF), BF16),
        ],
        compiler_params=pltpu.CompilerParams(
            dimension_semantics=("arbitrary",), vmem_limit_bytes=VMEM_LIMIT),
        name="layer_prompt",
    )(x, p, *weights)


def _mixer_sample_kernel(x_ref, st_ref, c_ref, n_ref, mrow_ref,
                         g_pre_ref, w_in_ref, w_gate_ref, b_gate_ref, w_conv_ref, b_conv_ref,
                         g_cn_ref, b_cn_ref, g_mn_ref, w_out_ref, g_post_ref,
                         y_ref, conv_out_ref, c_out_ref, n_out_ref, m_out_ref,
                         ext_ref, q_ref, kt_ref, wv_ref, wk_ref, carried_ref, qn_ref, dec_ref,
                         *, seq_len, stacked_first):
    NB = SAMPLE_SEQS
    R = NB * SEQ_PAD
    if stacked_first:
        c_dst = c_out_ref.at[0]
        c_out_ref[1:] = jnp.zeros((c_out_ref.shape[0] - 1,) + tuple(c_out_ref.shape[1:]), F32)
    else:
        c_dst = c_out_ref
    x = x_ref[...]
    h = _rms(x, g_pre_ref[...]).astype(BF16)

    zv = _dot(h, w_in_ref[:, 0:CONV_CH])
    zg = _dot(h, w_in_ref[:, CONV_CH:2 * CONV_CH])
    a = zv * _sigmoid(zg)
    ext_ref[:, 0:CONV_TAIL, :] = st_ref[...]
    ext_ref[:, CONV_TAIL:CONV_TAIL + SEQ_PAD, :] = a.reshape(NB, SEQ_PAD, CONV_CH)
    ext_ref[:, CONV_TAIL + SEQ_PAD:, :] = jnp.zeros((NB, SAMPLE_EXT_ROWS - CONV_TAIL - SEQ_PAD, CONV_CH), F32)
    acc = jnp.broadcast_to(b_conv_ref[...][None], (NB, SEQ_PAD, CONV_CH))
    for j in range(CONV_WIDTH):
        acc = acc + w_conv_ref[j:j + 1, :][None] * ext_ref[:, pl.ds(j, SEQ_PAD), :]
    conv_out_ref[...] = ext_ref[:, seq_len:seq_len + CONV_TAIL, :]
    mix_parts = _conv_branch_post(acc.reshape(R, CONV_CH), g_cn_ref, b_cn_ref)

    gates = _dot(h, w_gate_ref[...]) + b_gate_ref[...]
    row = lax.broadcasted_iota(jnp.int32, (R, R), 0)
    col = lax.broadcasted_iota(jnp.int32, (R, R), 1)
    same_seq = (row // SEQ_PAD) == (col // SEQ_PAD)
    causal = same_seq & (col <= row)
    bcum = _dot_exact(causal, _log_sigmoid(gates))
    mask = causal & ((col % SEQ_PAD) < seq_len)
    pick_last = same_seq & ((col % SEQ_PAD) == seq_len - 1)
    inter_all = bcum + mrow_ref[...]
    gates_t = gates.T
    bcum_t = bcum.T
    lane = lax.broadcasted_iota(jnp.int32, (R, LANES), 1)
    row_valid = (lax.broadcasted_iota(jnp.int32, (R, 1), 0) % SEQ_PAD) < seq_len
    stats = jnp.where((lane >= HEADS) & (lane < 2 * HEADS), bcum, 0.0)
    q_off = 2 * CONV_CH
    z_parts = [_dot(h, w_in_ref[:, q_off + part * MLSTM_WIDTH:q_off + (part + 1) * MLSTM_WIDTH])
               for part in range(4)]
    saved = []
    for hd in range(HEADS):
        c0 = hd * HEAD_DIM
        zq = z_parts[0][:, c0:c0 + HEAD_DIM] * (HEAD_DIM ** -0.5)
        zk = z_parts[1][:, c0:c0 + HEAD_DIM]
        zvv = z_parts[2][:, c0:c0 + HEAD_DIM]
        zo = z_parts[3][:, c0:c0 + HEAD_DIM]
        i_row = gates_t[hd:hd + 1, :]
        b_row = bcum_t[HEADS + hd:HEADS + hd + 1, :]
        b_col = bcum[:, HEADS + hd:HEADS + hd + 1]
        d = jnp.where(mask, b_col - b_row + i_row, NEG)
        inter = inter_all[:, HEADS + hd:HEADS + hd + 1]
        m_t, w_inter, num, den = _mlstm_weighted(_qk(zq.astype(BF16), zk.astype(BF16)), zvv.astype(BF16), d, inter)
        stats = jnp.where(lane == hd, m_t, stats)
        stats = jnp.where(lane == 2 * HEADS + hd, w_inter, stats)
        q_ref[hd, 0:R, :] = zq
        q_ref[hd, R:R + SEQ_PAD, :] = jnp.zeros((SEQ_PAD, HEAD_DIM), F32)
        kt_ref[hd] = zk.T
        saved.append((m_t, w_inter, num, den, zo, zk, zvv))

    per_seq = _dot_exact(pick_last, stats)
    m_out_ref[...] = per_seq
    for hd in range(HEADS):
        zk, zvv = saved[hd][5], saved[hd][6]
        m_new = per_seq[:, hd:hd + 1]
        b_last = per_seq[:, HEADS + hd:HEADS + hd + 1]
        decay = per_seq[:, 2 * HEADS + hd:2 * HEADS + hd + 1]
        b_col = bcum[:, HEADS + hd:HEADS + hd + 1]
        i_col = gates[:, hd:hd + 1]
        ws = jnp.where(row_valid, jnp.exp(b_last - b_col + i_col - m_new), 0.0)
        wv_ref[hd] = (ws * zvv).astype(BF16)
        wk_ref[hd] = ws * zk
        dec_ref[hd] = jnp.broadcast_to(decay, (R, LANES))

    col_seq = lax.broadcasted_iota(jnp.int32, (HEAD_DIM, R), 1) // SEQ_PAD

    def per_sequence(b, carry):
        r0 = pl.multiple_of(b * SEQ_PAD, SEQ_PAD)
        for hd in range(HEADS):
            c_old = c_ref[b, hd]
            n_old = n_ref[b, hd:hd + 1, :]
            q2 = q_ref[hd, pl.ds(r0, 2 * SEQ_PAD), :]
            carried_ref[hd, pl.ds(r0, SEQ_PAD), :] = _dot(q2.astype(BF16), c_old.astype(BF16))[0:SEQ_PAD, :]
            qn = jnp.sum(q2[0:SEQ_PAD, :] * n_old, axis=1, keepdims=True)
            qn_ref[hd, pl.ds(r0, SEQ_PAD), :] = jnp.broadcast_to(qn, (SEQ_PAD, LANES))
            dec = dec_ref[hd, pl.ds(r0, 1), :]
            kt_b = jnp.where(col_seq == b, kt_ref[hd], 0.0).astype(BF16)
            c_dst[b, hd] = dec * c_old + _dot(kt_b, wv_ref[hd])
            n_out_ref[b, hd:hd + 1, :] = dec * n_old + jnp.sum(wk_ref[hd, pl.ds(r0, SEQ_PAD), :], axis=0,
                                                                keepdims=True)
        return carry

    lax.fori_loop(0, NB, per_sequence, 0, unroll=4)

    for hd in range(HEADS):
        c0 = hd * HEAD_DIM
        m_t, w_inter, num, den, zo = saved[hd][:5]
        num = num + carried_ref[hd] * w_inter
        den = den + qn_ref[hd][:, 0:1] * w_inter
        mix_parts.append(_head_out(num, den, m_t, g_mn_ref[:, c0:c0 + HEAD_DIM], zo))

    mix = jnp.concatenate(mix_parts, axis=1).astype(BF16)
    y_ref[...] = x + _rms(_dot(mix, w_out_ref[...]), g_post_ref[...])


N_MIXER_SAMPLE_INPUTS = 16


def _mixer_sample_kernel_inplace(*refs, seq_len):
    _mixer_sample_kernel(*refs[:N_MIXER_SAMPLE_INPUTS], *refs[N_MIXER_SAMPLE_INPUTS + 1:],
                         seq_len=seq_len, stacked_first=False)


def _mixer_sample(x, st, c, n, mrow, wts, seq_len, layer, c_stack):
    NB = SAMPLE_SEQS
    R = NB * SEQ_PAD
    depth, nseq = c.shape[0], c.shape[1]
    weights = (wts["g_mix_pre"], wts["w_in"], wts["w_gate"], wts["b_gate"], wts["w_conv_mix"],
               wts["b_conv_mix"], wts["g_conv_norm"], wts["b_conv_norm"], wts["g_mlstm_norm"],
               wts["w_out"], wts["g_mix_post"])
    rows = lambda width: pl.BlockSpec((R, width), lambda i: (i, 0))
    st_spec = pl.BlockSpec((NB, CONV_TAIL, CONV_CH), lambda i: (i, 0, 0))
    n_spec = pl.BlockSpec((NB, HEADS, HEAD_DIM), lambda i: (i, 0, 0))
    st_in = pl.BlockSpec((None, NB, CONV_TAIL, CONV_CH), lambda i: (layer, i, 0, 0))
    c_in = pl.BlockSpec((None, NB, HEADS, HEAD_DIM, HEAD_DIM), lambda i: (layer, i, 0, 0, 0))
    n_in = pl.BlockSpec((None, NB, HEADS, HEAD_DIM), lambda i: (layer, i, 0, 0))
    in_specs = [rows(D_MODEL), st_in, c_in, n_in, rows(LANES)] + [_full(w.shape) for w in weights]
    operands = (x, st, c, n, mrow, *weights)
    assert len(operands) == N_MIXER_SAMPLE_INPUTS
    if c_stack is None:
        body = functools.partial(_mixer_sample_kernel, seq_len=seq_len, stacked_first=True)
        c_out_spec = pl.BlockSpec((depth, NB, HEADS, HEAD_DIM, HEAD_DIM), lambda i: (0, i, 0, 0, 0))
        aliases = {}
    else:
        body = functools.partial(_mixer_sample_kernel_inplace, seq_len=seq_len)
        c_out_spec = c_in
        in_specs.append(pl.BlockSpec(memory_space=pl.ANY))
        operands = operands + (c_stack,)
        aliases = {N_MIXER_SAMPLE_INPUTS: 2}
    return pl.pallas_call(
        body,
        grid=(nseq // NB,),
        in_specs=in_specs,
        out_specs=(rows(D_MODEL), st_spec, c_out_spec, n_spec, rows(LANES)),
        out_shape=(jax.ShapeDtypeStruct(x.shape, F32), jax.ShapeDtypeStruct(st.shape[1:], F32),
                   jax.ShapeDtypeStruct(c.shape, F32), jax.ShapeDtypeStruct(n.shape[1:], F32),
                   jax.ShapeDtypeStruct(mrow.shape, F32)),
        input_output_aliases=aliases,
        scratch_shapes=[
            pltpu.VMEM((NB, SAMPLE_EXT_ROWS, CONV_CH), F32),
            pltpu.VMEM((HEADS, R + SEQ_PAD, HEAD_DIM), F32),
            pltpu.VMEM((HEADS, HEAD_DIM, R), F32),
            pltpu.VMEM((HEADS, R, HEAD_DIM), BF16),
            pltpu.VMEM((HEADS, R, HEAD_DIM), F32),
            pltpu.VMEM((HEADS, R, HEAD_DIM), F32),
            pltpu.VMEM((HEADS, R, LANES), F32),
            pltpu.VMEM((HEADS, R, LANES), F32),
        ],
        compiler_params=pltpu.CompilerParams(
            dimension_semantics=("arbitrary",), vmem_limit_bytes=VMEM_LIMIT),
        name="mixer_sample",
    )(*operands)


def _ffn_sample_kernel(x_ref, p_ref, st_ref, g_pre_ref, w_up_ref, w_conv_ref, b_conv_ref, w_down_ref,
                       g_post_ref, g_ple_ref, w_ple_ref, w_pg_ref,
                       y_ref, tail_out_ref,
                       ubuf_ref, f_ref, *, seq_len):
    NB = SAMPLE_SEQS_FFN
    R = NB * SEQ_PAD
    x = x_ref[...]
    h = _rms(x, g_pre_ref[...]).astype(BF16)
    lo = SEQ_PAD - FFN_TAIL
    for c in range(D_FF // FFN_CHUNK):
        halves = []
        for half in range(2):
            c0 = half * D_FF + c * FFN_CHUNK
            cs = slice(c0, c0 + FFN_CHUNK)
            u = _dot(h, w_up_ref[:, cs])
            ubuf_ref[half, :, lo:SEQ_PAD, :] = st_ref[:, :, cs]
            ubuf_ref[half, :, SEQ_PAD:2 * SEQ_PAD, :] = u.reshape(NB, SEQ_PAD, FFN_CHUNK)
            tail_out_ref[:, :, cs] = ubuf_ref[half, :, lo + seq_len:SEQ_PAD + seq_len, :]
            y = (w_conv_ref[0:1, cs][None] * ubuf_ref[half, :, pl.ds(lo, SEQ_PAD), :]
                 + w_conv_ref[1:2, cs][None] * ubuf_ref[half, :, pl.ds(lo + 1, SEQ_PAD), :]
                 + w_conv_ref[2:3, cs][None] * ubuf_ref[half, :, pl.ds(lo + 2, SEQ_PAD), :]
                 + b_conv_ref[:, cs][None])
            halves.append(y.reshape(R, FFN_CHUNK))
        f_ref[:, c * FFN_CHUNK:(c + 1) * FFN_CHUNK] = (_gelu_tanh(halves[0]) * halves[1]).astype(BF16)
    y_ref[...] = _ffn_tail(x, f_ref, p_ref[...], w_down_ref, g_post_ref, g_ple_ref, w_ple_ref, w_pg_ref)


def _ffn_sample(x, p, st, wts, seq_len, layer):
    NB = SAMPLE_SEQS_FFN
    R = NB * SEQ_PAD
    nseq = st.shape[1]
    weights = (wts["g_ffn_pre"], wts["w_up"], wts["w_conv_ffn"], wts["b_conv_ffn"], wts["w_down"],
               wts["g_ffn_post"], wts["g_ple"], wts["w_ple"], wts["w_ple_gate"])
    st_spec = pl.BlockSpec((NB, FFN_TAIL, 2 * D_FF), lambda i: (i, 0, 0))
    st_in = pl.BlockSpec((None, NB, FFN_TAIL, 2 * D_FF), lambda i: (layer, i, 0, 0))
    return pl.pallas_call(
        functools.partial(_ffn_sample_kernel, seq_len=seq_len),
        grid=(nseq // NB,),
        in_specs=[pl.BlockSpec((R, D_MODEL), lambda i: (i, 0)), pl.BlockSpec((R, PLE_DIM), lambda i: (i, 0)),
                  st_in] + [_full(w.shape) for w in weights],
        out_specs=(pl.BlockSpec((R, D_MODEL), lambda i: (i, 0)), st_spec),
        out_shape=(jax.ShapeDtypeStruct(x.shape, F32), jax.ShapeDtypeStruct(st.shape[1:], F32)),
        scratch_shapes=[
            pltpu.VMEM((2, NB, 2 * SEQ_PAD, FFN_CHUNK), F32),
            pltpu.VMEM((R, D_FF), BF16),
        ],
        compiler_params=pltpu.CompilerParams(
            dimension_semantics=("arbitrary",), vmem_limit_bytes=VMEM_LIMIT),
        name="ffn_sample",
    )(x, p, st, *weights)


def _pad_seq(a):
    nseq, seq_len, width = a.shape
    return jnp.pad(a, ((0, 0), (0, SEQ_PAD - seq_len), (0, 0))).reshape(nseq * SEQ_PAD, width)


def _layer_weights(l, g_mix_pre, w_in, b_igate, b_fgate, w_conv_mix, b_conv_mix, g_conv_norm,
                   b_conv_norm, g_mlstm_norm, w_out, g_mix_post, g_ffn_pre, w_up, w_conv_ffn,
                   b_conv_ffn, w_down, g_ffn_post, g_ple, w_ple, w_ple_gate):
    row = lambda v: v[l][None, :].astype(F32)
    n_gate = 2 * HEADS
    w_gate = jnp.pad(w_in[l][:, MAIN_COLS:], ((0, 0), (0, LANES - n_gate))).astype(BF16)
    b_gate = jnp.pad(jnp.concatenate([b_igate[l], b_fgate[l]]), (0, LANES - n_gate))[None, :].astype(F32)
    return {
        "g_mix_pre": row(g_mix_pre), "w_in": w_in[l][:, :MAIN_COLS].astype(BF16),
        "w_gate": w_gate, "b_gate": b_gate,
        "w_conv_mix": w_conv_mix[l].astype(F32), "b_conv_mix": row(b_conv_mix),
        "g_conv_norm": row(g_conv_norm), "b_conv_norm": row(b_conv_norm),
        "g_mlstm_norm": row(g_mlstm_norm), "w_out": w_out[l].astype(BF16), "g_mix_post": row(g_mix_post),
        "g_ffn_pre": row(g_ffn_pre), "w_up": w_up[l].astype(BF16), "w_conv_ffn": w_conv_ffn[l].astype(F32),
        "b_conv_ffn": row(b_conv_ffn), "w_down": w_down[l].astype(BF16), "g_ffn_post": row(g_ffn_post),
        "g_ple": row(g_ple), "w_ple": w_ple[l].astype(BF16), "w_ple_gate": w_ple_gate[l].astype(BF16),
    }


def kernel(x_prompt, x_sample, p_prompt, p_sample, state_conv_mix, state_mlstm_C, state_mlstm_n, state_mlstm_m, state_conv_ffn, g_mix_pre, w_in, b_igate, b_fgate, w_conv_mix, b_conv_mix, g_conv_norm, b_conv_norm, g_mlstm_norm, w_out, g_mix_post, g_ffn_pre, w_up, w_conv_ffn, b_conv_ffn, w_down, g_ffn_post, g_ple, w_ple, w_ple_gate):
    depth = w_in.shape[0]
    nseq, seq_len, _ = x_sample.shape
    assert FFN_TAIL <= seq_len <= SEQ_PAD and nseq % SAMPLE_SEQS == 0 and nseq % SAMPLE_SEQS_FFN == 0
    assert x_prompt.shape[1] % PROMPT_TILE == 0 and CONV_TAIL <= PROMPT_TILE // SUBLANES
    xp = x_prompt
    xs = _pad_seq(x_sample)
    pc, pC, pn, pm, pf = [], [], [], [], []
    sc, sn, sm, sf = [], [], [], []
    c_stack = None
    for l in range(depth):
        wts = _layer_weights(l, g_mix_pre, w_in, b_igate, b_fgate, w_conv_mix, b_conv_mix, g_conv_norm,
                             b_conv_norm, g_mlstm_norm, w_out, g_mix_post, g_ffn_pre, w_up, w_conv_ffn,
                             b_conv_ffn, w_down, g_ffn_post, g_ple, w_ple, w_ple_gate)
        xp, c1, C1, n1, m1, f1 = _layer_prompt(xp, p_prompt, wts, l)
        pc.append(c1); pC.append(C1); pn.append(n1); pm.append(m1[:, :HEADS, 0]); pf.append(f1)

        mrow = jnp.pad(jnp.repeat(state_mlstm_m[l].astype(F32), SEQ_PAD, axis=0),
                       ((0, 0), (HEADS, LANES - 2 * HEADS)))
        xs, c2, c_stack, n2, m2 = _mixer_sample(xs, state_conv_mix, state_mlstm_C, state_mlstm_n, mrow,
                                                wts, seq_len, l, c_stack)
        xs, f2 = _ffn_sample(xs, _pad_seq(p_sample[l]), state_conv_ffn, wts, seq_len, l)
        sc.append(c2); sn.append(n2); sf.append(f2)
        sm.append(m2.reshape(nseq, SEQ_PAD, LANES)[:, 0, :HEADS])
    ys = xs.reshape(nseq, SEQ_PAD, D_MODEL)[:, :seq_len]
    return (xp, ys, jnp.stack(pc), jnp.stack(pC), jnp.stack(pn), jnp.stack(pm), jnp.stack(pf),
            jnp.stack(sc), c_stack, jnp.stack(sn), jnp.stack(sm), jnp.stack(sf))
```

```python
import functools

import jax
import jax.numpy as jnp
from jax import lax
from jax.experimental import pallas as pl
from jax.experimental.pallas import tpu as pltpu

F32 = jnp.float32
BF16 = jnp.bfloat16

D_MODEL = 1024
CONV_CH = 512
CONV_WIDTH = 31
CONV_TAIL = CONV_WIDTH - 1
CONV_GROUPS = 4
HEADS = 4
HEAD_DIM = 128
MLSTM_WIDTH = HEADS * HEAD_DIM
D_FF = 2816
FFN_TAIL = 2
PLE_DIM = 256
EPS = 1e-6
MAIN_COLS = 2 * CONV_CH + 4 * MLSTM_WIDTH
LANES = 128
SUBLANES = 8
NEG = -1e30

PROMPT_TILE = 256
SAMPLE_SEQS = 16
SAMPLE_SEQS_FFN = 32
SEQ_PAD = SUBLANES
HIST = 32
SHIFT_ROWS = PROMPT_TILE + HIST - SUBLANES
CONV_ROWS = 64
CONV_IN_FLIGHT = 2
FFN_CHUNK = 256
SAMPLE_EXT_ROWS = -(-(CONV_TAIL + SEQ_PAD) // SUBLANES) * SUBLANES
VMEM_LIMIT = 56 * 1024 * 1024


def _dot(a, b):
    return jnp.dot(a, b, preferred_element_type=F32)


def _dot_exact(sel, x):
    hi = x.astype(BF16)
    r1 = x - hi.astype(F32)
    mid = r1.astype(BF16)
    lo = (r1 - mid.astype(F32)).astype(BF16)
    y = _dot(jnp.where(sel, 1.0, 0.0).astype(BF16), jnp.concatenate([hi, mid, lo], axis=1))
    return y[:, 0:LANES] + y[:, LANES:2 * LANES] + y[:, 2 * LANES:3 * LANES]


def _rms(x, g):
    ms = jnp.mean(x * x, axis=-1, keepdims=True)
    return x * lax.rsqrt(ms + EPS) * g


def _layernorm(x):
    mu = jnp.mean(x, axis=-1, keepdims=True)
    xc = x - mu
    var = jnp.mean(xc * xc, axis=-1, keepdims=True)
    return xc * lax.rsqrt(var + EPS)


def _sigmoid(x):
    return 1.0 / (1.0 + jnp.exp(-x))


def _log_sigmoid(x):
    return jnp.minimum(x, 0.0) - jnp.log(1.0 + jnp.exp(-jnp.abs(x)))


def _exact_zero(v):
    bits = pltpu.bitcast(v, jnp.uint32)
    bits = lax.shift_right_logical(lax.shift_right_logical(bits, jnp.uint32(16)), jnp.uint32(16))
    return pltpu.bitcast(bits, F32)


def _gelu_tanh(x):
    return 0.5 * x * (1.0 + jnp.tanh(0.7978845608028654 * (x + 0.044715 * (x * x * x))))


def _conv_branch_post(acc, g_ref, b_ref):
    parts = []
    for g in range(CONV_GROUPS):
        sl = slice(g * LANES, (g + 1) * LANES)
        y = _layernorm(acc[:, sl]) * g_ref[:, sl] + b_ref[:, sl]
        parts.append(y * _sigmoid(y))
    return parts


def _qk(q_bf, k_bf):
    return lax.dot_general(q_bf, k_bf, (((1,), (1,)), ((), ())), preferred_element_type=F32)


def _mlstm_weighted(scores, v_bf, d, inter):
    m_t = jnp.maximum(inter, jnp.max(d, axis=1, keepdims=True))
    w_intra = jnp.exp(d - m_t)
    w_inter = jnp.exp(inter - m_t)
    s = scores * w_intra
    num = _dot(s.astype(BF16), v_bf)
    den = jnp.sum(s, axis=1, keepdims=True)
    return m_t, w_inter, num, den


def _head_out(num, den, m_t, g_mn, zo):
    hh = num / jnp.maximum(jnp.abs(den), jnp.exp(-m_t))
    return _layernorm(hh) * g_mn * _sigmoid(zo)


def _ffn_tail(x, f_ref, p, w_down_ref, g_post_ref, g_ple_ref, w_ple_ref, w_pg_ref):
    x2 = x + _rms(_dot(f_ref[...], w_down_ref[...]), g_post_ref[...])
    emb = _dot(p.astype(BF16), w_ple_ref[...])
    gate = _sigmoid(_dot(_rms(x2, g_ple_ref[...]).astype(BF16), w_pg_ref[...]))
    return x2 + emb * gate


def _full(shape):
    n = len(shape)
    return pl.BlockSpec(shape, lambda *_: (0,) * n, pipeline_mode=pl.Buffered(1))


def _layer_prompt_kernel(x_ref, p_ref,
                         g_pre_ref, w_in_ref, w_gate_ref, b_gate_ref, w_cm_ref, b_cm_ref,
                         g_cn_ref, b_cn_ref, g_mn_ref, w_out_ref, g_post_ref,
                         g_fpre_ref, w_up_ref, w_cf_ref, b_cf_ref, w_down_ref, g_fpost_ref,
                         g_ple_ref, w_ple_ref, w_pg_ref,
                         y_ref, conv_out_ref, c_out_ref, n_out_ref, m_out_ref, tail_out_ref,
                         ext_ref, sh_ref, mix_ref, cn_ref, m_ref, h_ref, z_ref, x1_ref, hist_ref, fw_ref, f_ref,
                         *, tiles_per_seq):
    T = PROMPT_TILE
    G = T // SUBLANES
    i = pl.program_id(0)
    n_tiles = pl.num_programs(0) - 1
    s_mix = lax.rem(jnp.minimum(i, n_tiles - 1), tiles_per_seq)
    s_ffn = lax.rem(jnp.maximum(i - 1, 0), tiles_per_seq)
    slot = lax.rem(i, 2)

    @pl.when(i == 0)
    def _():
        x1_ref[...] = jnp.zeros(x1_ref.shape, F32)

    @pl.when(s_mix == 0)
    def _():
        ext_ref[0:HIST, :] = jnp.zeros((HIST, CONV_CH), F32)
        cn_ref[...] = jnp.zeros(cn_ref.shape, F32)
        m_ref[...] = jnp.zeros(m_ref.shape, F32)

    @pl.when(s_ffn == 0)
    def _():
        hist_ref[...] = jnp.zeros(hist_ref.shape, F32)

    wr = lax.broadcasted_iota(jnp.int32, (T, T), 0)
    wc = lax.broadcasted_iota(jnp.int32, (T, T), 1)
    to_work = jnp.where(wc == (wr % SUBLANES) * G + wr // SUBLANES, 1.0, 0.0).astype(BF16)
    to_token = jnp.where(wr == (wc % SUBLANES) * G + wc // SUBLANES, 1.0, 0.0).astype(BF16)

    def moved_down(group, prev_group):
        first = lax.broadcasted_iota(jnp.int32, group.shape, 0) == 0
        return jnp.where(first, pltpu.roll(prev_group, 1, 0), pltpu.roll(group, 1, 0))

    x = x_ref[...]
    h_ref[...] = _rms(x, g_pre_ref[...]).astype(BF16)
    zv = _dot(h_ref[...], w_in_ref[:, 0:CONV_CH])
    zg = _dot(h_ref[...], w_in_ref[:, CONV_CH:2 * CONV_CH])
    gates = _dot(h_ref[...], w_gate_ref[...]) + b_gate_ref[...]
    q_off = 2 * CONV_CH
    for part in range(4):
        ps = slice(part * MLSTM_WIDTH, (part + 1) * MLSTM_WIDTH)
        z_ref[slot, :, ps] = _dot(h_ref[...],
                                  w_in_ref[:, q_off + part * MLSTM_WIDTH:q_off + (part + 1) * MLSTM_WIDTH])

    ext_ref[HIST:HIST + T, :] = zv * _sigmoid(zg)
    for r in range(1, SUBLANES):
        sh_ref[r - 1] = ext_ref[pl.ds(r, SHIFT_ROWS), :]
    recent = []
    for g in range(CONV_GROUPS):
        cs = slice(g * LANES, (g + 1) * LANES)
        for rb in range(T // CONV_ROWS):
            acc = jnp.broadcast_to(b_cm_ref[:, cs], (CONV_ROWS, LANES))
            for j in range(CONV_WIDTH):
                off = HIST - CONV_TAIL + j
                r, base = off % SUBLANES, rb * CONV_ROWS + off - off % SUBLANES
                src = ext_ref if r == 0 else sh_ref.at[r - 1]
                w_row = w_cm_ref[j:j + 1, cs]
                if j == 0 and len(recent) == CONV_IN_FLIGHT:
                    w_row = w_row + _exact_zero(recent.pop(0))
                acc = acc + w_row * src[base:base + CONV_ROWS, cs]
            y = _layernorm(acc) * g_cn_ref[:, cs] + b_cn_ref[:, cs]
            mix_ref[rb * CONV_ROWS:(rb + 1) * CONV_ROWS, cs] = (y * _sigmoid(y)).astype(BF16)
            recent.append(acc[0:1, :])
    ext_ref[0:HIST, :] = ext_ref[T:T + HIST, :]

    x1 = x1_ref[1 - slot]
    h2 = _rms(x1, g_fpre_ref[...]).astype(BF16)
    hp = _dot(to_work, h2).astype(BF16)
    for c in range(D_FF // FFN_CHUNK):
        halves = []
        for half in range(2):
            c0 = half * D_FF + c * FFN_CHUNK
            cs = slice(c0, c0 + FFN_CHUNK)
            u = _dot(hp, w_up_ref[:, cs])
            prev = hist_ref[:, cs]
            hist_ref[:, cs] = u[T - 2 * SUBLANES:T, :]
            w_last = moved_down(u[T - SUBLANES:T, :], prev[SUBLANES:2 * SUBLANES, :])
            w_last2 = moved_down(u[T - 2 * SUBLANES:T - SUBLANES, :], prev[0:SUBLANES, :])
            u1 = jnp.concatenate([w_last, u[0:T - SUBLANES, :]], axis=0)
            u2 = jnp.concatenate([w_last2, w_last, u[0:T - 2 * SUBLANES, :]], axis=0)
            halves.append(w_cf_ref[0:1, cs] * u2 + w_cf_ref[1:2, cs] * u1
                          + w_cf_ref[2:3, cs] * u + b_cf_ref[:, cs])
        fw_ref[:, c * FFN_CHUNK:(c + 1) * FFN_CHUNK] = (_gelu_tanh(halves[0]) * halves[1]).astype(BF16)

    causal = wc <= wr
    bcum = _dot_exact(causal, _log_sigmoid(gates))
    gates_t = gates.T
    bcum_t = bcum.T
    heads = []
    for hd in range(HEADS):
        c0 = hd * HEAD_DIM
        zvv = z_ref[slot, :, 2 * MLSTM_WIDTH + c0:2 * MLSTM_WIDTH + c0 + HEAD_DIM]
        q_bf = (z_ref[slot, :, c0:c0 + HEAD_DIM] * (HEAD_DIM ** -0.5)).astype(BF16)
        k_bf = z_ref[slot, :, MLSTM_WIDTH + c0:MLSTM_WIDTH + c0 + HEAD_DIM].astype(BF16)
        cn = cn_ref[hd]
        heads.append((zvv, k_bf, cn, _qk(q_bf, k_bf), _dot(q_bf, cn.astype(BF16))))

    finished = []
    for hd in range(HEADS):
        zvv, k_bf, cn, scores, carried = heads[hd]
        i_row = gates_t[hd:hd + 1, :]
        i_col = gates[:, hd:hd + 1]
        b_row = bcum_t[HEADS + hd:HEADS + hd + 1, :]
        b_col = bcum[:, HEADS + hd:HEADS + hd + 1]
        m_prev = m_ref[hd:hd + 1, 0:1]
        d = jnp.where(causal, b_col - b_row + i_row, NEG)
        m_t, w_inter, num, den = _mlstm_weighted(scores, zvv.astype(BF16), d, b_col + m_prev)
        m_new = m_t[T - 1:T, :]
        ws = jnp.exp(b_col[T - 1:T, :] - b_col + i_col - m_new)
        vp = jnp.concatenate([ws * zvv, jnp.broadcast_to(ws, (T, HEAD_DIM))], axis=1).astype(BF16)
        kv = lax.dot_general(k_bf, vp, (((0,), (0,)), ((), ())), preferred_element_type=F32)
        finished.append((m_t, w_inter, num, den, m_new, w_inter[T - 1:T, :], kv))

    for hd in range(HEADS):
        c0 = hd * HEAD_DIM
        _, _, cn, _, carried = heads[hd]
        m_t, w_inter, num, den, m_new, decay, kv = finished[hd]
        zo = z_ref[slot, :, 3 * MLSTM_WIDTH + c0:3 * MLSTM_WIDTH + c0 + HEAD_DIM]
        num = num + carried[:, 0:HEAD_DIM] * w_inter
        den = den + carried[:, HEAD_DIM:HEAD_DIM + 1] * w_inter
        mix_ref[:, CONV_CH + c0:CONV_CH + c0 + HEAD_DIM] = _head_out(
            num, den, m_t, g_mn_ref[:, c0:c0 + HEAD_DIM], zo).astype(BF16)
        cn_ref[hd] = decay * cn + kv
        m_ref[hd:hd + 1, :] = jnp.broadcast_to(m_new, (1, LANES))

    for c in range(D_FF // FFN_CHUNK):
        fs = slice(c * FFN_CHUNK, (c + 1) * FFN_CHUNK)
        f_ref[:, fs] = _dot(to_token, fw_ref[:, fs]).astype(BF16)
    y_ref[...] = _ffn_tail(x1, f_ref, p_ref[...], w_down_ref, g_fpost_ref, g_ple_ref, w_ple_ref, w_pg_ref)

    x1_ref[slot] = x + _rms(_dot(mix_ref[...], w_out_ref[...]), g_post_ref[...])

    @pl.when((s_mix == tiles_per_seq - 1) & (i < n_tiles))
    def _():
        conv_out_ref[...] = ext_ref[pl.ds(HIST - CONV_TAIL, CONV_TAIL), :]
        m_out_ref[...] = m_ref[...]
        for hd in range(HEADS):
            cn = cn_ref[hd]
            c_out_ref[hd] = cn[:, 0:HEAD_DIM]
            n_out_ref[hd:hd + 1, :] = cn[:, HEAD_DIM:].T[0:1, :]

    @pl.when((s_ffn == tiles_per_seq - 1) & (i >= 1))
    def _():
        tail_out_ref[0:1, :] = hist_ref[SUBLANES - 1:SUBLANES, :]
        tail_out_ref[1:2, :] = hist_ref[2 * SUBLANES - 1:2 * SUBLANES, :]


def _layer_prompt(x, p, wts, layer):
    B, S, D = x.shape
    T = PROMPT_TILE
    tps = S // T
    n_tiles = B * tps
    weights = (wts["g_mix_pre"], wts["w_in"], wts["w_gate"], wts["b_gate"], wts["w_conv_mix"],
               wts["b_conv_mix"], wts["g_conv_norm"], wts["b_conv_norm"], wts["g_mlstm_norm"],
               wts["w_out"], wts["g_mix_post"],
               wts["g_ffn_pre"], wts["w_up"], wts["w_conv_ffn"], wts["b_conv_ffn"], wts["w_down"],
               wts["g_ffn_post"], wts["g_ple"], wts["w_ple"], wts["w_ple_gate"])
    mix_tile = lambda i: jnp.minimum(i, n_tiles - 1)
    ffn_tile = lambda i: jnp.maximum(i - 1, 0)
    out_shape = (
        jax.ShapeDtypeStruct((B, S, D), F32),
        jax.ShapeDtypeStruct((B, CONV_TAIL, CONV_CH), F32),
        jax.ShapeDtypeStruct((B, HEADS, HEAD_DIM, HEAD_DIM), F32),
        jax.ShapeDtypeStruct((B, HEADS, HEAD_DIM), F32),
        jax.ShapeDtypeStruct((B, SUBLANES, LANES), F32),
        jax.ShapeDtypeStruct((B, FFN_TAIL, 2 * D_FF), F32),
    )
    out_specs = (
        pl.BlockSpec((None, T, D), lambda i: (ffn_tile(i) // tps, ffn_tile(i) % tps, 0)),
        pl.BlockSpec((None, CONV_TAIL, CONV_CH), lambda i: (mix_tile(i) // tps, 0, 0)),
        pl.BlockSpec((None, HEADS, HEAD_DIM, HEAD_DIM), lambda i: (mix_tile(i) // tps, 0, 0, 0)),
        pl.BlockSpec((None, HEADS, HEAD_DIM), lambda i: (mix_tile(i) // tps, 0, 0)),
        pl.BlockSpec((None, SUBLANES, LANES), lambda i: (mix_tile(i) // tps, 0, 0)),
        pl.BlockSpec((None, FFN_TAIL, 2 * D_FF), lambda i: (ffn_tile(i) // tps, 0, 0)),
    )
    return pl.pallas_call(
        functools.partial(_layer_prompt_kernel, tiles_per_seq=tps),
        grid=(n_tiles + 1,),
        in_specs=[pl.BlockSpec((None, T, D), lambda i: (mix_tile(i) // tps, mix_tile(i) % tps, 0)),
                  pl.BlockSpec((None, None, T, PLE_DIM),
                               lambda i: (layer, ffn_tile(i) // tps, ffn_tile(i) % tps, 0))]
                 + [_full(w.shape) for w in weights],
        out_specs=out_specs,
        out_shape=out_shape,
        scratch_shapes=[
            pltpu.VMEM((HIST + T, CONV_CH), F32),
            pltpu.VMEM((SUBLANES - 1, SHIFT_ROWS, CONV_CH), F32),
            pltpu.VMEM((T, D_MODEL), BF16),
            pltpu.VMEM((HEADS, HEAD_DIM, 2 * HEAD_DIM), F32),
            pltpu.VMEM((SUBLANES, LANES), F32),
            pltpu.VMEM((T, D_MODEL), BF16),
            pltpu.VMEM((2, T, 4 * MLSTM_WIDTH), F32),
            pltpu.VMEM((2, T, D_MODEL), F32),
            pltpu.VMEM((2 * SUBLANES, 2 * D_FF), F32),
            pltpu.VMEM((T, D_FF), BF16),
            pltpu.VMEM((T, D_FF), BF16),
        ],
        compiler_params=pltpu.CompilerParams(
            dimension_semantics=("arbitrary",), vmem_limit_bytes=VMEM_LIMIT),
        name="layer_prompt",
    )(x, p, *weights)


def _mixer_sample_kernel(x_ref, st_ref, c_ref, n_ref, mrow_ref,
                         g_pre_ref, w_in_ref, w_gate_ref, b_gate_ref, w_conv_ref, b_conv_ref,
                         g_cn_ref, b_cn_ref, g_mn_ref, w_out_ref, g_post_ref,
                         y_ref, conv_out_ref, c_out_ref, n_out_ref, m_out_ref,
                         ext_ref, q_ref, kt_ref, wv_ref, wk_ref, carried_ref, qn_ref, dec_ref,
                         *, seq_len, stacked_first):
    NB = SAMPLE_SEQS
    R = NB * SEQ_PAD
    if stacked_first:
        c_dst = c_out_ref.at[0]
        c_out_ref[1:] = jnp.zeros((c_out_ref.shape[0] - 1,) + tuple(c_out_ref.shape[1:]), F32)
    else:
        c_dst = c_out_ref
    x = x_ref[...]
    h = _rms(x, g_pre_ref[...]).astype(BF16)

    zv = _dot(h, w_in_ref[:, 0:CONV_CH])
    zg = _dot(h, w_in_ref[:, CONV_CH:2 * CONV_CH])
    a = zv * _sigmoid(zg)
    ext_ref[:, 0:CONV_TAIL, :] = st_ref[...]
    ext_ref[:, CONV_TAIL:CONV_TAIL + SEQ_PAD, :] = a.reshape(NB, SEQ_PAD, CONV_CH)
    ext_ref[:, CONV_TAIL + SEQ_PAD:, :] = jnp.zeros((NB, SAMPLE_EXT_ROWS - CONV_TAIL - SEQ_PAD, CONV_CH), F32)
    acc = jnp.broadcast_to(b_conv_ref[...][None], (NB, SEQ_PAD, CONV_CH))
    for j in range(CONV_WIDTH):
        acc = acc + w_conv_ref[j:j + 1, :][None] * ext_ref[:, pl.ds(j, SEQ_PAD), :]
    conv_out_ref[...] = ext_ref[:, seq_len:seq_len + CONV_TAIL, :]
    mix_parts = _conv_branch_post(acc.reshape(R, CONV_CH), g_cn_ref, b_cn_ref)

    gates = _dot(h, w_gate_ref[...]) + b_gate_ref[...]
    row = lax.broadcasted_iota(jnp.int32, (R, R), 0)
    col = lax.broadcasted_iota(jnp.int32, (R, R), 1)
    same_seq = (row // SEQ_PAD) == (col // SEQ_PAD)
    causal = same_seq & (col <= row)
    bcum = _dot_exact(causal, _log_sigmoid(gates))
    mask = causal & ((col % SEQ_PAD) < seq_len)
    pick_last = same_seq & ((col % SEQ_PAD) == seq_len - 1)
    inter_all = bcum + mrow_ref[...]
    gates_t = gates.T
    bcum_t = bcum.T
    lane = lax.broadcasted_iota(jnp.int32, (R, LANES), 1)
    row_valid = (lax.broadcasted_iota(jnp.int32, (R, 1), 0) % SEQ_PAD) < seq_len
    stats = jnp.where((lane >= HEADS) & (lane < 2 * HEADS), bcum, 0.0)
    q_off = 2 * CONV_CH
    z_parts = [_dot(h, w_in_ref[:, q_off + part * MLSTM_WIDTH:q_off + (part + 1) * MLSTM_WIDTH])
               for part in range(4)]
    saved = []
    for hd in range(HEADS):
        c0 = hd * HEAD_DIM
        zq = z_parts[0][:, c0:c0 + HEAD_DIM] * (HEAD_DIM ** -0.5)
        zk = z_parts[1][:, c0:c0 + HEAD_DIM]
        zvv = z_parts[2][:, c0:c0 + HEAD_DIM]
        zo = z_parts[3][:, c0:c0 + HEAD_DIM]
        i_row = gates_t[hd:hd + 1, :]
        b_row = bcum_t[HEADS + hd:HEADS + hd + 1, :]
        b_col = bcum[:, HEADS + hd:HEADS + hd + 1]
        d = jnp.where(mask, b_col - b_row + i_row, NEG)
        inter = inter_all[:, HEADS + hd:HEADS + hd + 1]
        m_t, w_inter, num, den = _mlstm_weighted(_qk(zq.astype(BF16), zk.astype(BF16)), zvv.astype(BF16), d, inter)
        stats = jnp.where(lane == hd, m_t, stats)
        stats = jnp.where(lane == 2 * HEADS + hd, w_inter, stats)
        q_ref[hd, 0:R, :] = zq
        q_ref[hd, R:R + SEQ_PAD, :] = jnp.zeros((SEQ_PAD, HEAD_DIM), F32)
        kt_ref[hd] = zk.T
        saved.append((m_t, w_inter, num, den, zo, zk, zvv))

    per_seq = _dot_exact(pick_last, stats)
    m_out_ref[...] = per_seq
    for hd in range(HEADS):
        zk, zvv = saved[hd][5], saved[hd][6]
        m_new = per_seq[:, hd:hd + 1]
        b_last = per_seq[:, HEADS + hd:HEADS + hd + 1]
        decay = per_seq[:, 2 * HEADS + hd:2 * HEADS + hd + 1]
        b_col = bcum[:, HEADS + hd:HEADS + hd + 1]
        i_col = gates[:, hd:hd + 1]
        ws = jnp.where(row_valid, jnp.exp(b_last - b_col + i_col - m_new), 0.0)
        wv_ref[hd] = (ws * zvv).astype(BF16)
        wk_ref[hd] = ws * zk
        dec_ref[hd] = jnp.broadcast_to(decay, (R, LANES))

    col_seq = lax.broadcasted_iota(jnp.int32, (HEAD_DIM, R), 1) // SEQ_PAD

    def per_sequence(b, carry):
        r0 = pl.multiple_of(b * SEQ_PAD, SEQ_PAD)
        for hd in range(HEADS):
            c_old = c_ref[b, hd]
            n_old = n_ref[b, hd:hd + 1, :]
            q2 = q_ref[hd, pl.ds(r0, 2 * SEQ_PAD), :]
            carried_ref[hd, pl.ds(r0, SEQ_PAD), :] = _dot(q2.astype(BF16), c_old.astype(BF16))[0:SEQ_PAD, :]
            qn = jnp.sum(q2[0:SEQ_PAD, :] * n_old, axis=1, keepdims=True)
            qn_ref[hd, pl.ds(r0, SEQ_PAD), :] = jnp.broadcast_to(qn, (SEQ_PAD, LANES))
            dec = dec_ref[hd, pl.ds(r0, 1), :]
            kt_b = jnp.where(col_seq == b, kt_ref[hd], 0.0).astype(BF16)
            c_dst[b, hd] = dec * c_old + _dot(kt_b, wv_ref[hd])
            n_out_ref[b, hd:hd + 1, :] = dec * n_old + jnp.sum(wk_ref[hd, pl.ds(r0, SEQ_PAD), :], axis=0,
                                                                keepdims=True)
        return carry

    lax.fori_loop(0, NB, per_sequence, 0, unroll=4)

    for hd in range(HEADS):
        c0 = hd * HEAD_DIM
        m_t, w_inter, num, den, zo = saved[hd][:5]
        num = num + carried_ref[hd] * w_inter
        den = den + qn_ref[hd][:, 0:1] * w_inter
        mix_parts.append(_head_out(num, den, m_t, g_mn_ref[:, c0:c0 + HEAD_DIM], zo))

    mix = jnp.concatenate(mix_parts, axis=1).astype(BF16)
    y_ref[...] = x + _rms(_dot(mix, w_out_ref[...]), g_post_ref[...])


N_MIXER_SAMPLE_INPUTS = 16


def _mixer_sample_kernel_inplace(*refs, seq_len):
    _mixer_sample_kernel(*refs[:N_MIXER_SAMPLE_INPUTS], *refs[N_MIXER_SAMPLE_INPUTS + 1:],
                         seq_len=seq_len, stacked_first=False)


def _mixer_sample(x, st, c, n, mrow, wts, seq_len, layer, c_stack):
    NB = SAMPLE_SEQS
    R = NB * SEQ_PAD
    depth, nseq = c.shape[0], c.shape[1]
    weights = (wts["g_mix_pre"], wts["w_in"], wts["w_gate"], wts["b_gate"], wts["w_conv_mix"],
               wts["b_conv_mix"], wts["g_conv_norm"], wts["b_conv_norm"], wts["g_mlstm_norm"],
               wts["w_out"], wts["g_mix_post"])
    rows = lambda width: pl.BlockSpec((R, width), lambda i: (i, 0))
    st_spec = pl.BlockSpec((NB, CONV_TAIL, CONV_CH), lambda i: (i, 0, 0))
    n_spec = pl.BlockSpec((NB, HEADS, HEAD_DIM), lambda i: (i, 0, 0))
    st_in = pl.BlockSpec((None, NB, CONV_TAIL, CONV_CH), lambda i: (layer, i, 0, 0))
    c_in = pl.BlockSpec((None, NB, HEADS, HEAD_DIM, HEAD_DIM), lambda i: (layer, i, 0, 0, 0))
    n_in = pl.BlockSpec((None, NB, HEADS, HEAD_DIM), lambda i: (layer, i, 0, 0))
    in_specs = [rows(D_MODEL), st_in, c_in, n_in, rows(LANES)] + [_full(w.shape) for w in weights]
    operands = (x, st, c, n, mrow, *weights)
    assert len(operands) == N_MIXER_SAMPLE_INPUTS
    if c_stack is None:
        body = functools.partial(_mixer_sample_kernel, seq_len=seq_len, stacked_first=True)
        c_out_spec = pl.BlockSpec((depth, NB, HEADS, HEAD_DIM, HEAD_DIM), lambda i: (0, i, 0, 0, 0))
        aliases = {}
    else:
        body = functools.partial(_mixer_sample_kernel_inplace, seq_len=seq_len)
        c_out_spec = c_in
        in_specs.append(pl.BlockSpec(memory_space=pl.ANY))
        operands = operands + (c_stack,)
        aliases = {N_MIXER_SAMPLE_INPUTS: 2}
    return pl.pallas_call(
        body,
        grid=(nseq // NB,),
        in_specs=in_specs,
        out_specs=(rows(D_MODEL), st_spec, c_out_spec, n_spec, rows(LANES)),
        out_shape=(jax.ShapeDtypeStruct(x.shape, F32), jax.ShapeDtypeStruct(st.shape[1:], F32),
                   jax.ShapeDtypeStruct(c.shape, F32), jax.ShapeDtypeStruct(n.shape[1:], F32),
                   jax.ShapeDtypeStruct(mrow.shape, F32)),
        input_output_aliases=aliases,
        scratch_shapes=[
            pltpu.VMEM((NB, SAMPLE_EXT_ROWS, CONV_CH), F32),
            pltpu.VMEM((HEADS, R + SEQ_PAD, HEAD_DIM), F32),
            pltpu.VMEM((HEADS, HEAD_DIM, R), F32),
            pltpu.VMEM((HEADS, R, HEAD_DIM), BF16),
            pltpu.VMEM((HEADS, R, HEAD_DIM), F32),
            pltpu.VMEM((HEADS, R, HEAD_DIM), F32),
            pltpu.VMEM((HEADS, R, LANES), F32),
            pltpu.VMEM((HEADS, R, LANES), F32),
        ],
        compiler_params=pltpu.CompilerParams(
            dimension_semantics=("arbitrary",), vmem_limit_bytes=VMEM_LIMIT),
        name="mixer_sample",
    )(*operands)


def _ffn_sample_kernel(x_ref, p_ref, st_ref, g_pre_ref, w_up_ref, w_conv_ref, b_conv_ref, w_down_ref,
                       g_post_ref, g_ple_ref, w_ple_ref, w_pg_ref,
                       y_ref, tail_out_ref,
                       ubuf_ref, f_ref, *, seq_len):
    NB = SAMPLE_SEQS_FFN
    R = NB * SEQ_PAD
    x = x_ref[...]
    h = _rms(x, g_pre_ref[...]).astype(BF16)
    lo = SEQ_PAD - FFN_TAIL
    for c in range(D_FF // FFN_CHUNK):
        halves = []
        for half in range(2):
            c0 = half * D_FF + c * FFN_CHUNK
            cs = slice(c0, c0 + FFN_CHUNK)
            u = _dot(h, w_up_ref[:, cs])
            ubuf_ref[half, :, lo:SEQ_PAD, :] = st_ref[:, :, cs]
            ubuf_ref[half, :, SEQ_PAD:2 * SEQ_PAD, :] = u.reshape(NB, SEQ_PAD, FFN_CHUNK)
            tail_out_ref[:, :, cs] = ubuf_ref[half, :, lo + seq_len:SEQ_PAD + seq_len, :]
            y = (w_conv_ref[0:1, cs][None] * ubuf_ref[half, :, pl.ds(lo, SEQ_PAD), :]
                 + w_conv_ref[1:2, cs][None] * ubuf_ref[half, :, pl.ds(lo + 1, SEQ_PAD), :]
                 + w_conv_ref[2:3, cs][None] * ubuf_ref[half, :, pl.ds(lo + 2, SEQ_PAD), :]
                 + b_conv_ref[:, cs][None])
            halves.append(y.reshape(R, FFN_CHUNK))
        f_ref[:, c * FFN_CHUNK:(c + 1) * FFN_CHUNK] = (_gelu_tanh(halves[0]) * halves[1]).astype(BF16)
    y_ref[...] = _ffn_tail(x, f_ref, p_ref[...], w_down_ref, g_post_ref, g_ple_ref, w_ple_ref, w_pg_ref)


def _ffn_sample(x, p, st, wts, seq_len, layer):
    NB = SAMPLE_SEQS_FFN
    R = NB * SEQ_PAD
    nseq = st.shape[1]
    weights = (wts["g_ffn_pre"], wts["w_up"], wts["w_conv_ffn"], wts["b_conv_ffn"], wts["w_down"],
               wts["g_ffn_post"], wts["g_ple"], wts["w_ple"], wts["w_ple_gate"])
    st_spec = pl.BlockSpec((NB, FFN_TAIL, 2 * D_FF), lambda i: (i, 0, 0))
    st_in = pl.BlockSpec((None, NB, FFN_TAIL, 2 * D_FF), lambda i: (layer, i, 0, 0))
    return pl.pallas_call(
        functools.partial(_ffn_sample_kernel, seq_len=seq_len),
        grid=(nseq // NB,),
        in_specs=[pl.BlockSpec((R, D_MODEL), lambda i: (i, 0)), pl.BlockSpec((R, PLE_DIM), lambda i: (i, 0)),
                  st_in] + [_full(w.shape) for w in weights],
        out_specs=(pl.BlockSpec((R, D_MODEL), lambda i: (i, 0)), st_spec),
        out_shape=(jax.ShapeDtypeStruct(x.shape, F32), jax.ShapeDtypeStruct(st.shape[1:], F32)),
        scratch_shapes=[
            pltpu.VMEM((2, NB, 2 * SEQ_PAD, FFN_CHUNK), F32),
            pltpu.VMEM((R, D_FF), BF16),
        ],
        compiler_params=pltpu.CompilerParams(
            dimension_semantics=("arbitrary",), vmem_limit_bytes=VMEM_LIMIT),
        name="ffn_sample",
    )(x, p, st, *weights)


def _pad_seq(a):
    nseq, seq_len, width = a.shape
    return jnp.pad(a, ((0, 0), (0, SEQ_PAD - seq_len), (0, 0))).reshape(nseq * SEQ_PAD, width)


def _layer_weights(l, g_mix_pre, w_in, b_igate, b_fgate, w_conv_mix, b_conv_mix, g_conv_norm,
                   b_conv_norm, g_mlstm_norm, w_out, g_mix_post, g_ffn_pre, w_up, w_conv_ffn,
                   b_conv_ffn, w_down, g_ffn_post, g_ple, w_ple, w_ple_gate):
    row = lambda v: v[l][None, :].astype(F32)
    n_gate = 2 * HEADS
    w_gate = jnp.pad(w_in[l][:, MAIN_COLS:], ((0, 0), (0, LANES - n_gate))).astype(BF16)
    b_gate = jnp.pad(jnp.concatenate([b_igate[l], b_fgate[l]]), (0, LANES - n_gate))[None, :].astype(F32)
    return {
        "g_mix_pre": row(g_mix_pre), "w_in": w_in[l][:, :MAIN_COLS].astype(BF16),
        "w_gate": w_gate, "b_gate": b_gate,
        "w_conv_mix": w_conv_mix[l].astype(F32), "b_conv_mix": row(b_conv_mix),
        "g_conv_norm": row(g_conv_norm), "b_conv_norm": row(b_conv_norm),
        "g_mlstm_norm": row(g_mlstm_norm), "w_out": w_out[l].astype(BF16), "g_mix_post": row(g_mix_post),
        "g_ffn_pre": row(g_ffn_pre), "w_up": w_up[l].astype(BF16), "w_conv_ffn": w_conv_ffn[l].astype(F32),
        "b_conv_ffn": row(b_conv_ffn), "w_down": w_down[l].astype(BF16), "g_ffn_post": row(g_ffn_post),
        "g_ple": row(g_ple), "w_ple": w_ple[l].astype(BF16), "w_ple_gate": w_ple_gate[l].astype(BF16),
    }


def kernel(x_prompt, x_sample, p_prompt, p_sample, state_conv_mix, state_mlstm_C, state_mlstm_n, state_mlstm_m, state_conv_ffn, g_mix_pre, w_in, b_igate, b_fgate, w_conv_mix, b_conv_mix, g_conv_norm, b_conv_norm, g_mlstm_norm, w_out, g_mix_post, g_ffn_pre, w_up, w_conv_ffn, b_conv_ffn, w_down, g_ffn_post, g_ple, w_ple, w_ple_gate):
    depth = w_in.shape[0]
    nseq, seq_len, _ = x_sample.shape
    assert FFN_TAIL <= seq_len <= SEQ_PAD and nseq % SAMPLE_SEQS == 0 and nseq % SAMPLE_SEQS_FFN == 0
    assert x_prompt.shape[1] % PROMPT_TILE == 0 and CONV_TAIL <= HIST <= PROMPT_TILE
    xp = x_prompt
    xs = _pad_seq(x_sample)
    pc, pC, pn, pm, pf = [], [], [], [], []
    sc, sn, sm, sf = [], [], [], []
    c_stack = None
    for l in range(depth):
        wts = _layer_weights(l, g_mix_pre, w_in, b_igate, b_fgate, w_conv_mix, b_conv_mix, g_conv_norm,
                             b_conv_norm, g_mlstm_norm, w_out, g_mix_post, g_ffn_pre, w_up, w_conv_ffn,
                             b_conv_ffn, w_down, g_ffn_post, g_ple, w_ple, w_ple_gate)
        xp, c1, C1, n1, m1, f1 = _layer_prompt(xp, p_prompt, wts, l)
        pc.append(c1); pC.append(C1); pn.append(n1); pm.append(m1[:, :HEADS, 0]); pf.append(f1)

        mrow = jnp.pad(jnp.repeat(state_mlstm_m[l].astype(F32), SEQ_PAD, axis=0),
                       ((0, 0), (HEADS, LANES - 2 * HEADS)))
        xs, c2, c_stack, n2, m2 = _mixer_sample(xs, state_conv_mix, state_mlstm_C, state_mlstm_n, mrow,
                                                wts, seq_len, l, c_stack)
        xs, f2 = _ffn_sample(xs, _pad_seq(p_sample[l]), state_conv_ffn, wts, seq_len, l)
        sc.append(c2); sn.append(n2); sf.append(f2)
        sm.append(m2.reshape(nseq, SEQ_PAD, LANES)[:, 0, :HEADS])
    ys = xs.reshape(nseq, SEQ_PAD, D_MODEL)[:, :seq_len]
    return (xp, ys, jnp.stack(pc), jnp.stack(pC), jnp.stack(pn), jnp.stack(pm), jnp.stack(pf),
            jnp.stack(sc), c_stack, jnp.stack(sn), jnp.stack(sm), jnp.stack(sf))
```

```python
import functools

import jax
import jax.numpy as jnp
from jax import lax
from jax.experimental import pallas as pl
from jax.experimental.pallas import tpu as pltpu

F32 = jnp.float32
BF16 = jnp.bfloat16

D_MODEL = 1024
CONV_CH = 512
CONV_WIDTH = 31
CONV_TAIL = CONV_WIDTH - 1
CONV_GROUPS = 4
HEADS = 4
HEAD_DIM = 128
MLSTM_WIDTH = HEADS * HEAD_DIM
D_FF = 2816
FFN_TAIL = 2
PLE_DIM = 256
EPS = 1e-6
MAIN_COLS = 2 * CONV_CH + 4 * MLSTM_WIDTH
LANES = 128
SUBLANES = 8
NEG = -1e30

PROMPT_TILE = 256
SAMPLE_SEQS = 16
SAMPLE_SEQS_FFN = 32
SEQ_PAD = SUBLANES
HIST = 32
SHIFT_ROWS = PROMPT_TILE + HIST - SUBLANES
CONV_ROWS = 64
CONV_IN_FLIGHT = 2
FFN_CHUNK = 256
SAMPLE_EXT_ROWS = -(-(CONV_TAIL + SEQ_PAD) // SUBLANES) * SUBLANES
VMEM_LIMIT = 56 * 1024 * 1024


def _dot(a, b):
    return jnp.dot(a, b, preferred_element_type=F32)


def _dot_exact(sel, x):
    hi = x.astype(BF16)
    r1 = x - hi.astype(F32)
    mid = r1.astype(BF16)
    lo = (r1 - mid.astype(F32)).astype(BF16)
    y = _dot(jnp.where(sel, 1.0, 0.0).astype(BF16), jnp.concatenate([hi, mid, lo], axis=1))
    return y[:, 0:LANES] + y[:, LANES:2 * LANES] + y[:, 2 * LANES:3 * LANES]


def _rms(x, g):
    ms = jnp.mean(x * x, axis=-1, keepdims=True)
    return x * lax.rsqrt(ms + EPS) * g


def _layernorm(x):
    mu = jnp.mean(x, axis=-1, keepdims=True)
    xc = x - mu
    var = jnp.mean(xc * xc, axis=-1, keepdims=True)
    return xc * lax.rsqrt(var + EPS)


def _sigmoid(x):
    return 1.0 / (1.0 + jnp.exp(-x))


def _log_sigmoid(x):
    return jnp.minimum(x, 0.0) - jnp.log(1.0 + jnp.exp(-jnp.abs(x)))


def _exact_zero(v):
    bits = pltpu.bitcast(v, jnp.uint32)
    bits = lax.shift_right_logical(lax.shift_right_logical(bits, jnp.uint32(16)), jnp.uint32(16))
    return pltpu.bitcast(bits, F32)


def _gelu_tanh(x):
    return 0.5 * x * (1.0 + jnp.tanh(0.7978845608028654 * (x + 0.044715 * (x * x * x))))


def _conv_branch_post(acc, g_ref, b_ref):
    parts = []
    for g in range(CONV_GROUPS):
        sl = slice(g * LANES, (g + 1) * LANES)
        y = _layernorm(acc[:, sl]) * g_ref[:, sl] + b_ref[:, sl]
        parts.append(y * _sigmoid(y))
    return parts


def _qk(q_bf, k_bf):
    return lax.dot_general(q_bf, k_bf, (((1,), (1,)), ((), ())), preferred_element_type=F32)


def _mlstm_weighted(scores, v_bf, d, inter):
    m_t = jnp.maximum(inter, jnp.max(d, axis=1, keepdims=True))
    w_intra = jnp.exp(d - m_t)
    w_inter = jnp.exp(inter - m_t)
    s = scores * w_intra
    num = _dot(s.astype(BF16), v_bf)
    den = jnp.sum(s, axis=1, keepdims=True)
    return m_t, w_inter, num, den


def _head_out(num, den, m_t, g_mn, zo):
    hh = num / jnp.maximum(jnp.abs(den), jnp.exp(-m_t))
    return _layernorm(hh) * g_mn * _sigmoid(zo)


def _ffn_tail(x, f_ref, p, w_down_ref, g_post_ref, g_ple_ref, w_ple_ref, w_pg_ref):
    x2 = x + _rms(_dot(f_ref[...], w_down_ref[...]), g_post_ref[...])
    emb = _dot(p.astype(BF16), w_ple_ref[...])
    gate = _sigmoid(_dot(_rms(x2, g_ple_ref[...]).astype(BF16), w_pg_ref[...]))
    return x2 + emb * gate


def _layer_block(shape, layer):
    n = len(shape)
    return pl.BlockSpec((None,) + tuple(shape[1:]), lambda *_: (layer,) + (0,) * (n - 1),
                        pipeline_mode=pl.Buffered(1))


def _layer_prompt_kernel(x_ref, p_ref,
                         g_pre_ref, w_in_ref, w_gate_ref, b_gate_ref, w_cm_ref, b_cm_ref,
                         g_cn_ref, b_cn_ref, g_mn_ref, w_out_ref, g_post_ref,
                         g_fpre_ref, w_up_ref, w_cf_ref, b_cf_ref, w_down_ref, g_fpost_ref,
                         g_ple_ref, w_ple_ref, w_pg_ref,
                         y_ref, conv_out_ref, c_out_ref, n_out_ref, m_out_ref, tail_out_ref,
                         ext_ref, sh_ref, mix_ref, cn_ref, m_ref, h_ref, z_ref, x1_ref, hist_ref, fw_ref, f_ref,
                         *, tiles_per_seq):
    T = PROMPT_TILE
    G = T // SUBLANES
    i = pl.program_id(0)
    n_tiles = pl.num_programs(0) - 1
    s_mix = lax.rem(jnp.minimum(i, n_tiles - 1), tiles_per_seq)
    s_ffn = lax.rem(jnp.maximum(i - 1, 0), tiles_per_seq)
    slot = lax.rem(i, 2)

    @pl.when(i == 0)
    def _():
        x1_ref[...] = jnp.zeros(x1_ref.shape, F32)

    @pl.when(s_mix == 0)
    def _():
        ext_ref[0:HIST, :] = jnp.zeros((HIST, CONV_CH), F32)
        cn_ref[...] = jnp.zeros(cn_ref.shape, F32)
        m_ref[...] = jnp.zeros(m_ref.shape, F32)

    @pl.when(s_ffn == 0)
    def _():
        hist_ref[...] = jnp.zeros(hist_ref.shape, F32)

    wr = lax.broadcasted_iota(jnp.int32, (T, T), 0)
    wc = lax.broadcasted_iota(jnp.int32, (T, T), 1)
    to_work = jnp.where(wc == (wr % SUBLANES) * G + wr // SUBLANES, 1.0, 0.0).astype(BF16)
    to_token = jnp.where(wr == (wc % SUBLANES) * G + wc // SUBLANES, 1.0, 0.0).astype(BF16)

    def moved_down(group, prev_group):
        first = lax.broadcasted_iota(jnp.int32, group.shape, 0) == 0
        return jnp.where(first, pltpu.roll(prev_group, 1, 0), pltpu.roll(group, 1, 0))

    x = x_ref[...]
    h_ref[...] = _rms(x, g_pre_ref[...]).astype(BF16)
    zv = _dot(h_ref[...], w_in_ref[:, 0:CONV_CH])
    zg = _dot(h_ref[...], w_in_ref[:, CONV_CH:2 * CONV_CH])
    gates = _dot(h_ref[...], w_gate_ref[...]) + b_gate_ref[...]
    q_off = 2 * CONV_CH
    for part in range(4):
        ps = slice(part * MLSTM_WIDTH, (part + 1) * MLSTM_WIDTH)
        z_ref[slot, :, ps] = _dot(h_ref[...],
                                  w_in_ref[:, q_off + part * MLSTM_WIDTH:q_off + (part + 1) * MLSTM_WIDTH])

    ext_ref[HIST:HIST + T, :] = zv * _sigmoid(zg)
    for r in range(1, SUBLANES):
        sh_ref[r - 1] = ext_ref[pl.ds(r, SHIFT_ROWS), :]
    recent = []
    for g in range(CONV_GROUPS):
        cs = slice(g * LANES, (g + 1) * LANES)
        for rb in range(T // CONV_ROWS):
            acc = jnp.broadcast_to(b_cm_ref[:, cs], (CONV_ROWS, LANES))
            for j in range(CONV_WIDTH):
                off = HIST - CONV_TAIL + j
                r, base = off % SUBLANES, rb * CONV_ROWS + off - off % SUBLANES
                src = ext_ref if r == 0 else sh_ref.at[r - 1]
                w_row = w_cm_ref[j:j + 1, cs]
                if j == 0 and len(recent) == CONV_IN_FLIGHT:
                    w_row = w_row + _exact_zero(recent.pop(0))
                acc = acc + w_row * src[base:base + CONV_ROWS, cs]
            y = _layernorm(acc) * g_cn_ref[:, cs] + b_cn_ref[:, cs]
            mix_ref[rb * CONV_ROWS:(rb + 1) * CONV_ROWS, cs] = (y * _sigmoid(y)).astype(BF16)
            recent.append(acc[0:1, :])
    ext_ref[0:HIST, :] = ext_ref[T:T + HIST, :]

    x1 = x1_ref[1 - slot]
    h2 = _rms(x1, g_fpre_ref[...]).astype(BF16)
    hp = _dot(to_work, h2).astype(BF16)
    for c in range(D_FF // FFN_CHUNK):
        halves = []
        for half in range(2):
            c0 = half * D_FF + c * FFN_CHUNK
            cs = slice(c0, c0 + FFN_CHUNK)
            u = _dot(hp, w_up_ref[:, cs])
            prev = hist_ref[:, cs]
            hist_ref[:, cs] = u[T - 2 * SUBLANES:T, :]
            w_last = moved_down(u[T - SUBLANES:T, :], prev[SUBLANES:2 * SUBLANES, :])
            w_last2 = moved_down(u[T - 2 * SUBLANES:T - SUBLANES, :], prev[0:SUBLANES, :])
            u1 = jnp.concatenate([w_last, u[0:T - SUBLANES, :]], axis=0)
            u2 = jnp.concatenate([w_last2, w_last, u[0:T - 2 * SUBLANES, :]], axis=0)
            halves.append(w_cf_ref[0:1, cs] * u2 + w_cf_ref[1:2, cs] * u1
                          + w_cf_ref[2:3, cs] * u + b_cf_ref[:, cs])
        fw_ref[:, c * FFN_CHUNK:(c + 1) * FFN_CHUNK] = (_gelu_tanh(halves[0]) * halves[1]).astype(BF16)

    causal = wc <= wr
    bcum = _dot_exact(causal, _log_sigmoid(gates))
    gates_t = gates.T
    bcum_t = bcum.T
    heads = []
    for hd in range(HEADS):
        c0 = hd * HEAD_DIM
        zvv = z_ref[slot, :, 2 * MLSTM_WIDTH + c0:2 * MLSTM_WIDTH + c0 + HEAD_DIM]
        q_bf = (z_ref[slot, :, c0:c0 + HEAD_DIM] * (HEAD_DIM ** -0.5)).astype(BF16)
        k_bf = z_ref[slot, :, MLSTM_WIDTH + c0:MLSTM_WIDTH + c0 + HEAD_DIM].astype(BF16)
        cn = cn_ref[hd]
        heads.append((zvv, k_bf, cn, _qk(q_bf, k_bf), _dot(q_bf, cn.astype(BF16))))

    finished = []
    for hd in range(HEADS):
        zvv, k_bf, cn, scores, carried = heads[hd]
        i_row = gates_t[hd:hd + 1, :]
        i_col = gates[:, hd:hd + 1]
        b_row = bcum_t[HEADS + hd:HEADS + hd + 1, :]
        b_col = bcum[:, HEADS + hd:HEADS + hd + 1]
        m_prev = m_ref[hd:hd + 1, 0:1]
        d = jnp.where(causal, b_col - b_row + i_row, NEG)
        m_t, w_inter, num, den = _mlstm_weighted(scores, zvv.astype(BF16), d, b_col + m_prev)
        m_new = m_t[T - 1:T, :]
        ws = jnp.exp(b_col[T - 1:T, :] - b_col + i_col - m_new)
        vp = jnp.concatenate([ws * zvv, jnp.broadcast_to(ws, (T, HEAD_DIM))], axis=1).astype(BF16)
        kv = lax.dot_general(k_bf, vp, (((0,), (0,)), ((), ())), preferred_element_type=F32)
        finished.append((m_t, w_inter, num, den, m_new, w_inter[T - 1:T, :], kv))

    for hd in range(HEADS):
        c0 = hd * HEAD_DIM
        _, _, cn, _, carried = heads[hd]
        m_t, w_inter, num, den, m_new, decay, kv = finished[hd]
        zo = z_ref[slot, :, 3 * MLSTM_WIDTH + c0:3 * MLSTM_WIDTH + c0 + HEAD_DIM]
        num = num + carried[:, 0:HEAD_DIM] * w_inter
        den = den + carried[:, HEAD_DIM:HEAD_DIM + 1] * w_inter
        mix_ref[:, CONV_CH + c0:CONV_CH + c0 + HEAD_DIM] = _head_out(
            num, den, m_t, g_mn_ref[:, c0:c0 + HEAD_DIM], zo).astype(BF16)
        cn_ref[hd] = decay * cn + kv
        m_ref[hd:hd + 1, :] = jnp.broadcast_to(m_new, (1, LANES))

    for c in range(D_FF // FFN_CHUNK):
        fs = slice(c * FFN_CHUNK, (c + 1) * FFN_CHUNK)
        f_ref[:, fs] = _dot(to_token, fw_ref[:, fs]).astype(BF16)
    y_ref[...] = _ffn_tail(x1, f_ref, p_ref[...], w_down_ref, g_fpost_ref, g_ple_ref, w_ple_ref, w_pg_ref)

    x1_ref[slot] = x + _rms(_dot(mix_ref[...], w_out_ref[...]), g_post_ref[...])

    @pl.when((s_mix == tiles_per_seq - 1) & (i < n_tiles))
    def _():
        conv_out_ref[...] = ext_ref[pl.ds(HIST - CONV_TAIL, CONV_TAIL), :]
        m_out_ref[...] = m_ref[...]
        for hd in range(HEADS):
            cn = cn_ref[hd]
            c_out_ref[hd] = cn[:, 0:HEAD_DIM]
            n_out_ref[hd:hd + 1, :] = cn[:, HEAD_DIM:].T[0:1, :]

    @pl.when((s_ffn == tiles_per_seq - 1) & (i >= 1))
    def _():
        tail_out_ref[0:1, :] = hist_ref[SUBLANES - 1:SUBLANES, :]
        tail_out_ref[1:2, :] = hist_ref[2 * SUBLANES - 1:2 * SUBLANES, :]


def _layer_prompt(x, p, wts, layer):
    B, S, D = x.shape
    T = PROMPT_TILE
    tps = S // T
    n_tiles = B * tps
    weights = (wts["g_mix_pre"], wts["w_in"], wts["w_gate"], wts["b_gate"], wts["w_conv_mix"],
               wts["b_conv_mix"], wts["g_conv_norm"], wts["b_conv_norm"], wts["g_mlstm_norm"],
               wts["w_out"], wts["g_mix_post"],
               wts["g_ffn_pre"], wts["w_up"], wts["w_conv_ffn"], wts["b_conv_ffn"], wts["w_down"],
               wts["g_ffn_post"], wts["g_ple"], wts["w_ple"], wts["w_ple_gate"])
    mix_tile = lambda i: jnp.minimum(i, n_tiles - 1)
    ffn_tile = lambda i: jnp.maximum(i - 1, 0)
    out_shape = (
        jax.ShapeDtypeStruct((B, S, D), F32),
        jax.ShapeDtypeStruct((B, CONV_TAIL, CONV_CH), F32),
        jax.ShapeDtypeStruct((B, HEADS, HEAD_DIM, HEAD_DIM), F32),
        jax.ShapeDtypeStruct((B, HEADS, HEAD_DIM), F32),
        jax.ShapeDtypeStruct((B, SUBLANES, LANES), F32),
        jax.ShapeDtypeStruct((B, FFN_TAIL, 2 * D_FF), F32),
    )
    out_specs = (
        pl.BlockSpec((None, T, D), lambda i: (ffn_tile(i) // tps, ffn_tile(i) % tps, 0)),
        pl.BlockSpec((None, CONV_TAIL, CONV_CH), lambda i: (mix_tile(i) // tps, 0, 0)),
        pl.BlockSpec((None, HEADS, HEAD_DIM, HEAD_DIM), lambda i: (mix_tile(i) // tps, 0, 0, 0)),
        pl.BlockSpec((None, HEADS, HEAD_DIM), lambda i: (mix_tile(i) // tps, 0, 0)),
        pl.BlockSpec((None, SUBLANES, LANES), lambda i: (mix_tile(i) // tps, 0, 0)),
        pl.BlockSpec((None, FFN_TAIL, 2 * D_FF), lambda i: (ffn_tile(i) // tps, 0, 0)),
    )
    return pl.pallas_call(
        functools.partial(_layer_prompt_kernel, tiles_per_seq=tps),
        grid=(n_tiles + 1,),
        in_specs=[pl.BlockSpec((None, T, D), lambda i: (mix_tile(i) // tps, mix_tile(i) % tps, 0)),
                  pl.BlockSpec((None, None, T, PLE_DIM),
                               lambda i: (layer, ffn_tile(i) // tps, ffn_tile(i) % tps, 0))]
                 + [_layer_block(w.shape, layer) for w in weights],
        out_specs=out_specs,
        out_shape=out_shape,
        scratch_shapes=[
            pltpu.VMEM((HIST + T, CONV_CH), F32),
            pltpu.VMEM((SUBLANES - 1, SHIFT_ROWS, CONV_CH), F32),
            pltpu.VMEM((T, D_MODEL), BF16),
            pltpu.VMEM((HEADS, HEAD_DIM, 2 * HEAD_DIM), F32),
            pltpu.VMEM((SUBLANES, LANES), F32),
            pltpu.VMEM((T, D_MODEL), BF16),
            pltpu.VMEM((2, T, 4 * MLSTM_WIDTH), F32),
            pltpu.VMEM((2, T, D_MODEL), F32),
            pltpu.VMEM((2 * SUBLANES, 2 * D_FF), F32),
            pltpu.VMEM((T, D_FF), BF16),
            pltpu.VMEM((T, D_FF), BF16),
        ],
        compiler_params=pltpu.CompilerParams(
            dimension_semantics=("arbitrary",), vmem_limit_bytes=VMEM_LIMIT),
        name="layer_prompt",
    )(x, p, *weights)


def _mixer_sample_kernel(x_ref, st_ref, c_ref, n_ref, mrow_ref,
                         g_pre_ref, w_in_ref, w_gate_ref, b_gate_ref, w_conv_ref, b_conv_ref,
                         g_cn_ref, b_cn_ref, g_mn_ref, w_out_ref, g_post_ref,
                         y_ref, conv_out_ref, c_out_ref, n_out_ref, m_out_ref,
                         ext_ref, q_ref, kt_ref, wv_ref, wk_ref, carried_ref, qn_ref, dec_ref,
                         *, seq_len, stacked_first):
    NB = SAMPLE_SEQS
    R = NB * SEQ_PAD
    if stacked_first:
        c_dst = c_out_ref.at[0]
        c_out_ref[1:] = jnp.zeros((c_out_ref.shape[0] - 1,) + tuple(c_out_ref.shape[1:]), F32)
    else:
        c_dst = c_out_ref
    x = x_ref[...]
    h = _rms(x, g_pre_ref[...]).astype(BF16)

    zv = _dot(h, w_in_ref[:, 0:CONV_CH])
    zg = _dot(h, w_in_ref[:, CONV_CH:2 * CONV_CH])
    a = zv * _sigmoid(zg)
    ext_ref[:, 0:CONV_TAIL, :] = st_ref[...]
    ext_ref[:, CONV_TAIL:CONV_TAIL + SEQ_PAD, :] = a.reshape(NB, SEQ_PAD, CONV_CH)
    ext_ref[:, CONV_TAIL + SEQ_PAD:, :] = jnp.zeros((NB, SAMPLE_EXT_ROWS - CONV_TAIL - SEQ_PAD, CONV_CH), F32)
    acc = jnp.broadcast_to(b_conv_ref[...][None], (NB, SEQ_PAD, CONV_CH))
    for j in range(CONV_WIDTH):
        acc = acc + w_conv_ref[j:j + 1, :][None] * ext_ref[:, pl.ds(j, SEQ_PAD), :]
    conv_out_ref[...] = ext_ref[:, seq_len:seq_len + CONV_TAIL, :]
    mix_parts = _conv_branch_post(acc.reshape(R, CONV_CH), g_cn_ref, b_cn_ref)

    gates = _dot(h, w_gate_ref[...]) + b_gate_ref[...]
    row = lax.broadcasted_iota(jnp.int32, (R, R), 0)
    col = lax.broadcasted_iota(jnp.int32, (R, R), 1)
    same_seq = (row // SEQ_PAD) == (col // SEQ_PAD)
    causal = same_seq & (col <= row)
    bcum = _dot_exact(causal, _log_sigmoid(gates))
    mask = causal & ((col % SEQ_PAD) < seq_len)
    pick_last = same_seq & ((col % SEQ_PAD) == seq_len - 1)
    inter_all = bcum + mrow_ref[...]
    gates_t = gates.T
    bcum_t = bcum.T
    lane = lax.broadcasted_iota(jnp.int32, (R, LANES), 1)
    row_valid = (lax.broadcasted_iota(jnp.int32, (R, 1), 0) % SEQ_PAD) < seq_len
    stats = jnp.where((lane >= HEADS) & (lane < 2 * HEADS), bcum, 0.0)
    q_off = 2 * CONV_CH
    z_parts = [_dot(h, w_in_ref[:, q_off + part * MLSTM_WIDTH:q_off + (part + 1) * MLSTM_WIDTH])
               for part in range(4)]
    saved = []
    for hd in range(HEADS):
        c0 = hd * HEAD_DIM
        zq = z_parts[0][:, c0:c0 + HEAD_DIM] * (HEAD_DIM ** -0.5)
        zk = z_parts[1][:, c0:c0 + HEAD_DIM]
        zvv = z_parts[2][:, c0:c0 + HEAD_DIM]
        zo = z_parts[3][:, c0:c0 + HEAD_DIM]
        i_row = gates_t[hd:hd + 1, :]
        b_row = bcum_t[HEADS + hd:HEADS + hd + 1, :]
        b_col = bcum[:, HEADS + hd:HEADS + hd + 1]
        d = jnp.where(mask, b_col - b_row + i_row, NEG)
        inter = inter_all[:, HEADS + hd:HEADS + hd + 1]
        m_t, w_inter, num, den = _mlstm_weighted(_qk(zq.astype(BF16), zk.astype(BF16)), zvv.astype(BF16), d, inter)
        stats = jnp.where(lane == hd, m_t, stats)
        stats = jnp.where(lane == 2 * HEADS + hd, w_inter, stats)
        q_ref[hd, 0:R, :] = zq
        q_ref[hd, R:R + SEQ_PAD, :] = jnp.zeros((SEQ_PAD, HEAD_DIM), F32)
        kt_ref[hd] = zk.T
        saved.append((m_t, w_inter, num, den, zo, zk, zvv))

    per_seq = _dot_exact(pick_last, stats)
    m_out_ref[...] = per_seq
    for hd in range(HEADS):
        zk, zvv = saved[hd][5], saved[hd][6]
        m_new = per_seq[:, hd:hd + 1]
        b_last = per_seq[:, HEADS + hd:HEADS + hd + 1]
        decay = per_seq[:, 2 * HEADS + hd:2 * HEADS + hd + 1]
        b_col = bcum[:, HEADS + hd:HEADS + hd + 1]
        i_col = gates[:, hd:hd + 1]
        ws = jnp.where(row_valid, jnp.exp(b_last - b_col + i_col - m_new), 0.0)
        wv_ref[hd] = (ws * zvv).astype(BF16)
        wk_ref[hd] = ws * zk
        dec_ref[hd] = jnp.broadcast_to(decay, (R, LANES))

    col_seq = lax.broadcasted_iota(jnp.int32, (HEAD_DIM, R), 1) // SEQ_PAD

    def per_sequence(b, carry):
        r0 = pl.multiple_of(b * SEQ_PAD, SEQ_PAD)
        for hd in range(HEADS):
            c_old = c_ref[b, hd]
            n_old = n_ref[b, hd:hd + 1, :]
            q2 = q_ref[hd, pl.ds(r0, 2 * SEQ_PAD), :]
            carried_ref[hd, pl.ds(r0, SEQ_PAD), :] = _dot(q2.astype(BF16), c_old.astype(BF16))[0:SEQ_PAD, :]
            qn = jnp.sum(q2[0:SEQ_PAD, :] * n_old, axis=1, keepdims=True)
            qn_ref[hd, pl.ds(r0, SEQ_PAD), :] = jnp.broadcast_to(qn, (SEQ_PAD, LANES))
            dec = dec_ref[hd, pl.ds(r0, 1), :]
            kt_b = jnp.where(col_seq == b, kt_ref[hd], 0.0).astype(BF16)
            c_dst[b, hd] = dec * c_old + _dot(kt_b, wv_ref[hd])
            n_out_ref[b, hd:hd + 1, :] = dec * n_old + jnp.sum(wk_ref[hd, pl.ds(r0, SEQ_PAD), :], axis=0,
                                                                keepdims=True)
        return carry

    lax.fori_loop(0, NB, per_sequence, 0, unroll=4)

    for hd in range(HEADS):
        c0 = hd * HEAD_DIM
        m_t, w_inter, num, den, zo = saved[hd][:5]
        num = num + carried_ref[hd] * w_inter
        den = den + qn_ref[hd][:, 0:1] * w_inter
        mix_parts.append(_head_out(num, den, m_t, g_mn_ref[:, c0:c0 + HEAD_DIM], zo))

    mix = jnp.concatenate(mix_parts, axis=1).astype(BF16)
    y_ref[...] = x + _rms(_dot(mix, w_out_ref[...]), g_post_ref[...])


N_MIXER_SAMPLE_INPUTS = 16


def _mixer_sample_kernel_inplace(*refs, seq_len):
    _mixer_sample_kernel(*refs[:N_MIXER_SAMPLE_INPUTS], *refs[N_MIXER_SAMPLE_INPUTS + 1:],
                         seq_len=seq_len, stacked_first=False)


def _mixer_sample(x, st, c, n, mrow, wts, seq_len, layer, c_stack):
    NB = SAMPLE_SEQS
    R = NB * SEQ_PAD
    depth, nseq = c.shape[0], c.shape[1]
    weights = (wts["g_mix_pre"], wts["w_in"], wts["w_gate"], wts["b_gate"], wts["w_conv_mix"],
               wts["b_conv_mix"], wts["g_conv_norm"], wts["b_conv_norm"], wts["g_mlstm_norm"],
               wts["w_out"], wts["g_mix_post"])
    rows = lambda width: pl.BlockSpec((R, width), lambda i: (i, 0))
    st_spec = pl.BlockSpec((NB, CONV_TAIL, CONV_CH), lambda i: (i, 0, 0))
    n_spec = pl.BlockSpec((NB, HEADS, HEAD_DIM), lambda i: (i, 0, 0))
    st_in = pl.BlockSpec((None, NB, CONV_TAIL, CONV_CH), lambda i: (layer, i, 0, 0))
    c_in = pl.BlockSpec((None, NB, HEADS, HEAD_DIM, HEAD_DIM), lambda i: (layer, i, 0, 0, 0))
    n_in = pl.BlockSpec((None, NB, HEADS, HEAD_DIM), lambda i: (layer, i, 0, 0))
    in_specs = [rows(D_MODEL), st_in, c_in, n_in, rows(LANES)] + [_layer_block(w.shape, layer) for w in weights]
    operands = (x, st, c, n, mrow, *weights)
    assert len(operands) == N_MIXER_SAMPLE_INPUTS
    if c_stack is None:
        body = functools.partial(_mixer_sample_kernel, seq_len=seq_len, stacked_first=True)
        c_out_spec = pl.BlockSpec((depth, NB, HEADS, HEAD_DIM, HEAD_DIM), lambda i: (0, i, 0, 0, 0))
        aliases = {}
    else:
        body = functools.partial(_mixer_sample_kernel_inplace, seq_len=seq_len)
        c_out_spec = c_in
        in_specs.append(pl.BlockSpec(memory_space=pl.ANY))
        operands = operands + (c_stack,)
        aliases = {N_MIXER_SAMPLE_INPUTS: 2}
    return pl.pallas_call(
        body,
        grid=(nseq // NB,),
        in_specs=in_specs,
        out_specs=(rows(D_MODEL), st_spec, c_out_spec, n_spec, rows(LANES)),
        out_shape=(jax.ShapeDtypeStruct(x.shape, F32), jax.ShapeDtypeStruct(st.shape[1:], F32),
                   jax.ShapeDtypeStruct(c.shape, F32), jax.ShapeDtypeStruct(n.shape[1:], F32),
                   jax.ShapeDtypeStruct(mrow.shape, F32)),
        input_output_aliases=aliases,
        scratch_shapes=[
            pltpu.VMEM((NB, SAMPLE_EXT_ROWS, CONV_CH), F32),
            pltpu.VMEM((HEADS, R + SEQ_PAD, HEAD_DIM), F32),
            pltpu.VMEM((HEADS, HEAD_DIM, R), F32),
            pltpu.VMEM((HEADS, R, HEAD_DIM), BF16),
            pltpu.VMEM((HEADS, R, HEAD_DIM), F32),
            pltpu.VMEM((HEADS, R, HEAD_DIM), F32),
            pltpu.VMEM((HEADS, R, LANES), F32),
            pltpu.VMEM((HEADS, R, LANES), F32),
        ],
        compiler_params=pltpu.CompilerParams(
            dimension_semantics=("arbitrary",), vmem_limit_bytes=VMEM_LIMIT),
        name="mixer_sample",
    )(*operands)


def _ffn_sample_kernel(x_ref, p_ref, st_ref, g_pre_ref, w_up_ref, w_conv_ref, b_conv_ref, w_down_ref,
                       g_post_ref, g_ple_ref, w_ple_ref, w_pg_ref,
                       y_ref, tail_out_ref,
                       ubuf_ref, f_ref, *, seq_len):
    NB = SAMPLE_SEQS_FFN
    R = NB * SEQ_PAD
    x = x_ref[...]
    h = _rms(x, g_pre_ref[...]).astype(BF16)
    lo = SEQ_PAD - FFN_TAIL
    for c in range(D_FF // FFN_CHUNK):
        halves = []
        for half in range(2):
            c0 = half * D_FF + c * FFN_CHUNK
            cs = slice(c0, c0 + FFN_CHUNK)
            u = _dot(h, w_up_ref[:, cs])
            ubuf_ref[half, :, lo:SEQ_PAD, :] = st_ref[:, :, cs]
            ubuf_ref[half, :, SEQ_PAD:2 * SEQ_PAD, :] = u.reshape(NB, SEQ_PAD, FFN_CHUNK)
            tail_out_ref[:, :, cs] = ubuf_ref[half, :, lo + seq_len:SEQ_PAD + seq_len, :]
            y = (w_conv_ref[0:1, cs][None] * ubuf_ref[half, :, pl.ds(lo, SEQ_PAD), :]
                 + w_conv_ref[1:2, cs][None] * ubuf_ref[half, :, pl.ds(lo + 1, SEQ_PAD), :]
                 + w_conv_ref[2:3, cs][None] * ubuf_ref[half, :, pl.ds(lo + 2, SEQ_PAD), :]
                 + b_conv_ref[:, cs][None])
            halves.append(y.reshape(R, FFN_CHUNK))
        f_ref[:, c * FFN_CHUNK:(c + 1) * FFN_CHUNK] = (_gelu_tanh(halves[0]) * halves[1]).astype(BF16)
    y_ref[...] = _ffn_tail(x, f_ref, p_ref[...], w_down_ref, g_post_ref, g_ple_ref, w_ple_ref, w_pg_ref)


def _ffn_sample(x, p, st, wts, seq_len, layer):
    NB = SAMPLE_SEQS_FFN
    R = NB * SEQ_PAD
    nseq = st.shape[1]
    weights = (wts["g_ffn_pre"], wts["w_up"], wts["w_conv_ffn"], wts["b_conv_ffn"], wts["w_down"],
               wts["g_ffn_post"], wts["g_ple"], wts["w_ple"], wts["w_ple_gate"])
    st_spec = pl.BlockSpec((NB, FFN_TAIL, 2 * D_FF), lambda i: (i, 0, 0))
    st_in = pl.BlockSpec((None, NB, FFN_TAIL, 2 * D_FF), lambda i: (layer, i, 0, 0))
    return pl.pallas_call(
        functools.partial(_ffn_sample_kernel, seq_len=seq_len),
        grid=(nseq // NB,),
        in_specs=[pl.BlockSpec((R, D_MODEL), lambda i: (i, 0)), pl.BlockSpec((R, PLE_DIM), lambda i: (i, 0)),
                  st_in] + [_layer_block(w.shape, layer) for w in weights],
        out_specs=(pl.BlockSpec((R, D_MODEL), lambda i: (i, 0)), st_spec),
        out_shape=(jax.ShapeDtypeStruct(x.shape, F32), jax.ShapeDtypeStruct(st.shape[1:], F32)),
        scratch_shapes=[
            pltpu.VMEM((2, NB, 2 * SEQ_PAD, FFN_CHUNK), F32),
            pltpu.VMEM((R, D_FF), BF16),
        ],
        compiler_params=pltpu.CompilerParams(
            dimension_semantics=("arbitrary",), vmem_limit_bytes=VMEM_LIMIT),
        name="ffn_sample",
    )(x, p, st, *weights)


def _pad_seq(a):
    nseq, seq_len, width = a.shape
    return jnp.pad(a, ((0, 0), (0, SEQ_PAD - seq_len), (0, 0))).reshape(nseq * SEQ_PAD, width)


def _stacked_weights(g_mix_pre, w_in, b_igate, b_fgate, w_conv_mix, b_conv_mix, g_conv_norm,
                     b_conv_norm, g_mlstm_norm, w_out, g_mix_post, g_ffn_pre, w_up, w_conv_ffn,
                     b_conv_ffn, w_down, g_ffn_post, g_ple, w_ple, w_ple_gate):
    row = lambda v: v[:, None, :].astype(F32)
    n_gate = 2 * HEADS
    w_gate = jnp.pad(w_in[:, :, MAIN_COLS:], ((0, 0), (0, 0), (0, LANES - n_gate))).astype(BF16)
    b_gate = jnp.pad(jnp.concatenate([b_igate, b_fgate], axis=1), ((0, 0), (0, LANES - n_gate)))
    return {
        "g_mix_pre": row(g_mix_pre), "w_in": w_in[:, :, :MAIN_COLS].astype(BF16),
        "w_gate": w_gate, "b_gate": row(b_gate),
        "w_conv_mix": w_conv_mix.astype(F32), "b_conv_mix": row(b_conv_mix),
        "g_conv_norm": row(g_conv_norm), "b_conv_norm": row(b_conv_norm),
        "g_mlstm_norm": row(g_mlstm_norm), "w_out": w_out.astype(BF16), "g_mix_post": row(g_mix_post),
        "g_ffn_pre": row(g_ffn_pre), "w_up": w_up.astype(BF16), "w_conv_ffn": w_conv_ffn.astype(F32),
        "b_conv_ffn": row(b_conv_ffn), "w_down": w_down.astype(BF16), "g_ffn_post": row(g_ffn_post),
        "g_ple": row(g_ple), "w_ple": w_ple.astype(BF16), "w_ple_gate": w_ple_gate.astype(BF16),
    }


def kernel(x_prompt, x_sample, p_prompt, p_sample, state_conv_mix, state_mlstm_C, state_mlstm_n, state_mlstm_m, state_conv_ffn, g_mix_pre, w_in, b_igate, b_fgate, w_conv_mix, b_conv_mix, g_conv_norm, b_conv_norm, g_mlstm_norm, w_out, g_mix_post, g_ffn_pre, w_up, w_conv_ffn, b_conv_ffn, w_down, g_ffn_post, g_ple, w_ple, w_ple_gate):
    depth = w_in.shape[0]
    nseq, seq_len, _ = x_sample.shape
    assert FFN_TAIL <= seq_len <= SEQ_PAD and nseq % SAMPLE_SEQS == 0 and nseq % SAMPLE_SEQS_FFN == 0
    assert x_prompt.shape[1] % PROMPT_TILE == 0 and CONV_TAIL <= HIST <= PROMPT_TILE
    xp = x_prompt
    xs = _pad_seq(x_sample)
    pc, pC, pn, pm, pf = [], [], [], [], []
    sc, sn, sm, sf = [], [], [], []
    c_stack = None
    wts = _stacked_weights(g_mix_pre, w_in, b_igate, b_fgate, w_conv_mix, b_conv_mix, g_conv_norm,
                           b_conv_norm, g_mlstm_norm, w_out, g_mix_post, g_ffn_pre, w_up, w_conv_ffn,
                           b_conv_ffn, w_down, g_ffn_post, g_ple, w_ple, w_ple_gate)
    for l in range(depth):
        xp, c1, C1, n1, m1, f1 = _layer_prompt(xp, p_prompt, wts, l)
        pc.append(c1); pC.append(C1); pn.append(n1); pm.append(m1[:, :HEADS, 0]); pf.append(f1)

        mrow = jnp.pad(jnp.repeat(state_mlstm_m[l].astype(F32), SEQ_PAD, axis=0),
                       ((0, 0), (HEADS, LANES - 2 * HEADS)))
        xs, c2, c_stack, n2, m2 = _mixer_sample(xs, state_conv_mix, state_mlstm_C, state_mlstm_n, mrow,
                                                wts, seq_len, l, c_stack)
        xs, f2 = _ffn_sample(xs, _pad_seq(p_sample[l]), state_conv_ffn, wts, seq_len, l)
        sc.append(c2); sn.append(n2); sf.append(f2)
        sm.append(m2.reshape(nseq, SEQ_PAD, LANES)[:, 0, :HEADS])
    ys = xs.reshape(nseq, SEQ_PAD, D_MODEL)[:, :seq_len]
    return (xp, ys, jnp.stack(pc), jnp.stack(pC), jnp.stack(pn), jnp.stack(pm), jnp.stack(pf),
            jnp.stack(sc), c_stack, jnp.stack(sn), jnp.stack(sm), jnp.stack(sf))
```

```python
import functools

import jax
import jax.numpy as jnp
from jax import lax
from jax.experimental import pallas as pl
from jax.experimental.pallas import tpu as pltpu

F32 = jnp.float32
BF16 = jnp.bfloat16

D_MODEL = 1024
CONV_CH = 512
CONV_WIDTH = 31
CONV_TAIL = CONV_WIDTH - 1
CONV_GROUPS = 4
HEADS = 4
HEAD_DIM = 128
MLSTM_WIDTH = HEADS * HEAD_DIM
D_FF = 2816
FFN_TAIL = 2
PLE_DIM = 256
EPS = 1e-6
MAIN_COLS = 2 * CONV_CH + 4 * MLSTM_WIDTH
LANES = 128
SUBLANES = 8
NEG = -1e30

PROMPT_TILE = 256
SAMPLE_SEQS = 16
SAMPLE_SEQS_FFN = 32
SEQ_PAD = SUBLANES
HIST = 32
SHIFT_ROWS = PROMPT_TILE + HIST - SUBLANES
CONV_ROWS = 64
CONV_IN_FLIGHT = 2
FFN_CHUNK = 256
SAMPLE_EXT_ROWS = -(-(CONV_TAIL + SEQ_PAD) // SUBLANES) * SUBLANES
VMEM_LIMIT = 56 * 1024 * 1024


def _dot(a, b):
    return jnp.dot(a, b, preferred_element_type=F32)


def _dot_exact(sel, x):
    hi = x.astype(BF16)
    r1 = x - hi.astype(F32)
    mid = r1.astype(BF16)
    lo = (r1 - mid.astype(F32)).astype(BF16)
    y = _dot(jnp.where(sel, 1.0, 0.0).astype(BF16), jnp.concatenate([hi, mid, lo], axis=1))
    return y[:, 0:LANES] + y[:, LANES:2 * LANES] + y[:, 2 * LANES:3 * LANES]


def _rms(x, g):
    ms = jnp.mean(x * x, axis=-1, keepdims=True)
    return x * lax.rsqrt(ms + EPS) * g


def _layernorm(x):
    mu = jnp.mean(x, axis=-1, keepdims=True)
    xc = x - mu
    var = jnp.mean(xc * xc, axis=-1, keepdims=True)
    return xc * lax.rsqrt(var + EPS)


def _sigmoid(x):
    return 1.0 / (1.0 + jnp.exp(-x))


def _log_sigmoid(x):
    return jnp.minimum(x, 0.0) - jnp.log(1.0 + jnp.exp(-jnp.abs(x)))


def _exact_zero(v):
    bits = pltpu.bitcast(v, jnp.uint32)
    bits = lax.shift_right_logical(lax.shift_right_logical(bits, jnp.uint32(16)), jnp.uint32(16))
    return pltpu.bitcast(bits, F32)


def _gelu_tanh(x):
    return 0.5 * x * (1.0 + jnp.tanh(0.7978845608028654 * (x + 0.044715 * (x * x * x))))


def _conv_branch_post(acc, g_ref, b_ref):
    parts = []
    for g in range(CONV_GROUPS):
        sl = slice(g * LANES, (g + 1) * LANES)
        y = _layernorm(acc[:, sl]) * g_ref[:, sl] + b_ref[:, sl]
        parts.append(y * _sigmoid(y))
    return parts


def _qk(q_bf, k_bf):
    return lax.dot_general(q_bf, k_bf, (((1,), (1,)), ((), ())), preferred_element_type=F32)


def _mlstm_weighted(scores, v_bf, d, inter):
    m_t = jnp.maximum(inter, jnp.max(d, axis=1, keepdims=True))
    w_intra = jnp.exp(d - m_t)
    w_inter = jnp.exp(inter - m_t)
    s = scores * w_intra
    num = _dot(s.astype(BF16), v_bf)
    den = jnp.sum(s, axis=1, keepdims=True)
    return m_t, w_inter, num, den


def _head_out(num, den, m_t, g_mn, zo):
    hh = num / jnp.maximum(jnp.abs(den), jnp.exp(-m_t))
    return _layernorm(hh) * g_mn * _sigmoid(zo)


def _ffn_tail(x, f_ref, p, w_down_ref, g_post_ref, g_ple_ref, w_ple_ref, w_pg_ref):
    x2 = x + _rms(_dot(f_ref[...], w_down_ref[...]), g_post_ref[...])
    emb = _dot(p.astype(BF16), w_ple_ref[...])
    gate = _sigmoid(_dot(_rms(x2, g_ple_ref[...]).astype(BF16), w_pg_ref[...]))
    return x2 + emb * gate


def _layer_block(shape, layer):
    n = len(shape)
    return pl.BlockSpec((None,) + tuple(shape[1:]), lambda *_: (layer,) + (0,) * (n - 1),
                        pipeline_mode=pl.Buffered(1))


def _layer_prompt_kernel(x_ref, p_ref,
                         g_pre_ref, w_in_ref, w_gate_ref, b_gate_ref, w_cm_ref, b_cm_ref,
                         g_cn_ref, b_cn_ref, g_mn_ref, w_out_ref, g_post_ref,
                         g_fpre_ref, w_up_ref, w_cf_ref, b_cf_ref, w_down_ref, g_fpost_ref,
                         g_ple_ref, w_ple_ref, w_pg_ref,
                         y_ref, conv_out_ref, c_out_ref, n_out_ref, m_out_ref, tail_out_ref,
                         ext_ref, sh_ref, mix_ref, cn_ref, m_ref, h_ref, z_ref, x1_ref, hist_ref, fw_ref, f_ref,
                         *, tiles_per_seq):
    T = PROMPT_TILE
    G = T // SUBLANES
    i = pl.program_id(0)
    n_tiles = pl.num_programs(0) - 1
    s_mix = lax.rem(jnp.minimum(i, n_tiles - 1), tiles_per_seq)
    s_ffn = lax.rem(jnp.maximum(i - 1, 0), tiles_per_seq)
    slot = lax.rem(i, 2)

    @pl.when(i == 0)
    def _():
        x1_ref[...] = jnp.zeros(x1_ref.shape, F32)

    @pl.when(s_mix == 0)
    def _():
        ext_ref[0:HIST, :] = jnp.zeros((HIST, CONV_CH), F32)
        cn_ref[...] = jnp.zeros(cn_ref.shape, F32)
        m_ref[...] = jnp.zeros(m_ref.shape, F32)

    @pl.when(s_ffn == 0)
    def _():
        hist_ref[...] = jnp.zeros(hist_ref.shape, F32)

    wr = lax.broadcasted_iota(jnp.int32, (T, T), 0)
    wc = lax.broadcasted_iota(jnp.int32, (T, T), 1)
    to_work = jnp.where(wc == (wr % SUBLANES) * G + wr // SUBLANES, 1.0, 0.0).astype(BF16)
    to_token = jnp.where(wr == (wc % SUBLANES) * G + wc // SUBLANES, 1.0, 0.0).astype(BF16)

    def moved_down(group, prev_group):
        first = lax.broadcasted_iota(jnp.int32, group.shape, 0) == 0
        return jnp.where(first, pltpu.roll(prev_group, 1, 0), pltpu.roll(group, 1, 0))

    x = x_ref[...]
    h_ref[...] = _rms(x, g_pre_ref[...]).astype(BF16)
    zv = _dot(h_ref[...], w_in_ref[:, 0:CONV_CH])
    zg = _dot(h_ref[...], w_in_ref[:, CONV_CH:2 * CONV_CH])
    gates = _dot(h_ref[...], w_gate_ref[...]) + b_gate_ref[...]
    q_off = 2 * CONV_CH
    for part in range(4):
        ps = slice(part * MLSTM_WIDTH, (part + 1) * MLSTM_WIDTH)
        z_ref[slot, :, ps] = _dot(h_ref[...],
                                  w_in_ref[:, q_off + part * MLSTM_WIDTH:q_off + (part + 1) * MLSTM_WIDTH])

    ext_ref[HIST:HIST + T, :] = zv * _sigmoid(zg)
    for r in range(1, SUBLANES):
        sh_ref[r - 1] = ext_ref[pl.ds(r, SHIFT_ROWS), :]
    recent = []
    for g in range(CONV_GROUPS):
        cs = slice(g * LANES, (g + 1) * LANES)
        for rb in range(T // CONV_ROWS):
            acc = jnp.broadcast_to(b_cm_ref[:, cs], (CONV_ROWS, LANES))
            for j in range(CONV_WIDTH):
                off = HIST - CONV_TAIL + j
                r, base = off % SUBLANES, rb * CONV_ROWS + off - off % SUBLANES
                src = ext_ref if r == 0 else sh_ref.at[r - 1]
                w_row = w_cm_ref[j:j + 1, cs]
                if j == 0 and len(recent) == CONV_IN_FLIGHT:
                    w_row = w_row + _exact_zero(recent.pop(0))
                acc = acc + w_row * src[base:base + CONV_ROWS, cs]
            y = _layernorm(acc) * g_cn_ref[:, cs] + b_cn_ref[:, cs]
            mix_ref[rb * CONV_ROWS:(rb + 1) * CONV_ROWS, cs] = (y * _sigmoid(y)).astype(BF16)
            recent.append(acc[0:1, :])
    ext_ref[0:HIST, :] = ext_ref[T:T + HIST, :]

    x1 = x1_ref[1 - slot]
    h2 = _rms(x1, g_fpre_ref[...]).astype(BF16)
    hp = _dot(to_work, h2).astype(BF16)
    for c in range(D_FF // FFN_CHUNK):
        halves = []
        for half in range(2):
            c0 = half * D_FF + c * FFN_CHUNK
            cs = slice(c0, c0 + FFN_CHUNK)
            u = _dot(hp, w_up_ref[:, cs])
            prev = hist_ref[:, cs]
            hist_ref[:, cs] = u[T - 2 * SUBLANES:T, :]
            w_last = moved_down(u[T - SUBLANES:T, :], prev[SUBLANES:2 * SUBLANES, :])
            w_last2 = moved_down(u[T - 2 * SUBLANES:T - SUBLANES, :], prev[0:SUBLANES, :])
            u1 = jnp.concatenate([w_last, u[0:T - SUBLANES, :]], axis=0)
            u2 = jnp.concatenate([w_last2, w_last, u[0:T - 2 * SUBLANES, :]], axis=0)
            halves.append(w_cf_ref[0:1, cs] * u2 + w_cf_ref[1:2, cs] * u1
                          + w_cf_ref[2:3, cs] * u + b_cf_ref[:, cs])
        fw_ref[:, c * FFN_CHUNK:(c + 1) * FFN_CHUNK] = (_gelu_tanh(halves[0]) * halves[1]).astype(BF16)

    causal = wc <= wr
    bcum = _dot_exact(causal, _log_sigmoid(gates))
    gates_t = gates.T
    bcum_t = bcum.T
    heads = []
    for hd in range(HEADS):
        c0 = hd * HEAD_DIM
        zvv = z_ref[slot, :, 2 * MLSTM_WIDTH + c0:2 * MLSTM_WIDTH + c0 + HEAD_DIM]
        q_bf = (z_ref[slot, :, c0:c0 + HEAD_DIM] * (HEAD_DIM ** -0.5)).astype(BF16)
        k_bf = z_ref[slot, :, MLSTM_WIDTH + c0:MLSTM_WIDTH + c0 + HEAD_DIM].astype(BF16)
        cn = cn_ref[hd]
        heads.append((zvv, k_bf, cn, _qk(q_bf, k_bf), _dot(q_bf, cn.astype(BF16))))

    finished = []
    for hd in range(HEADS):
        zvv, k_bf, cn, scores, carried = heads[hd]
        i_row = gates_t[hd:hd + 1, :]
        i_col = gates[:, hd:hd + 1]
        b_row = bcum_t[HEADS + hd:HEADS + hd + 1, :]
        b_col = bcum[:, HEADS + hd:HEADS + hd + 1]
        m_prev = m_ref[hd:hd + 1, 0:1]
        d = jnp.where(causal, b_col - b_row + i_row, NEG)
        m_t, w_inter, num, den = _mlstm_weighted(scores, zvv.astype(BF16), d, b_col + m_prev)
        m_new = m_t[T - 1:T, :]
        ws = jnp.exp(b_col[T - 1:T, :] - b_col + i_col - m_new)
        vp = jnp.concatenate([ws * zvv, jnp.broadcast_to(ws, (T, HEAD_DIM))], axis=1).astype(BF16)
        kv = lax.dot_general(k_bf, vp, (((0,), (0,)), ((), ())), preferred_element_type=F32)
        finished.append((m_t, w_inter, num, den, m_new, w_inter[T - 1:T, :], kv))

    for hd in range(HEADS):
        c0 = hd * HEAD_DIM
        _, _, cn, _, carried = heads[hd]
        m_t, w_inter, num, den, m_new, decay, kv = finished[hd]
        zo = z_ref[slot, :, 3 * MLSTM_WIDTH + c0:3 * MLSTM_WIDTH + c0 + HEAD_DIM]
        num = num + carried[:, 0:HEAD_DIM] * w_inter
        den = den + carried[:, HEAD_DIM:HEAD_DIM + 1] * w_inter
        mix_ref[:, CONV_CH + c0:CONV_CH + c0 + HEAD_DIM] = _head_out(
            num, den, m_t, g_mn_ref[:, c0:c0 + HEAD_DIM], zo).astype(BF16)
        cn_ref[hd] = decay * cn + kv
        m_ref[hd:hd + 1, :] = jnp.broadcast_to(m_new, (1, LANES))

    for c in range(D_FF // FFN_CHUNK):
        fs = slice(c * FFN_CHUNK, (c + 1) * FFN_CHUNK)
        f_ref[:, fs] = _dot(to_token, fw_ref[:, fs]).astype(BF16)
    y_ref[...] = _ffn_tail(x1, f_ref, p_ref[...], w_down_ref, g_fpost_ref, g_ple_ref, w_ple_ref, w_pg_ref)

    x1_ref[slot] = x + _rms(_dot(mix_ref[...], w_out_ref[...]), g_post_ref[...])

    @pl.when((s_mix == tiles_per_seq - 1) & (i < n_tiles))
    def _():
        conv_out_ref[...] = ext_ref[pl.ds(HIST - CONV_TAIL, CONV_TAIL), :]
        m_out_ref[...] = m_ref[...]
        for hd in range(HEADS):
            cn = cn_ref[hd]
            c_out_ref[hd] = cn[:, 0:HEAD_DIM]
            n_out_ref[hd:hd + 1, :] = cn[:, HEAD_DIM:].T[0:1, :]

    @pl.when((s_ffn == tiles_per_seq - 1) & (i >= 1))
    def _():
        tail_out_ref[0:1, :] = hist_ref[SUBLANES - 1:SUBLANES, :]
        tail_out_ref[1:2, :] = hist_ref[2 * SUBLANES - 1:2 * SUBLANES, :]


def _layer_prompt(x, p, wts, layer):
    B, S, D = x.shape
    T = PROMPT_TILE
    tps = S // T
    n_tiles = B * tps
    weights = (wts["g_mix_pre"], wts["w_in"], wts["w_gate"], wts["b_gate"], wts["w_conv_mix"],
               wts["b_conv_mix"], wts["g_conv_norm"], wts["b_conv_norm"], wts["g_mlstm_norm"],
               wts["w_out"], wts["g_mix_post"],
               wts["g_ffn_pre"], wts["w_up"], wts["w_conv_ffn"], wts["b_conv_ffn"], wts["w_down"],
               wts["g_ffn_post"], wts["g_ple"], wts["w_ple"], wts["w_ple_gate"])
    mix_tile = lambda i: jnp.minimum(i, n_tiles - 1)
    ffn_tile = lambda i: jnp.maximum(i - 1, 0)
    out_shape = (
        jax.ShapeDtypeStruct((B, S, D), F32),
        jax.ShapeDtypeStruct((B, CONV_TAIL, CONV_CH), F32),
        jax.ShapeDtypeStruct((B, HEADS, HEAD_DIM, HEAD_DIM), F32),
        jax.ShapeDtypeStruct((B, HEADS, HEAD_DIM), F32),
        jax.ShapeDtypeStruct((B, SUBLANES, LANES), F32),
        jax.ShapeDtypeStruct((B, FFN_TAIL, 2 * D_FF), F32),
    )
    out_specs = (
        pl.BlockSpec((None, T, D), lambda i: (ffn_tile(i) // tps, ffn_tile(i) % tps, 0)),
        pl.BlockSpec((None, CONV_TAIL, CONV_CH), lambda i: (mix_tile(i) // tps, 0, 0)),
        pl.BlockSpec((None, HEADS, HEAD_DIM, HEAD_DIM), lambda i: (mix_tile(i) // tps, 0, 0, 0)),
        pl.BlockSpec((None, HEADS, HEAD_DIM), lambda i: (mix_tile(i) // tps, 0, 0)),
        pl.BlockSpec((None, SUBLANES, LANES), lambda i: (mix_tile(i) // tps, 0, 0)),
        pl.BlockSpec((None, FFN_TAIL, 2 * D_FF), lambda i: (ffn_tile(i) // tps, 0, 0)),
    )
    return pl.pallas_call(
        functools.partial(_layer_prompt_kernel, tiles_per_seq=tps),
        grid=(n_tiles + 1,),
        in_specs=[pl.BlockSpec((None, T, D), lambda i: (mix_tile(i) // tps, mix_tile(i) % tps, 0)),
                  pl.BlockSpec((None, None, T, PLE_DIM),
                               lambda i: (layer, ffn_tile(i) // tps, ffn_tile(i) % tps, 0))]
                 + [_layer_block(w.shape, layer) for w in weights],
        out_specs=out_specs,
        out_shape=out_shape,
        scratch_shapes=[
            pltpu.VMEM((HIST + T, CONV_CH), F32),
            pltpu.VMEM((SUBLANES - 1, SHIFT_ROWS, CONV_CH), F32),
            pltpu.VMEM((T, D_MODEL), BF16),
            pltpu.VMEM((HEADS, HEAD_DIM, 2 * HEAD_DIM), F32),
            pltpu.VMEM((SUBLANES, LANES), F32),
            pltpu.VMEM((T, D_MODEL), BF16),
            pltpu.VMEM((2, T, 4 * MLSTM_WIDTH), F32),
            pltpu.VMEM((2, T, D_MODEL), F32),
            pltpu.VMEM((2 * SUBLANES, 2 * D_FF), F32),
            pltpu.VMEM((T, D_FF), BF16),
            pltpu.VMEM((T, D_FF), BF16),
        ],
        compiler_params=pltpu.CompilerParams(
            dimension_semantics=("arbitrary",), vmem_limit_bytes=VMEM_LIMIT),
        name="layer_prompt",
    )(x, p, *weights)


def _mixer_sample_kernel(x_ref, st_ref, c_ref, n_ref, mrow_ref,
                         g_pre_ref, w_in_ref, w_gate_ref, b_gate_ref, w_conv_ref, b_conv_ref,
                         g_cn_ref, b_cn_ref, g_mn_ref, w_out_ref, g_post_ref,
                         y_ref, conv_out_ref, c_out_ref, n_out_ref, m_out_ref,
                         ext_ref, q_ref, kt_ref, wv_ref, wk_ref, carried_ref, qn_ref, dec_ref,
                         *, seq_len, stacked_first):
    NB = SAMPLE_SEQS
    R = NB * SEQ_PAD
    if stacked_first:
        c_dst, conv_dst = c_out_ref.at[0], conv_out_ref.at[0]
        for ref in (c_out_ref, conv_out_ref):
            ref[1:] = jnp.zeros((ref.shape[0] - 1,) + tuple(ref.shape[1:]), F32)
    else:
        c_dst, conv_dst = c_out_ref, conv_out_ref
    x = x_ref[...]
    h = _rms(x, g_pre_ref[...]).astype(BF16)

    zv = _dot(h, w_in_ref[:, 0:CONV_CH])
    zg = _dot(h, w_in_ref[:, CONV_CH:2 * CONV_CH])
    a = zv * _sigmoid(zg)
    ext_ref[:, 0:CONV_TAIL, :] = st_ref[...]
    ext_ref[:, CONV_TAIL:CONV_TAIL + SEQ_PAD, :] = a.reshape(NB, SEQ_PAD, CONV_CH)
    ext_ref[:, CONV_TAIL + SEQ_PAD:, :] = jnp.zeros((NB, SAMPLE_EXT_ROWS - CONV_TAIL - SEQ_PAD, CONV_CH), F32)
    acc = jnp.broadcast_to(b_conv_ref[...][None], (NB, SEQ_PAD, CONV_CH))
    for j in range(CONV_WIDTH):
        acc = acc + w_conv_ref[j:j + 1, :][None] * ext_ref[:, pl.ds(j, SEQ_PAD), :]
    conv_dst[...] = ext_ref[:, seq_len:seq_len + CONV_TAIL, :]
    mix_parts = _conv_branch_post(acc.reshape(R, CONV_CH), g_cn_ref, b_cn_ref)

    gates = _dot(h, w_gate_ref[...]) + b_gate_ref[...]
    row = lax.broadcasted_iota(jnp.int32, (R, R), 0)
    col = lax.broadcasted_iota(jnp.int32, (R, R), 1)
    same_seq = (row // SEQ_PAD) == (col // SEQ_PAD)
    causal = same_seq & (col <= row)
    bcum = _dot_exact(causal, _log_sigmoid(gates))
    mask = causal & ((col % SEQ_PAD) < seq_len)
    pick_last = same_seq & ((col % SEQ_PAD) == seq_len - 1)
    inter_all = bcum + mrow_ref[...]
    gates_t = gates.T
    bcum_t = bcum.T
    lane = lax.broadcasted_iota(jnp.int32, (R, LANES), 1)
    row_valid = (lax.broadcasted_iota(jnp.int32, (R, 1), 0) % SEQ_PAD) < seq_len
    stats = jnp.where((lane >= HEADS) & (lane < 2 * HEADS), bcum, 0.0)
    q_off = 2 * CONV_CH
    z_parts = [_dot(h, w_in_ref[:, q_off + part * MLSTM_WIDTH:q_off + (part + 1) * MLSTM_WIDTH])
               for part in range(4)]
    saved = []
    for hd in range(HEADS):
        c0 = hd * HEAD_DIM
        zq = z_parts[0][:, c0:c0 + HEAD_DIM] * (HEAD_DIM ** -0.5)
        zk = z_parts[1][:, c0:c0 + HEAD_DIM]
        zvv = z_parts[2][:, c0:c0 + HEAD_DIM]
        zo = z_parts[3][:, c0:c0 + HEAD_DIM]
        i_row = gates_t[hd:hd + 1, :]
        b_row = bcum_t[HEADS + hd:HEADS + hd + 1, :]
        b_col = bcum[:, HEADS + hd:HEADS + hd + 1]
        d = jnp.where(mask, b_col - b_row + i_row, NEG)
        inter = inter_all[:, HEADS + hd:HEADS + hd + 1]
        m_t, w_inter, num, den = _mlstm_weighted(_qk(zq.astype(BF16), zk.astype(BF16)), zvv.astype(BF16), d, inter)
        stats = jnp.where(lane == hd, m_t, stats)
        stats = jnp.where(lane == 2 * HEADS + hd, w_inter, stats)
        q_ref[hd, 0:R, :] = zq
        q_ref[hd, R:R + SEQ_PAD, :] = jnp.zeros((SEQ_PAD, HEAD_DIM), F32)
        kt_ref[hd] = zk.T
        saved.append((m_t, w_inter, num, den, zo, zk, zvv))

    per_seq = _dot_exact(pick_last, stats)
    m_out_ref[...] = per_seq
    for hd in range(HEADS):
        zk, zvv = saved[hd][5], saved[hd][6]
        m_new = per_seq[:, hd:hd + 1]
        b_last = per_seq[:, HEADS + hd:HEADS + hd + 1]
        decay = per_seq[:, 2 * HEADS + hd:2 * HEADS + hd + 1]
        b_col = bcum[:, HEADS + hd:HEADS + hd + 1]
        i_col = gates[:, hd:hd + 1]
        ws = jnp.where(row_valid, jnp.exp(b_last - b_col + i_col - m_new), 0.0)
        wv_ref[hd] = (ws * zvv).astype(BF16)
        wk_ref[hd] = ws * zk
        dec_ref[hd] = jnp.broadcast_to(decay, (R, LANES))

    col_seq = lax.broadcasted_iota(jnp.int32, (HEAD_DIM, R), 1) // SEQ_PAD

    def per_sequence(b, carry):
        r0 = pl.multiple_of(b * SEQ_PAD, SEQ_PAD)
        for hd in range(HEADS):
            c_old = c_ref[b, hd]
            n_old = n_ref[b, hd:hd + 1, :]
            q2 = q_ref[hd, pl.ds(r0, 2 * SEQ_PAD), :]
            carried_ref[hd, pl.ds(r0, SEQ_PAD), :] = _dot(q2.astype(BF16), c_old.astype(BF16))[0:SEQ_PAD, :]
            qn = jnp.sum(q2[0:SEQ_PAD, :] * n_old, axis=1, keepdims=True)
            qn_ref[hd, pl.ds(r0, SEQ_PAD), :] = jnp.broadcast_to(qn, (SEQ_PAD, LANES))
            dec = dec_ref[hd, pl.ds(r0, 1), :]
            kt_b = jnp.where(col_seq == b, kt_ref[hd], 0.0).astype(BF16)
            c_dst[b, hd] = dec * c_old + _dot(kt_b, wv_ref[hd])
            n_out_ref[b, hd:hd + 1, :] = dec * n_old + jnp.sum(wk_ref[hd, pl.ds(r0, SEQ_PAD), :], axis=0,
                                                                keepdims=True)
        return carry

    lax.fori_loop(0, NB, per_sequence, 0, unroll=4)

    for hd in range(HEADS):
        c0 = hd * HEAD_DIM
        m_t, w_inter, num, den, zo = saved[hd][:5]
        num = num + carried_ref[hd] * w_inter
        den = den + qn_ref[hd][:, 0:1] * w_inter
        mix_parts.append(_head_out(num, den, m_t, g_mn_ref[:, c0:c0 + HEAD_DIM], zo))

    mix = jnp.concatenate(mix_parts, axis=1).astype(BF16)
    y_ref[...] = x + _rms(_dot(mix, w_out_ref[...]), g_post_ref[...])


N_MIXER_SAMPLE_INPUTS = 16
N_STACKED = 2


def _mixer_sample_kernel_inplace(*refs, seq_len):
    _mixer_sample_kernel(*refs[:N_MIXER_SAMPLE_INPUTS], *refs[N_MIXER_SAMPLE_INPUTS + N_STACKED:],
                         seq_len=seq_len, stacked_first=False)


def _mixer_sample(x, st, c, n, mrow, wts, seq_len, layer, stacks):
    NB = SAMPLE_SEQS
    R = NB * SEQ_PAD
    depth, nseq = c.shape[0], c.shape[1]
    weights = (wts["g_mix_pre"], wts["w_in"], wts["w_gate"], wts["b_gate"], wts["w_conv_mix"],
               wts["b_conv_mix"], wts["g_conv_norm"], wts["b_conv_norm"], wts["g_mlstm_norm"],
               wts["w_out"], wts["g_mix_post"])
    rows = lambda width: pl.BlockSpec((R, width), lambda i: (i, 0))
    n_spec = pl.BlockSpec((NB, HEADS, HEAD_DIM), lambda i: (i, 0, 0))
    st_in = pl.BlockSpec((None, NB, CONV_TAIL, CONV_CH), lambda i: (layer, i, 0, 0))
    c_in = pl.BlockSpec((None, NB, HEADS, HEAD_DIM, HEAD_DIM), lambda i: (layer, i, 0, 0, 0))
    n_in = pl.BlockSpec((None, NB, HEADS, HEAD_DIM), lambda i: (layer, i, 0, 0))
    in_specs = [rows(D_MODEL), st_in, c_in, n_in, rows(LANES)] + [_layer_block(w.shape, layer) for w in weights]
    operands = (x, st, c, n, mrow, *weights)
    assert len(operands) == N_MIXER_SAMPLE_INPUTS
    if stacks is None:
        body = functools.partial(_mixer_sample_kernel, seq_len=seq_len, stacked_first=True)
        st_out_spec = pl.BlockSpec((depth, NB, CONV_TAIL, CONV_CH), lambda i: (0, i, 0, 0))
        c_out_spec = pl.BlockSpec((depth, NB, HEADS, HEAD_DIM, HEAD_DIM), lambda i: (0, i, 0, 0, 0))
        aliases = {}
    else:
        body = functools.partial(_mixer_sample_kernel_inplace, seq_len=seq_len)
        st_out_spec, c_out_spec = st_in, c_in
        in_specs += [pl.BlockSpec(memory_space=pl.ANY)] * N_STACKED
        operands = operands + tuple(stacks)
        aliases = {N_MIXER_SAMPLE_INPUTS: 1, N_MIXER_SAMPLE_INPUTS + 1: 2}
    return pl.pallas_call(
        body,
        grid=(nseq // NB,),
        in_specs=in_specs,
        out_specs=(rows(D_MODEL), st_out_spec, c_out_spec, n_spec, rows(LANES)),
        out_shape=(jax.ShapeDtypeStruct(x.shape, F32), jax.ShapeDtypeStruct(st.shape, F32),
                   jax.ShapeDtypeStruct(c.shape, F32), jax.ShapeDtypeStruct(n.shape[1:], F32),
                   jax.ShapeDtypeStruct(mrow.shape, F32)),
        input_output_aliases=aliases,
        scratch_shapes=[
            pltpu.VMEM((NB, SAMPLE_EXT_ROWS, CONV_CH), F32),
            pltpu.VMEM((HEADS, R + SEQ_PAD, HEAD_DIM), F32),
            pltpu.VMEM((HEADS, HEAD_DIM, R), F32),
            pltpu.VMEM((HEADS, R, HEAD_DIM), BF16),
            pltpu.VMEM((HEADS, R, HEAD_DIM), F32),
            pltpu.VMEM((HEADS, R, HEAD_DIM), F32),
            pltpu.VMEM((HEADS, R, LANES), F32),
            pltpu.VMEM((HEADS, R, LANES), F32),
        ],
        compiler_params=pltpu.CompilerParams(
            dimension_semantics=("arbitrary",), vmem_limit_bytes=VMEM_LIMIT),
        name="mixer_sample",
    )(*operands)


def _ffn_sample_kernel(x_ref, p_ref, st_ref, g_pre_ref, w_up_ref, w_conv_ref, b_conv_ref, w_down_ref,
                       g_post_ref, g_ple_ref, w_ple_ref, w_pg_ref,
                       y_ref, tail_out_ref,
                       ubuf_ref, f_ref, *, seq_len):
    NB = SAMPLE_SEQS_FFN
    R = NB * SEQ_PAD
    x = x_ref[...]
    h = _rms(x, g_pre_ref[...]).astype(BF16)
    lo = SEQ_PAD - FFN_TAIL
    for c in range(D_FF // FFN_CHUNK):
        halves = []
        for half in range(2):
            c0 = half * D_FF + c * FFN_CHUNK
            cs = slice(c0, c0 + FFN_CHUNK)
            u = _dot(h, w_up_ref[:, cs])
            ubuf_ref[half, :, lo:SEQ_PAD, :] = st_ref[:, :, cs]
            ubuf_ref[half, :, SEQ_PAD:2 * SEQ_PAD, :] = u.reshape(NB, SEQ_PAD, FFN_CHUNK)
            tail_out_ref[:, :, cs] = ubuf_ref[half, :, lo + seq_len:SEQ_PAD + seq_len, :]
            y = (w_conv_ref[0:1, cs][None] * ubuf_ref[half, :, pl.ds(lo, SEQ_PAD), :]
                 + w_conv_ref[1:2, cs][None] * ubuf_ref[half, :, pl.ds(lo + 1, SEQ_PAD), :]
                 + w_conv_ref[2:3, cs][None] * ubuf_ref[half, :, pl.ds(lo + 2, SEQ_PAD), :]
                 + b_conv_ref[:, cs][None])
            halves.append(y.reshape(R, FFN_CHUNK))
        f_ref[:, c * FFN_CHUNK:(c + 1) * FFN_CHUNK] = (_gelu_tanh(halves[0]) * halves[1]).astype(BF16)
    y_ref[...] = _ffn_tail(x, f_ref, p_ref[...], w_down_ref, g_post_ref, g_ple_ref, w_ple_ref, w_pg_ref)


def _ffn_sample(x, p, st, wts, seq_len, layer):
    NB = SAMPLE_SEQS_FFN
    R = NB * SEQ_PAD
    nseq = st.shape[1]
    weights = (wts["g_ffn_pre"], wts["w_up"], wts["w_conv_ffn"], wts["b_conv_ffn"], wts["w_down"],
               wts["g_ffn_post"], wts["g_ple"], wts["w_ple"], wts["w_ple_gate"])
    st_spec = pl.BlockSpec((NB, FFN_TAIL, 2 * D_FF), lambda i: (i, 0, 0))
    st_in = pl.BlockSpec((None, NB, FFN_TAIL, 2 * D_FF), lambda i: (layer, i, 0, 0))
    return pl.pallas_call(
        functools.partial(_ffn_sample_kernel, seq_len=seq_len),
        grid=(nseq // NB,),
        in_specs=[pl.BlockSpec((R, D_MODEL), lambda i: (i, 0)), pl.BlockSpec((R, PLE_DIM), lambda i: (i, 0)),
                  st_in] + [_layer_block(w.shape, layer) for w in weights],
        out_specs=(pl.BlockSpec((R, D_MODEL), lambda i: (i, 0)), st_spec),
        out_shape=(jax.ShapeDtypeStruct(x.shape, F32), jax.ShapeDtypeStruct(st.shape[1:], F32)),
        scratch_shapes=[
            pltpu.VMEM((2, NB, 2 * SEQ_PAD, FFN_CHUNK), F32),
            pltpu.VMEM((R, D_FF), BF16),
        ],
        compiler_params=pltpu.CompilerParams(
            dimension_semantics=("arbitrary",), vmem_limit_bytes=VMEM_LIMIT),
        name="ffn_sample",
    )(x, p, st, *weights)


def _pad_seq(a):
    nseq, seq_len, width = a.shape
    return jnp.pad(a, ((0, 0), (0, SEQ_PAD - seq_len), (0, 0))).reshape(nseq * SEQ_PAD, width)


def _stacked_weights(g_mix_pre, w_in, b_igate, b_fgate, w_conv_mix, b_conv_mix, g_conv_norm,
                     b_conv_norm, g_mlstm_norm, w_out, g_mix_post, g_ffn_pre, w_up, w_conv_ffn,
                     b_conv_ffn, w_down, g_ffn_post, g_ple, w_ple, w_ple_gate):
    row = lambda v: v[:, None, :].astype(F32)
    n_gate = 2 * HEADS
    w_gate = jnp.pad(w_in[:, :, MAIN_COLS:], ((0, 0), (0, 0), (0, LANES - n_gate))).astype(BF16)
    b_gate = jnp.pad(jnp.concatenate([b_igate, b_fgate], axis=1), ((0, 0), (0, LANES - n_gate)))
    return {
        "g_mix_pre": row(g_mix_pre), "w_in": w_in[:, :, :MAIN_COLS].astype(BF16),
        "w_gate": w_gate, "b_gate": row(b_gate),
        "w_conv_mix": w_conv_mix.astype(F32), "b_conv_mix": row(b_conv_mix),
        "g_conv_norm": row(g_conv_norm), "b_conv_norm": row(b_conv_norm),
        "g_mlstm_norm": row(g_mlstm_norm), "w_out": w_out.astype(BF16), "g_mix_post": row(g_mix_post),
        "g_ffn_pre": row(g_ffn_pre), "w_up": w_up.astype(BF16), "w_conv_ffn": w_conv_ffn.astype(F32),
        "b_conv_ffn": row(b_conv_ffn), "w_down": w_down.astype(BF16), "g_ffn_post": row(g_ffn_post),
        "g_ple": row(g_ple), "w_ple": w_ple.astype(BF16), "w_ple_gate": w_ple_gate.astype(BF16),
    }


def kernel(x_prompt, x_sample, p_prompt, p_sample, state_conv_mix, state_mlstm_C, state_mlstm_n, state_mlstm_m, state_conv_ffn, g_mix_pre, w_in, b_igate, b_fgate, w_conv_mix, b_conv_mix, g_conv_norm, b_conv_norm, g_mlstm_norm, w_out, g_mix_post, g_ffn_pre, w_up, w_conv_ffn, b_conv_ffn, w_down, g_ffn_post, g_ple, w_ple, w_ple_gate):
    depth = w_in.shape[0]
    nseq, seq_len, _ = x_sample.shape
    assert FFN_TAIL <= seq_len <= SEQ_PAD and nseq % SAMPLE_SEQS == 0 and nseq % SAMPLE_SEQS_FFN == 0
    assert x_prompt.shape[1] % PROMPT_TILE == 0 and CONV_TAIL <= HIST <= PROMPT_TILE
    xp = x_prompt
    xs = _pad_seq(x_sample)
    pc, pC, pn, pm, pf = [], [], [], [], []
    sn, sm, sf = [], [], []
    stacks = None
    wts = _stacked_weights(g_mix_pre, w_in, b_igate, b_fgate, w_conv_mix, b_conv_mix, g_conv_norm,
                           b_conv_norm, g_mlstm_norm, w_out, g_mix_post, g_ffn_pre, w_up, w_conv_ffn,
                           b_conv_ffn, w_down, g_ffn_post, g_ple, w_ple, w_ple_gate)
    for l in range(depth):
        xp, c1, C1, n1, m1, f1 = _layer_prompt(xp, p_prompt, wts, l)
        pc.append(c1); pC.append(C1); pn.append(n1); pm.append(m1[:, :HEADS, 0]); pf.append(f1)

        mrow = jnp.pad(jnp.repeat(state_mlstm_m[l].astype(F32), SEQ_PAD, axis=0),
                       ((0, 0), (HEADS, LANES - 2 * HEADS)))
        xs, conv_stack, c_stack, n2, m2 = _mixer_sample(xs, state_conv_mix, state_mlstm_C, state_mlstm_n, mrow,
                                                        wts, seq_len, l, stacks)
        stacks = (conv_stack, c_stack)
        xs, f2 = _ffn_sample(xs, _pad_seq(p_sample[l]), state_conv_ffn, wts, seq_len, l)
        sn.append(n2); sf.append(f2)
        sm.append(m2.reshape(nseq, SEQ_PAD, LANES)[:, 0, :HEADS])
    ys = xs.reshape(nseq, SEQ_PAD, D_MODEL)[:, :seq_len]
    return (xp, ys, jnp.stack(pc), jnp.stack(pC), jnp.stack(pn), jnp.stack(pm), jnp.stack(pf),
            stacks[0], stacks[1], jnp.stack(sn), jnp.stack(sm), jnp.stack(sf))
```

```python
import functools

import jax
import jax.numpy as jnp
from jax import lax
from jax.experimental import pallas as pl
from jax.experimental.pallas import tpu as pltpu

F32 = jnp.float32
BF16 = jnp.bfloat16

D_MODEL = 1024
CONV_CH = 512
CONV_WIDTH = 31
CONV_TAIL = CONV_WIDTH - 1
CONV_GROUPS = 4
HEADS = 4
HEAD_DIM = 128
MLSTM_WIDTH = HEADS * HEAD_DIM
D_FF = 2816
FFN_TAIL = 2
PLE_DIM = 256
EPS = 1e-6
MAIN_COLS = 2 * CONV_CH + 4 * MLSTM_WIDTH
LANES = 128
SUBLANES = 8
NEG = -1e30

PROMPT_TILE = 256
SAMPLE_SEQS = 16
SAMPLE_SEQS_FFN = 32
SEQ_PAD = SUBLANES
HIST = 32
SHIFT_ROWS = PROMPT_TILE + HIST - SUBLANES
CONV_ROWS = 64
CONV_IN_FLIGHT = 2
FFN_CHUNK = 256
SAMPLE_EXT_ROWS = -(-(CONV_TAIL + SEQ_PAD) // SUBLANES) * SUBLANES
VMEM_LIMIT = 56 * 1024 * 1024


def _dot(a, b):
    return jnp.dot(a, b, preferred_element_type=F32)


def _dot_exact(sel, x):
    hi = x.astype(BF16)
    r1 = x - hi.astype(F32)
    mid = r1.astype(BF16)
    lo = (r1 - mid.astype(F32)).astype(BF16)
    y = _dot(jnp.where(sel, 1.0, 0.0).astype(BF16), jnp.concatenate([hi, mid, lo], axis=1))
    return y[:, 0:LANES] + y[:, LANES:2 * LANES] + y[:, 2 * LANES:3 * LANES]


def _rms(x, g):
    ms = jnp.mean(x * x, axis=-1, keepdims=True)
    return x * lax.rsqrt(ms + EPS) * g


def _layernorm(x):
    mu = jnp.mean(x, axis=-1, keepdims=True)
    xc = x - mu
    var = jnp.mean(xc * xc, axis=-1, keepdims=True)
    return xc * lax.rsqrt(var + EPS)


def _sigmoid(x):
    return 1.0 / (1.0 + jnp.exp(-x))


def _log_sigmoid(x):
    return jnp.minimum(x, 0.0) - jnp.log(1.0 + jnp.exp(-jnp.abs(x)))


def _exact_zero(v):
    bits = pltpu.bitcast(v, jnp.uint32)
    bits = lax.shift_right_logical(lax.shift_right_logical(bits, jnp.uint32(16)), jnp.uint32(16))
    return pltpu.bitcast(bits, F32)


def _gelu_tanh(x):
    return 0.5 * x * (1.0 + jnp.tanh(0.7978845608028654 * (x + 0.044715 * (x * x * x))))


def _conv_branch_post(acc, g_ref, b_ref):
    parts = []
    for g in range(CONV_GROUPS):
        sl = slice(g * LANES, (g + 1) * LANES)
        y = _layernorm(acc[:, sl]) * g_ref[:, sl] + b_ref[:, sl]
        parts.append(y * _sigmoid(y))
    return parts


def _qk(q_bf, k_bf):
    return lax.dot_general(q_bf, k_bf, (((1,), (1,)), ((), ())), preferred_element_type=F32)


def _mlstm_weighted(scores, v_bf, d, inter):
    m_t = jnp.maximum(inter, jnp.max(d, axis=1, keepdims=True))
    w_intra = jnp.exp(d - m_t)
    w_inter = jnp.exp(inter - m_t)
    s = scores * w_intra
    num = _dot(s.astype(BF16), v_bf)
    den = jnp.sum(s, axis=1, keepdims=True)
    return m_t, w_inter, num, den


def _head_out(num, den, m_t, g_mn, zo):
    hh = num / jnp.maximum(jnp.abs(den), jnp.exp(-m_t))
    return _layernorm(hh) * g_mn * _sigmoid(zo)


def _ffn_tail(x, f_ref, p, w_down_ref, g_post_ref, g_ple_ref, w_ple_ref, w_pg_ref):
    x2 = x + _rms(_dot(f_ref[...], w_down_ref[...]), g_post_ref[...])
    emb = _dot(p.astype(BF16), w_ple_ref[...])
    gate = _sigmoid(_dot(_rms(x2, g_ple_ref[...]).astype(BF16), w_pg_ref[...]))
    return x2 + emb * gate


def _layer_block(shape, layer):
    n = len(shape)
    return pl.BlockSpec((None,) + tuple(shape[1:]), lambda *_: (layer,) + (0,) * (n - 1),
                        pipeline_mode=pl.Buffered(1))


def _layer_prompt_kernel(x_ref, p_ref,
                         g_pre_ref, w_in_ref, w_gate_ref, b_gate_ref, w_cm_ref, b_cm_ref,
                         g_cn_ref, b_cn_ref, g_mn_ref, w_out_ref, g_post_ref,
                         g_fpre_ref, w_up_ref, w_cf_ref, b_cf_ref, w_down_ref, g_fpost_ref,
                         g_ple_ref, w_ple_ref, w_pg_ref,
                         y_ref, conv_out_ref, c_out_ref, n_out_ref, m_out_ref, tail_out_ref,
                         ext_ref, sh_ref, mix_ref, cn_ref, m_ref, h_ref, z_ref, x1_ref, hist_ref, fw_ref, f_ref,
                         *, tiles_per_seq):
    T = PROMPT_TILE
    G = T // SUBLANES
    i = pl.program_id(0)
    n_tiles = pl.num_programs(0) - 1
    s_mix = lax.rem(jnp.minimum(i, n_tiles - 1), tiles_per_seq)
    s_ffn = lax.rem(jnp.maximum(i - 1, 0), tiles_per_seq)
    slot = lax.rem(i, 2)

    @pl.when(i == 0)
    def _():
        x1_ref[...] = jnp.zeros(x1_ref.shape, F32)

    @pl.when(s_mix == 0)
    def _():
        ext_ref[0:HIST, :] = jnp.zeros((HIST, CONV_CH), F32)
        cn_ref[...] = jnp.zeros(cn_ref.shape, F32)
        m_ref[...] = jnp.zeros(m_ref.shape, F32)

    @pl.when(s_ffn == 0)
    def _():
        hist_ref[...] = jnp.zeros(hist_ref.shape, F32)

    wr = lax.broadcasted_iota(jnp.int32, (T, T), 0)
    wc = lax.broadcasted_iota(jnp.int32, (T, T), 1)
    to_work = jnp.where(wc == (wr % SUBLANES) * G + wr // SUBLANES, 1.0, 0.0).astype(BF16)
    to_token = jnp.where(wr == (wc % SUBLANES) * G + wc // SUBLANES, 1.0, 0.0).astype(BF16)

    def moved_down(group, prev_group):
        first = lax.broadcasted_iota(jnp.int32, group.shape, 0) == 0
        return jnp.where(first, pltpu.roll(prev_group, 1, 0), pltpu.roll(group, 1, 0))

    x = x_ref[...]
    h_ref[...] = _rms(x, g_pre_ref[...]).astype(BF16)
    zv = _dot(h_ref[...], w_in_ref[:, 0:CONV_CH])
    zg = _dot(h_ref[...], w_in_ref[:, CONV_CH:2 * CONV_CH])
    gates = _dot(h_ref[...], w_gate_ref[...]) + b_gate_ref[...]
    q_off = 2 * CONV_CH
    for part in range(4):
        ps = slice(part * MLSTM_WIDTH, (part + 1) * MLSTM_WIDTH)
        z_ref[slot, :, ps] = _dot(h_ref[...],
                                  w_in_ref[:, q_off + part * MLSTM_WIDTH:q_off + (part + 1) * MLSTM_WIDTH])

    ext_ref[HIST:HIST + T, :] = zv * _sigmoid(zg)
    for r in range(1, SUBLANES):
        sh_ref[r - 1] = ext_ref[pl.ds(r, SHIFT_ROWS), :]
    recent = []
    for g in range(CONV_GROUPS):
        cs = slice(g * LANES, (g + 1) * LANES)
        for rb in range(T // CONV_ROWS):
            acc = jnp.broadcast_to(b_cm_ref[:, cs], (CONV_ROWS, LANES))
            for j in range(CONV_WIDTH):
                off = HIST - CONV_TAIL + j
                r, base = off % SUBLANES, rb * CONV_ROWS + off - off % SUBLANES
                src = ext_ref if r == 0 else sh_ref.at[r - 1]
                w_row = w_cm_ref[j:j + 1, cs]
                if j == 0 and len(recent) == CONV_IN_FLIGHT:
                    w_row = w_row + _exact_zero(recent.pop(0))
                acc = acc + w_row * src[base:base + CONV_ROWS, cs]
            y = _layernorm(acc) * g_cn_ref[:, cs] + b_cn_ref[:, cs]
            mix_ref[rb * CONV_ROWS:(rb + 1) * CONV_ROWS, cs] = (y * _sigmoid(y)).astype(BF16)
            recent.append(acc[0:1, :])
    ext_ref[0:HIST, :] = ext_ref[T:T + HIST, :]

    x1 = x1_ref[1 - slot]
    h2 = _rms(x1, g_fpre_ref[...]).astype(BF16)
    hp = _dot(to_work, h2).astype(BF16)
    for c in range(D_FF // FFN_CHUNK):
        halves = []
        for half in range(2):
            c0 = half * D_FF + c * FFN_CHUNK
            cs = slice(c0, c0 + FFN_CHUNK)
            u = _dot(hp, w_up_ref[:, cs])
            prev = hist_ref[:, cs]
            hist_ref[:, cs] = u[T - 2 * SUBLANES:T, :]
            w_last = moved_down(u[T - SUBLANES:T, :], prev[SUBLANES:2 * SUBLANES, :])
            w_last2 = moved_down(u[T - 2 * SUBLANES:T - SUBLANES, :], prev[0:SUBLANES, :])
            u1 = jnp.concatenate([w_last, u[0:T - SUBLANES, :]], axis=0)
            u2 = jnp.concatenate([w_last2, w_last, u[0:T - 2 * SUBLANES, :]], axis=0)
            halves.append(w_cf_ref[0:1, cs] * u2 + w_cf_ref[1:2, cs] * u1
                          + w_cf_ref[2:3, cs] * u + b_cf_ref[:, cs])
        fw_ref[:, c * FFN_CHUNK:(c + 1) * FFN_CHUNK] = (_gelu_tanh(halves[0]) * halves[1]).astype(BF16)

    causal = wc <= wr
    bcum = _dot_exact(causal, _log_sigmoid(gates))
    gates_t = gates.T
    bcum_t = bcum.T
    heads = []
    for hd in range(HEADS):
        c0 = hd * HEAD_DIM
        zvv = z_ref[slot, :, 2 * MLSTM_WIDTH + c0:2 * MLSTM_WIDTH + c0 + HEAD_DIM]
        q_bf = (z_ref[slot, :, c0:c0 + HEAD_DIM] * (HEAD_DIM ** -0.5)).astype(BF16)
        k_bf = z_ref[slot, :, MLSTM_WIDTH + c0:MLSTM_WIDTH + c0 + HEAD_DIM].astype(BF16)
        cn = cn_ref[hd]
        scores = (_qk(q_bf[0:T // 2, :], k_bf[0:T // 2, :]), _qk(q_bf[T // 2:T, :], k_bf))
        heads.append((zvv, k_bf, cn, scores, _dot(q_bf, cn.astype(BF16))))

    finished = []
    for hd in range(HEADS):
        zvv, k_bf, cn, scores, carried = heads[hd]
        i_row = gates_t[hd:hd + 1, :]
        i_col = gates[:, hd:hd + 1]
        b_row = bcum_t[HEADS + hd:HEADS + hd + 1, :]
        b_col = bcum[:, HEADS + hd:HEADS + hd + 1]
        m_prev = m_ref[hd:hd + 1, 0:1]
        H = T // 2
        v_bf = zvv.astype(BF16)
        inter = b_col + m_prev
        d_top = jnp.where(causal[0:H, 0:H], b_col[0:H, :] - b_row[:, 0:H] + i_row[:, 0:H], NEG)
        d_bot = jnp.where(causal[H:T, :], b_col[H:T, :] - b_row + i_row, NEG)
        top = _mlstm_weighted(scores[0], v_bf[0:H, :], d_top, inter[0:H, :])
        bot = _mlstm_weighted(scores[1], v_bf, d_bot, inter[H:T, :])
        m_t, w_inter, num, den = [jnp.concatenate([a, b], axis=0) for a, b in zip(top, bot)]
        m_new = m_t[T - 1:T, :]
        ws = jnp.exp(b_col[T - 1:T, :] - b_col + i_col - m_new)
        vp = jnp.concatenate([ws * zvv, jnp.broadcast_to(ws, (T, HEAD_DIM))], axis=1).astype(BF16)
        kv = lax.dot_general(k_bf, vp, (((0,), (0,)), ((), ())), preferred_element_type=F32)
        finished.append((m_t, w_inter, num, den, m_new, w_inter[T - 1:T, :], kv))

    for hd in range(HEADS):
        c0 = hd * HEAD_DIM
        _, _, cn, _, carried = heads[hd]
        m_t, w_inter, num, den, m_new, decay, kv = finished[hd]
        zo = z_ref[slot, :, 3 * MLSTM_WIDTH + c0:3 * MLSTM_WIDTH + c0 + HEAD_DIM]
        num = num + carried[:, 0:HEAD_DIM] * w_inter
        den = den + carried[:, HEAD_DIM:HEAD_DIM + 1] * w_inter
        mix_ref[:, CONV_CH + c0:CONV_CH + c0 + HEAD_DIM] = _head_out(
            num, den, m_t, g_mn_ref[:, c0:c0 + HEAD_DIM], zo).astype(BF16)
        cn_ref[hd] = decay * cn + kv
        m_ref[hd:hd + 1, :] = jnp.broadcast_to(m_new, (1, LANES))

    for c in range(D_FF // FFN_CHUNK):
        fs = slice(c * FFN_CHUNK, (c + 1) * FFN_CHUNK)
        f_ref[:, fs] = _dot(to_token, fw_ref[:, fs]).astype(BF16)
    y_ref[...] = _ffn_tail(x1, f_ref, p_ref[...], w_down_ref, g_fpost_ref, g_ple_ref, w_ple_ref, w_pg_ref)

    x1_ref[slot] = x + _rms(_dot(mix_ref[...], w_out_ref[...]), g_post_ref[...])

    @pl.when((s_mix == tiles_per_seq - 1) & (i < n_tiles))
    def _():
        conv_out_ref[...] = ext_ref[pl.ds(HIST - CONV_TAIL, CONV_TAIL), :]
        m_out_ref[...] = m_ref[...]
        for hd in range(HEADS):
            cn = cn_ref[hd]
            c_out_ref[hd] = cn[:, 0:HEAD_DIM]
            n_out_ref[hd:hd + 1, :] = cn[:, HEAD_DIM:].T[0:1, :]

    @pl.when((s_ffn == tiles_per_seq - 1) & (i >= 1))
    def _():
        tail_out_ref[0:1, :] = hist_ref[SUBLANES - 1:SUBLANES, :]
        tail_out_ref[1:2, :] = hist_ref[2 * SUBLANES - 1:2 * SUBLANES, :]


def _layer_prompt(x, p, wts, layer):
    B, S, D = x.shape
    T = PROMPT_TILE
    tps = S // T
    n_tiles = B * tps
    weights = (wts["g_mix_pre"], wts["w_in"], wts["w_gate"], wts["b_gate"], wts["w_conv_mix"],
               wts["b_conv_mix"], wts["g_conv_norm"], wts["b_conv_norm"], wts["g_mlstm_norm"],
               wts["w_out"], wts["g_mix_post"],
               wts["g_ffn_pre"], wts["w_up"], wts["w_conv_ffn"], wts["b_conv_ffn"], wts["w_down"],
               wts["g_ffn_post"], wts["g_ple"], wts["w_ple"], wts["w_ple_gate"])
    mix_tile = lambda i: jnp.minimum(i, n_tiles - 1)
    ffn_tile = lambda i: jnp.maximum(i - 1, 0)
    out_shape = (
        jax.ShapeDtypeStruct((B, S, D), F32),
        jax.ShapeDtypeStruct((B, CONV_TAIL, CONV_CH), F32),
        jax.ShapeDtypeStruct((B, HEADS, HEAD_DIM, HEAD_DIM), F32),
        jax.ShapeDtypeStruct((B, HEADS, HEAD_DIM), F32),
        jax.ShapeDtypeStruct((B, SUBLANES, LANES), F32),
        jax.ShapeDtypeStruct((B, FFN_TAIL, 2 * D_FF), F32),
    )
    out_specs = (
        pl.BlockSpec((None, T, D), lambda i: (ffn_tile(i) // tps, ffn_tile(i) % tps, 0)),
        pl.BlockSpec((None, CONV_TAIL, CONV_CH), lambda i: (mix_tile(i) // tps, 0, 0)),
        pl.BlockSpec((None, HEADS, HEAD_DIM, HEAD_DIM), lambda i: (mix_tile(i) // tps, 0, 0, 0)),
        pl.BlockSpec((None, HEADS, HEAD_DIM), lambda i: (mix_tile(i) // tps, 0, 0)),
        pl.BlockSpec((None, SUBLANES, LANES), lambda i: (mix_tile(i) // tps, 0, 0)),
        pl.BlockSpec((None, FFN_TAIL, 2 * D_FF), lambda i: (ffn_tile(i) // tps, 0, 0)),
    )
    return pl.pallas_call(
        functools.partial(_layer_prompt_kernel, tiles_per_seq=tps),
        grid=(n_tiles + 1,),
        in_specs=[pl.BlockSpec((None, T, D), lambda i: (mix_tile(i) // tps, mix_tile(i) % tps, 0)),
                  pl.BlockSpec((None, None, T, PLE_DIM),
                               lambda i: (layer, ffn_tile(i) // tps, ffn_tile(i) % tps, 0))]
                 + [_layer_block(w.shape, layer) for w in weights],
        out_specs=out_specs,
        out_shape=out_shape,
        scratch_shapes=[
            pltpu.VMEM((HIST + T, CONV_CH), F32),
            pltpu.VMEM((SUBLANES - 1, SHIFT_ROWS, CONV_CH), F32),
            pltpu.VMEM((T, D_MODEL), BF16),
            pltpu.VMEM((HEADS, HEAD_DIM, 2 * HEAD_DIM), F32),
            pltpu.VMEM((SUBLANES, LANES), F32),
            pltpu.VMEM((T, D_MODEL), BF16),
            pltpu.VMEM((2, T, 4 * MLSTM_WIDTH), F32),
            pltpu.VMEM((2, T, D_MODEL), F32),
            pltpu.VMEM((2 * SUBLANES, 2 * D_FF), F32),
            pltpu.VMEM((T, D_FF), BF16),
            pltpu.VMEM((T, D_FF), BF16),
        ],
        compiler_params=pltpu.CompilerParams(
            dimension_semantics=("arbitrary",), vmem_limit_bytes=VMEM_LIMIT),
        name="layer_prompt",
    )(x, p, *weights)


def _mixer_sample_kernel(x_ref, st_ref, c_ref, n_ref, mrow_ref,
                         g_pre_ref, w_in_ref, w_gate_ref, b_gate_ref, w_conv_ref, b_conv_ref,
                         g_cn_ref, b_cn_ref, g_mn_ref, w_out_ref, g_post_ref,
                         y_ref, conv_out_ref, c_out_ref, n_out_ref, m_out_ref,
                         ext_ref, q_ref, kt_ref, wv_ref, wk_ref, carried_ref, qn_ref, dec_ref,
                         *, seq_len, stacked_first):
    NB = SAMPLE_SEQS
    R = NB * SEQ_PAD
    if stacked_first:
        c_dst, conv_dst = c_out_ref.at[0], conv_out_ref.at[0]
        for ref in (c_out_ref, conv_out_ref):
            ref[1:] = jnp.zeros((ref.shape[0] - 1,) + tuple(ref.shape[1:]), F32)
    else:
        c_dst, conv_dst = c_out_ref, conv_out_ref
    x = x_ref[...]
    h = _rms(x, g_pre_ref[...]).astype(BF16)

    zv = _dot(h, w_in_ref[:, 0:CONV_CH])
    zg = _dot(h, w_in_ref[:, CONV_CH:2 * CONV_CH])
    a = zv * _sigmoid(zg)
    ext_ref[:, 0:CONV_TAIL, :] = st_ref[...]
    ext_ref[:, CONV_TAIL:CONV_TAIL + SEQ_PAD, :] = a.reshape(NB, SEQ_PAD, CONV_CH)
    ext_ref[:, CONV_TAIL + SEQ_PAD:, :] = jnp.zeros((NB, SAMPLE_EXT_ROWS - CONV_TAIL - SEQ_PAD, CONV_CH), F32)
    acc = jnp.broadcast_to(b_conv_ref[...][None], (NB, SEQ_PAD, CONV_CH))
    for j in range(CONV_WIDTH):
        acc = acc + w_conv_ref[j:j + 1, :][None] * ext_ref[:, pl.ds(j, SEQ_PAD), :]
    conv_dst[...] = ext_ref[:, seq_len:seq_len + CONV_TAIL, :]
    mix_parts = _conv_branch_post(acc.reshape(R, CONV_CH), g_cn_ref, b_cn_ref)

    gates = _dot(h, w_gate_ref[...]) + b_gate_ref[...]
    row = lax.broadcasted_iota(jnp.int32, (R, R), 0)
    col = lax.broadcasted_iota(jnp.int32, (R, R), 1)
    same_seq = (row // SEQ_PAD) == (col // SEQ_PAD)
    causal = same_seq & (col <= row)
    bcum = _dot_exact(causal, _log_sigmoid(gates))
    mask = causal & ((col % SEQ_PAD) < seq_len)
    pick_last = same_seq & ((col % SEQ_PAD) == seq_len - 1)
    inter_all = bcum + mrow_ref[...]
    gates_t = gates.T
    bcum_t = bcum.T
    lane = lax.broadcasted_iota(jnp.int32, (R, LANES), 1)
    row_valid = (lax.broadcasted_iota(jnp.int32, (R, 1), 0) % SEQ_PAD) < seq_len
    stats = jnp.where((lane >= HEADS) & (lane < 2 * HEADS), bcum, 0.0)
    q_off = 2 * CONV_CH
    z_parts = [_dot(h, w_in_ref[:, q_off + part * MLSTM_WIDTH:q_off + (part + 1) * MLSTM_WIDTH])
               for part in range(4)]
    saved = []
    for hd in range(HEADS):
        c0 = hd * HEAD_DIM
        zq = z_parts[0][:, c0:c0 + HEAD_DIM] * (HEAD_DIM ** -0.5)
        zk = z_parts[1][:, c0:c0 + HEAD_DIM]
        zvv = z_parts[2][:, c0:c0 + HEAD_DIM]
        zo = z_parts[3][:, c0:c0 + HEAD_DIM]
        i_row = gates_t[hd:hd + 1, :]
        b_row = bcum_t[HEADS + hd:HEADS + hd + 1, :]
        b_col = bcum[:, HEADS + hd:HEADS + hd + 1]
        d = jnp.where(mask, b_col - b_row + i_row, NEG)
        inter = inter_all[:, HEADS + hd:HEADS + hd + 1]
        m_t, w_inter, num, den = _mlstm_weighted(_qk(zq.astype(BF16), zk.astype(BF16)), zvv.astype(BF16), d, inter)
        stats = jnp.where(lane == hd, m_t, stats)
        stats = jnp.where(lane == 2 * HEADS + hd, w_inter, stats)
        q_ref[hd, 0:R, :] = zq
        q_ref[hd, R:R + SEQ_PAD, :] = jnp.zeros((SEQ_PAD, HEAD_DIM), F32)
        kt_ref[hd] = zk.T
        saved.append((m_t, w_inter, num, den, zo, zk, zvv))

    per_seq = _dot_exact(pick_last, stats)
    m_out_ref[...] = per_seq
    for hd in range(HEADS):
        zk, zvv = saved[hd][5], saved[hd][6]
        m_new = per_seq[:, hd:hd + 1]
        b_last = per_seq[:, HEADS + hd:HEADS + hd + 1]
        decay = per_seq[:, 2 * HEADS + hd:2 * HEADS + hd + 1]
        b_col = bcum[:, HEADS + hd:HEADS + hd + 1]
        i_col = gates[:, hd:hd + 1]
        ws = jnp.where(row_valid, jnp.exp(b_last - b_col + i_col - m_new), 0.0)
        wv_ref[hd] = (ws * zvv).astype(BF16)
        wk_ref[hd] = ws * zk
        dec_ref[hd] = jnp.broadcast_to(decay, (R, LANES))

    col_seq = lax.broadcasted_iota(jnp.int32, (HEAD_DIM, R), 1) // SEQ_PAD

    def per_sequence(b, carry):
        r0 = pl.multiple_of(b * SEQ_PAD, SEQ_PAD)
        for hd in range(HEADS):
            c_old = c_ref[b, hd]
            n_old = n_ref[b, hd:hd + 1, :]
            q2 = q_ref[hd, pl.ds(r0, 2 * SEQ_PAD), :]
            carried_ref[hd, pl.ds(r0, SEQ_PAD), :] = _dot(q2.astype(BF16), c_old.astype(BF16))[0:SEQ_PAD, :]
            qn = jnp.sum(q2[0:SEQ_PAD, :] * n_old, axis=1, keepdims=True)
            qn_ref[hd, pl.ds(r0, SEQ_PAD), :] = jnp.broadcast_to(qn, (SEQ_PAD, LANES))
            dec = dec_ref[hd, pl.ds(r0, 1), :]
            kt_b = jnp.where(col_seq == b, kt_ref[hd], 0.0).astype(BF16)
            c_dst[b, hd] = dec * c_old + _dot(kt_b, wv_ref[hd])
            n_out_ref[b, hd:hd + 1, :] = dec * n_old + jnp.sum(wk_ref[hd, pl.ds(r0, SEQ_PAD), :], axis=0,
                                                                keepdims=True)
        return carry

    lax.fori_loop(0, NB, per_sequence, 0, unroll=4)

    for hd in range(HEADS):
        c0 = hd * HEAD_DIM
        m_t, w_inter, num, den, zo = saved[hd][:5]
        num = num + carried_ref[hd] * w_inter
        den = den + qn_ref[hd][:, 0:1] * w_inter
        mix_parts.append(_head_out(num, den, m_t, g_mn_ref[:, c0:c0 + HEAD_DIM], zo))

    mix = jnp.concatenate(mix_parts, axis=1).astype(BF16)
    y_ref[...] = x + _rms(_dot(mix, w_out_ref[...]), g_post_ref[...])


N_MIXER_SAMPLE_INPUTS = 16
N_STACKED = 2


def _mixer_sample_kernel_inplace(*refs, seq_len):
    _mixer_sample_kernel(*refs[:N_MIXER_SAMPLE_INPUTS], *refs[N_MIXER_SAMPLE_INPUTS + N_STACKED:],
                         seq_len=seq_len, stacked_first=False)


def _mixer_sample(x, st, c, n, mrow, wts, seq_len, layer, stacks):
    NB = SAMPLE_SEQS
    R = NB * SEQ_PAD
    depth, nseq = c.shape[0], c.shape[1]
    weights = (wts["g_mix_pre"], wts["w_in"], wts["w_gate"], wts["b_gate"], wts["w_conv_mix"],
               wts["b_conv_mix"], wts["g_conv_norm"], wts["b_conv_norm"], wts["g_mlstm_norm"],
               wts["w_out"], wts["g_mix_post"])
    rows = lambda width: pl.BlockSpec((R, width), lambda i: (i, 0))
    n_spec = pl.BlockSpec((NB, HEADS, HEAD_DIM), lambda i: (i, 0, 0))
    st_in = pl.BlockSpec((None, NB, CONV_TAIL, CONV_CH), lambda i: (layer, i, 0, 0))
    c_in = pl.BlockSpec((None, NB, HEADS, HEAD_DIM, HEAD_DIM), lambda i: (layer, i, 0, 0, 0))
    n_in = pl.BlockSpec((None, NB, HEADS, HEAD_DIM), lambda i: (layer, i, 0, 0))
    in_specs = [rows(D_MODEL), st_in, c_in, n_in, rows(LANES)] + [_layer_block(w.shape, layer) for w in weights]
    operands = (x, st, c, n, mrow, *weights)
    assert len(operands) == N_MIXER_SAMPLE_INPUTS
    if stacks is None:
        body = functools.partial(_mixer_sample_kernel, seq_len=seq_len, stacked_first=True)
        st_out_spec = pl.BlockSpec((depth, NB, CONV_TAIL, CONV_CH), lambda i: (0, i, 0, 0))
        c_out_spec = pl.BlockSpec((depth, NB, HEADS, HEAD_DIM, HEAD_DIM), lambda i: (0, i, 0, 0, 0))
        aliases = {}
    else:
        body = functools.partial(_mixer_sample_kernel_inplace, seq_len=seq_len)
        st_out_spec, c_out_spec = st_in, c_in
        in_specs += [pl.BlockSpec(memory_space=pl.ANY)] * N_STACKED
        operands = operands + tuple(stacks)
        aliases = {N_MIXER_SAMPLE_INPUTS: 1, N_MIXER_SAMPLE_INPUTS + 1: 2}
    return pl.pallas_call(
        body,
        grid=(nseq // NB,),
        in_specs=in_specs,
        out_specs=(rows(D_MODEL), st_out_spec, c_out_spec, n_spec, rows(LANES)),
        out_shape=(jax.ShapeDtypeStruct(x.shape, F32), jax.ShapeDtypeStruct(st.shape, F32),
                   jax.ShapeDtypeStruct(c.shape, F32), jax.ShapeDtypeStruct(n.shape[1:], F32),
                   jax.ShapeDtypeStruct(mrow.shape, F32)),
        input_output_aliases=aliases,
        scratch_shapes=[
            pltpu.VMEM((NB, SAMPLE_EXT_ROWS, CONV_CH), F32),
            pltpu.VMEM((HEADS, R + SEQ_PAD, HEAD_DIM), F32),
            pltpu.VMEM((HEADS, HEAD_DIM, R), F32),
            pltpu.VMEM((HEADS, R, HEAD_DIM), BF16),
            pltpu.VMEM((HEADS, R, HEAD_DIM), F32),
            pltpu.VMEM((HEADS, R, HEAD_DIM), F32),
            pltpu.VMEM((HEADS, R, LANES), F32),
            pltpu.VMEM((HEADS, R, LANES), F32),
        ],
        compiler_params=pltpu.CompilerParams(
            dimension_semantics=("arbitrary",), vmem_limit_bytes=VMEM_LIMIT),
        name="mixer_sample",
    )(*operands)


def _ffn_sample_kernel(x_ref, p_ref, st_ref, g_pre_ref, w_up_ref, w_conv_ref, b_conv_ref, w_down_ref,
                       g_post_ref, g_ple_ref, w_ple_ref, w_pg_ref,
                       y_ref, tail_out_ref,
                       ubuf_ref, f_ref, *, seq_len):
    NB = SAMPLE_SEQS_FFN
    R = NB * SEQ_PAD
    x = x_ref[...]
    h = _rms(x, g_pre_ref[...]).astype(BF16)
    lo = SEQ_PAD - FFN_TAIL
    for c in range(D_FF // FFN_CHUNK):
        halves = []
        for half in range(2):
            c0 = half * D_FF + c * FFN_CHUNK
            cs = slice(c0, c0 + FFN_CHUNK)
            u = _dot(h, w_up_ref[:, cs])
            ubuf_ref[half, :, lo:SEQ_PAD, :] = st_ref[:, :, cs]
            ubuf_ref[half, :, SEQ_PAD:2 * SEQ_PAD, :] = u.reshape(NB, SEQ_PAD, FFN_CHUNK)
            tail_out_ref[:, :, cs] = ubuf_ref[half, :, lo + seq_len:SEQ_PAD + seq_len, :]
            y = (w_conv_ref[0:1, cs][None] * ubuf_ref[half, :, pl.ds(lo, SEQ_PAD), :]
                 + w_conv_ref[1:2, cs][None] * ubuf_ref[half, :, pl.ds(lo + 1, SEQ_PAD), :]
                 + w_conv_ref[2:3, cs][None] * ubuf_ref[half, :, pl.ds(lo + 2, SEQ_PAD), :]
                 + b_conv_ref[:, cs][None])
            halves.append(y.reshape(R, FFN_CHUNK))
        f_ref[:, c * FFN_CHUNK:(c + 1) * FFN_CHUNK] = (_gelu_tanh(halves[0]) * halves[1]).astype(BF16)
    y_ref[...] = _ffn_tail(x, f_ref, p_ref[...], w_down_ref, g_post_ref, g_ple_ref, w_ple_ref, w_pg_ref)


def _ffn_sample(x, p, st, wts, seq_len, layer):
    NB = SAMPLE_SEQS_FFN
    R = NB * SEQ_PAD
    nseq = st.shape[1]
    weights = (wts["g_ffn_pre"], wts["w_up"], wts["w_conv_ffn"], wts["b_conv_ffn"], wts["w_down"],
               wts["g_ffn_post"], wts["g_ple"], wts["w_ple"], wts["w_ple_gate"])
    st_spec = pl.BlockSpec((NB, FFN_TAIL, 2 * D_FF), lambda i: (i, 0, 0))
    st_in = pl.BlockSpec((None, NB, FFN_TAIL, 2 * D_FF), lambda i: (layer, i, 0, 0))
    return pl.pallas_call(
        functools.partial(_ffn_sample_kernel, seq_len=seq_len),
        grid=(nseq // NB,),
        in_specs=[pl.BlockSpec((R, D_MODEL), lambda i: (i, 0)), pl.BlockSpec((R, PLE_DIM), lambda i: (i, 0)),
                  st_in] + [_layer_block(w.shape, layer) for w in weights],
        out_specs=(pl.BlockSpec((R, D_MODEL), lambda i: (i, 0)), st_spec),
        out_shape=(jax.ShapeDtypeStruct(x.shape, F32), jax.ShapeDtypeStruct(st.shape[1:], F32)),
        scratch_shapes=[
            pltpu.VMEM((2, NB, 2 * SEQ_PAD, FFN_CHUNK), F32),
            pltpu.VMEM((R, D_FF), BF16),
        ],
        compiler_params=pltpu.CompilerParams(
            dimension_semantics=("arbitrary",), vmem_limit_bytes=VMEM_LIMIT),
        name="ffn_sample",
    )(x, p, st, *weights)


def _pad_seq(a):
    nseq, seq_len, width = a.shape
    return jnp.pad(a, ((0, 0), (0, SEQ_PAD - seq_len), (0, 0))).reshape(nseq * SEQ_PAD, width)


def _stacked_weights(g_mix_pre, w_in, b_igate, b_fgate, w_conv_mix, b_conv_mix, g_conv_norm,
                     b_conv_norm, g_mlstm_norm, w_out, g_mix_post, g_ffn_pre, w_up, w_conv_ffn,
                     b_conv_ffn, w_down, g_ffn_post, g_ple, w_ple, w_ple_gate):
    row = lambda v: v[:, None, :].astype(F32)
    n_gate = 2 * HEADS
    w_gate = jnp.pad(w_in[:, :, MAIN_COLS:], ((0, 0), (0, 0), (0, LANES - n_gate))).astype(BF16)
    b_gate = jnp.pad(jnp.concatenate([b_igate, b_fgate], axis=1), ((0, 0), (0, LANES - n_gate)))
    return {
        "g_mix_pre": row(g_mix_pre), "w_in": w_in[:, :, :MAIN_COLS].astype(BF16),
        "w_gate": w_gate, "b_gate": row(b_gate),
        "w_conv_mix": w_conv_mix.astype(F32), "b_conv_mix": row(b_conv_mix),
        "g_conv_norm": row(g_conv_norm), "b_conv_norm": row(b_conv_norm),
        "g_mlstm_norm": row(g_mlstm_norm), "w_out": w_out.astype(BF16), "g_mix_post": row(g_mix_post),
        "g_ffn_pre": row(g_ffn_pre), "w_up": w_up.astype(BF16), "w_conv_ffn": w_conv_ffn.astype(F32),
        "b_conv_ffn": row(b_conv_ffn), "w_down": w_down.astype(BF16), "g_ffn_post": row(g_ffn_post),
        "g_ple": row(g_ple), "w_ple": w_ple.astype(BF16), "w_ple_gate": w_ple_gate.astype(BF16),
    }


def kernel(x_prompt, x_sample, p_prompt, p_sample, state_conv_mix, state_mlstm_C, state_mlstm_n, state_mlstm_m, state_conv_ffn, g_mix_pre, w_in, b_igate, b_fgate, w_conv_mix, b_conv_mix, g_conv_norm, b_conv_norm, g_mlstm_norm, w_out, g_mix_post, g_ffn_pre, w_up, w_conv_ffn, b_conv_ffn, w_down, g_ffn_post, g_ple, w_ple, w_ple_gate):
    depth = w_in.shape[0]
    nseq, seq_len, _ = x_sample.shape
    assert FFN_TAIL <= seq_len <= SEQ_PAD and nseq % SAMPLE_SEQS == 0 and nseq % SAMPLE_SEQS_FFN == 0
    assert x_prompt.shape[1] % PROMPT_TILE == 0 and CONV_TAIL <= HIST <= PROMPT_TILE
    xp = x_prompt
    xs = _pad_seq(x_sample)
    pc, pC, pn, pm, pf = [], [], [], [], []
    sn, sm, sf = [], [], []
    stacks = None
    wts = _stacked_weights(g_mix_pre, w_in, b_igate, b_fgate, w_conv_mix, b_conv_mix, g_conv_norm,
                           b_conv_norm, g_mlstm_norm, w_out, g_mix_post, g_ffn_pre, w_up, w_conv_ffn,
                           b_conv_ffn, w_down, g_ffn_post, g_ple, w_ple, w_ple_gate)
    for l in range(depth):
        xp, c1, C1, n1, m1, f1 = _layer_prompt(xp, p_prompt, wts, l)
        pc.append(c1); pC.append(C1); pn.append(n1); pm.append(m1[:, :HEADS, 0]); pf.append(f1)

        mrow = jnp.pad(jnp.repeat(state_mlstm_m[l].astype(F32), SEQ_PAD, axis=0),
                       ((0, 0), (HEADS, LANES - 2 * HEADS)))
        xs, conv_stack, c_stack, n2, m2 = _mixer_sample(xs, state_conv_mix, state_mlstm_C, state_mlstm_n, mrow,
                                                        wts, seq_len, l, stacks)
        stacks = (conv_stack, c_stack)
        xs, f2 = _ffn_sample(xs, _pad_seq(p_sample[l]), state_conv_ffn, wts, seq_len, l)
        sn.append(n2); sf.append(f2)
        sm.append(m2.reshape(nseq, SEQ_PAD, LANES)[:, 0, :HEADS])
    ys = xs.reshape(nseq, SEQ_PAD, D_MODEL)[:, :seq_len]
    return (xp, ys, jnp.stack(pc), jnp.stack(pC), jnp.stack(pn), jnp.stack(pm), jnp.stack(pf),
            stacks[0], stacks[1], jnp.stack(sn), jnp.stack(sm), jnp.stack(sf))
```
